```python
import math
import jax, jax.numpy as jnp
from jax import lax
import numpy as np

D_MODEL = 1024
BATCH = 2
SEQ = 16384
DEPTH = 2

BLOCK = 128
A_HEADS = 4
A_QK = 32
A_V = 2 * A_QK
SB_HEADS = 4
SB_DIM = 64
SWA_Q_HEADS = 8
SWA_KV_HEADS = 2
SWA_REP = SWA_Q_HEADS // SWA_KV_HEADS
SWA_DIM = 64
WINDOW = 128
A_Q_W = A_HEADS * 2 * A_QK
A_K_W = A_HEADS * 2 * A_QK
A_V_W = A_HEADS * A_V
SB_W = SB_HEADS * SB_DIM
SWA_Q_W = SWA_Q_HEADS * SWA_DIM
SWA_KV_W = SWA_KV_HEADS * SWA_DIM
IN_WIDTH = A_Q_W + A_K_W + A_V_W + 3 * SB_W + SWA_Q_W + 2 * SWA_KV_W
MIX_WIDTH = A_V_W + SB_W + SWA_Q_W
D_FF = ((8 * D_MODEL // 3 + 255) // 256) * 256
N_MOD = 6
EPS = 1e-6

kernel_name = "hymba_diff_stickbreak_swa_hybrid"


def _rmsnorm(x, g):
    xf = x.astype(jnp.float32)
    y = xf * lax.rsqrt(jnp.mean(xf * xf, axis=-1, keepdims=True) + EPS)
    return (y * g.astype(jnp.float32)).astype(x.dtype)


def _alibi_slopes(n):
    return 2.0 ** (-8.0 * jnp.arange(1, n + 1, dtype=jnp.float32) / n)


def _diff_attention(q, k, v, lam, slopes):
    b, s, h = q.shape[:3]
    nb = s // BLOCK
    qb = jnp.moveaxis(q.reshape(b, nb, BLOCK, h, 2, A_QK), 1, 0)
    key_pos = jnp.arange(s)
    scale = A_QK ** -0.5

    def one_block(args):
        q_blk, i = args
        qpos = i * BLOCK + jnp.arange(BLOCK)
        dist = qpos[:, None] - key_pos[None, :]
        logits = jnp.einsum('bqhmd,bkhmd->bhmqk', q_blk, k,
                            preferred_element_type=jnp.float32) * scale
        logits = logits - slopes[None, :, None, None, None] * dist.astype(jnp.float32)
        logits = jnp.where(dist >= 0, logits, -jnp.inf)
        p = jax.nn.softmax(logits, axis=-1)
        w = p[:, :, 0] - lam * p[:, :, 1]
        return jnp.einsum('bhqk,bkhe->bqhe', w.astype(v.dtype), v)

    out = lax.map(one_block, (qb, jnp.arange(nb)))
    return jnp.moveaxis(out, 0, 1).reshape(b, s, h, A_V)


def _stick_breaking_attention(q, k, v):
    b, s, h, d = q.shape
    nb = s // BLOCK
    qb = jnp.moveaxis(q.reshape(b, nb, BLOCK, h, d), 1, 0)
    key_pos = jnp.arange(s)
    scale = d ** -0.5

    def one_block(args):
        q_blk, i = args
        qpos = i * BLOCK + jnp.arange(BLOCK)
        strict = key_pos[None, :] < qpos[:, None]
        z = jnp.einsum('bqhd,bkhd->bhqk', q_blk, k,
                       preferred_element_type=jnp.float32) * scale
        log_beta = jax.nn.log_sigmoid(z)
        log_1mb = jnp.where(strict, jax.nn.log_sigmoid(-z), 0.0)
        suffix = lax.cumsum(log_1mb, axis=3, reverse=True)
        log_a = log_beta + suffix - log_1mb
        a = jnp.where(strict, jnp.exp(log_a), 0.0)
        return jnp.einsum('bhqk,bkhd->bqhd', a.astype(v.dtype), v)

    out = lax.map(one_block, (qb, jnp.arange(nb)))
    return jnp.moveaxis(out, 0, 1).reshape(b, s, h * d)


def _swa_sink_attention(q, k, v, slopes, sinks):
    b, s, g, r, d = q.shape
    nb = s // BLOCK
    qb = q.reshape(b, nb, BLOCK, g, r, d)

    def with_prev(t):
        tb = t.reshape(b, nb, BLOCK, g, d)
        prev = jnp.pad(tb[:, :-1], ((0, 0), (1, 0), (0, 0), (0, 0), (0, 0)))
        return jnp.concatenate([prev, tb], axis=2)

    kw, vw = with_prev(k), with_prev(v)
    logits = jnp.einsum('bnqgrd,bnkgd->bgrnqk', qb, kw,
                        preferred_element_type=jnp.float32) * (d ** -0.5)
    qpos = BLOCK + jnp.arange(BLOCK)
    kpos = jnp.arange(2 * BLOCK)
    dist = qpos[:, None] - kpos[None, :]
    in_window = (dist >= 0) & (dist < WINDOW)
    key_abs = jnp.arange(nb)[:, None] * BLOCK - BLOCK + kpos[None, :]
    valid = in_window[None, :, :] & (key_abs >= 0)[:, None, :]
    logits = logits - slopes[:, :, None, None, None] * dist.astype(jnp.float32)
    logits = jnp.where(valid, logits, -jnp.inf)
    sink = jnp.broadcast_to(sinks.astype(jnp.float32)[None, :, :, None, None, None],
                            logits.shape[:-1] + (1,))
    p = jax.nn.softmax(jnp.concatenate([logits, sink], axis=-1), axis=-1)[..., :-1]
    out = jnp.einsum('bgrnqk,bnkgd->bnqgrd', p.astype(v.dtype), vw)
    return out.reshape(b, s, g * r * d)


def setup_inputs(seed: int = 0) -> dict:
    key = jax.random.key(seed)
    ks = jax.random.split(key, 20)
    f32 = jnp.float32
    nrm = lambda k, shape, std: jax.random.normal(k, shape, f32) * std
    return {
        "x": nrm(ks[0], (BATCH, SEQ, D_MODEL), 1.0),
        "c": nrm(ks[1], (BATCH, D_MODEL), 1.0),
        "ln1_g": 1.0 + nrm(ks[2], (DEPTH, D_MODEL), 0.02),
        "ln2_g": 1.0 + nrm(ks[3], (DEPTH, D_MODEL), 0.02),
        "w_mod": nrm(ks[4], (DEPTH, D_MODEL, N_MOD * D_MODEL), 0.5 * D_MODEL ** -0.5),
        "b_mod": nrm(ks[5], (DEPTH, N_MOD * D_MODEL), 0.02),
        "w_in": nrm(ks[6], (DEPTH, D_MODEL, IN_WIDTH), D_MODEL ** -0.5),
        "lam_q1": nrm(ks[7], (DEPTH, A_QK), 0.1),
        "lam_k1": nrm(ks[8], (DEPTH, A_QK), 0.1),
        "lam_q2": nrm(ks[9], (DEPTH, A_QK), 0.1),
        "lam_k2": nrm(ks[10], (DEPTH, A_QK), 0.1),
        "diff_norm_g": 1.0 + nrm(ks[11], (DEPTH, A_V), 0.02),
        "sb_norm_g": 1.0 + nrm(ks[12], (DEPTH, SB_W), 0.02),
        "swa_norm_g": 1.0 + nrm(ks[13], (DEPTH, SWA_Q_W), 0.02),
        "swa_sinks": nrm(ks[14], (DEPTH, SWA_Q_HEADS), 0.5),
        "w_out": nrm(ks[15], (DEPTH, MIX_WIDTH, D_MODEL), MIX_WIDTH ** -0.5),
        "w_gate": nrm(ks[16], (DEPTH, D_MODEL, D_FF), D_MODEL ** -0.5),
        "w_up": nrm(ks[17], (DEPTH, D_MODEL, D_FF), D_MODEL ** -0.5),
        "w_down": nrm(ks[18], (DEPTH, D_FF, D_MODEL), D_FF ** -0.5),
        "final_g": 1.0 + nrm(ks[19], (D_MODEL,), 0.02),
    }


def reference(x, c, ln1_g, ln2_g, w_mod, b_mod, w_in, lam_q1, lam_k1, lam_q2, lam_k2,
              diff_norm_g, sb_norm_g, swa_norm_g, swa_sinks, w_out, w_gate, w_up,
              w_down, final_g):
    b, s, _ = x.shape
    slopes_a = _alibi_slopes(A_HEADS)
    slopes_c = _alibi_slopes(SWA_Q_HEADS).reshape(SWA_KV_HEADS, SWA_REP)
    split_at = list(np.cumsum([A_Q_W, A_K_W, A_V_W, SB_W, SB_W, SB_W,
                               SWA_Q_W, SWA_KV_W]))
    c_act = jax.nn.silu(c)
    for l in range(DEPTH):
        mod = c_act @ w_mod[l] + b_mod[l]
        sh1, sc1, g1, sh2, sc2, g2 = jnp.split(mod, N_MOD, axis=-1)

        h = _rmsnorm(x, ln1_g[l]) * (1.0 + sc1[:, None]) + sh1[:, None]
        proj = h @ w_in[l]
        qa, ka, va, qb_, kb_, vb_, qc, kc, vc = jnp.split(proj, split_at, axis=-1)

        lam_init = 0.8 - 0.6 * math.exp(-0.3 * l)
        lam = (jnp.exp(jnp.sum(lam_q1[l].astype(jnp.float32) * lam_k1[l]))
               - jnp.exp(jnp.sum(lam_q2[l].astype(jnp.float32) * lam_k2[l])) + lam_init)
        ya = _diff_attention(qa.reshape(b, s, A_HEADS, 2, A_QK),
                             ka.reshape(b, s, A_HEADS, 2, A_QK),
                             va.reshape(b, s, A_HEADS, A_V), lam, slopes_a)
        ya = (_rmsnorm(ya, diff_norm_g[l]) * (1.0 - lam_init)).reshape(b, s, A_V_W)

        yb = _stick_breaking_attention(qb_.reshape(b, s, SB_HEADS, SB_DIM),
                                       kb_.reshape(b, s, SB_HEADS, SB_DIM),
                                       vb_.reshape(b, s, SB_HEADS, SB_DIM))
        yb = _rmsnorm(yb, sb_norm_g[l])

        yc = _swa_sink_attention(qc.reshape(b, s, SWA_KV_HEADS, SWA_REP, SWA_DIM),
                                 kc.reshape(b, s, SWA_KV_HEADS, SWA_DIM),
                                 vc.reshape(b, s, SWA_KV_HEADS, SWA_DIM),
                                 slopes_c,
                                 swa_sinks[l].reshape(SWA_KV_HEADS, SWA_REP))
        yc = _rmsnorm(yc, swa_norm_g[l])

        mixed = jnp.concatenate([ya, yb, yc], axis=-1) @ w_out[l]
        x = x + g1[:, None] * mixed

        h2 = _rmsnorm(x, ln2_g[l]) * (1.0 + sc2[:, None]) + sh2[:, None]
        ffn = (jax.nn.silu(h2 @ w_gate[l]) * (h2 @ w_up[l])) @ w_down[l]
        x = x + g2[:, None] * ffn
    return _rmsnorm(x, final_g)
```

```python
import functools
import math

import jax
import jax.numpy as jnp
from jax import lax
from jax.experimental import pallas as pl
from jax.experimental.pallas import tpu as pltpu

F32 = jnp.float32
BF16 = jnp.bfloat16

N_MOD = 6
EPS = 1e-6
A_HEADS = 4
A_QK = 32
A_V = 64
SB_HEADS = 4
SB_DIM = 64
SWA_Q_HEADS = 8
SWA_KV_HEADS = 2
SWA_REP = 4
SWA_DIM = 64
WINDOW = 128
A_W = 256
SB_W = 256
SWA_Q_W = 512
SWA_KV_W = 128

LANES = 128
NEG = -1e30
SB_DONE = -110.0

VMEM_LIMIT = 56 * 1024 * 1024

ROW_TILE = 512
FFN_TILE = 256
ATT_TILE = 256


def _params(*sem):
    return pltpu.CompilerParams(dimension_semantics=sem, vmem_limit_bytes=VMEM_LIMIT)


def _nt_dot(a, b):
    return lax.dot_general(a, b, (((1,), (1,)), ((), ())), preferred_element_type=F32)


def _dot(a, b):
    return jnp.dot(a, b, preferred_element_type=F32)


def _rms(x, g):
    ms = jnp.mean(x * x, axis=-1, keepdims=True)
    return x * lax.rsqrt(ms + EPS) * g


def _mod_kernel(c_ref, w_ref, b_ref, o_ref):
    cv = c_ref[...]
    ca = cv * jax.nn.sigmoid(cv)
    o_ref[0] = jnp.dot(ca, w_ref[0], preferred_element_type=F32,
                       precision=lax.Precision.HIGHEST) + b_ref[0]


def _modulation(c, w_mod, b_mod):
    depth, d, n = w_mod.shape
    b = c.shape[0]
    rows = 8
    cp = jnp.zeros((rows, d), F32).at[:b].set(c)
    tn = 1024
    out = pl.pallas_call(
        _mod_kernel,
        grid=(depth, n // tn),
        in_specs=[
            pl.BlockSpec((rows, d), lambda l, j: (0, 0)),
            pl.BlockSpec((1, d, tn), lambda l, j: (l, 0, j)),
            pl.BlockSpec((1, 1, tn), lambda l, j: (l, 0, j)),
        ],
        out_specs=pl.BlockSpec((1, rows, tn), lambda l, j: (l, 0, j)),
        out_shape=jax.ShapeDtypeStruct((depth, rows, n), F32),
        compiler_params=_params("parallel", "parallel"),
        name="modulation",
    )(cp, w_mod, b_mod.reshape(depth, 1, n))
    return out[:, :b].reshape(depth, b, N_MOD, d)


IN_SPLITS = (A_W, A_W, A_W, SB_W, SB_W, SB_W, SWA_Q_W, SWA_KV_W, SWA_KV_W)


def _in_proj_kernel(x_ref, g_ref, mod_ref, w_ref, *out_refs):
    x = x_ref[0]
    h = _rms(x, g_ref[...]) * (1.0 + mod_ref[0, 1:2, :]) + mod_ref[0, 0:1, :]
    proj = _dot(h.astype(BF16), w_ref[...])
    start = 0
    for ref, width in zip(out_refs, IN_SPLITS):
        ref[0] = proj[:, start:start + width].astype(BF16)
        start += width


def _in_proj(x, g, mod, w):
    b, s, d = x.shape
    tm = min(ROW_TILE, s)
    n = w.shape[1]
    return pl.pallas_call(
        _in_proj_kernel,
        grid=(b, s // tm),
        in_specs=[
            pl.BlockSpec((1, tm, d), lambda bi, i: (bi, i, 0)),
            pl.BlockSpec((1, d), lambda bi, i: (0, 0)),
            pl.BlockSpec((1, N_MOD, d), lambda bi, i: (bi, 0, 0)),
            pl.BlockSpec((d, n), lambda bi, i: (0, 0)),
        ],
        out_specs=[pl.BlockSpec((1, tm, wd), lambda bi, i: (bi, i, 0)) for wd in IN_SPLITS],
        out_shape=[jax.ShapeDtypeStruct((b, s, wd), BF16) for wd in IN_SPLITS],
        compiler_params=_params("parallel", "parallel"),
        name="in_proj",
    )(x, g.reshape(1, d), mod, w)


def _diff_attn_kernel(slopes_ref, lq1_ref, lk1_ref, lq2_ref, lk2_ref, q_ref, k_ref, v_ref,
                      o_ref, m_s, l_s, acc_s, *, tq, lam_init):
    pair = pl.program_id(1)
    i = pl.program_id(2)
    q = q_ref[0]
    seg = lax.broadcasted_iota(jnp.int32, (tq, LANES), 1) // A_QK
    qm = [jnp.where(seg == a, q, jnp.zeros_like(q)) for a in range(4)]
    rel_i = (lax.broadcasted_iota(jnp.int32, (tq, tq), 0)
             - lax.broadcasted_iota(jnp.int32, (tq, tq), 1))
    rel = rel_i.astype(F32)

    m_s[...] = jnp.full(m_s.shape, NEG, F32)
    l_s[...] = jnp.zeros(l_s.shape, F32)
    acc_s[...] = jnp.zeros(acc_s.shape, F32)

    def block(j, diagonal):
        start = pl.multiple_of(j * tq, tq)
        kj = k_ref[0, pl.ds(start, tq), :]
        vj = v_ref[0, pl.ds(start, tq), :]
        blocks_apart = ((i - j) * tq).astype(F32)
        for a in range(4):
            slope = slopes_ref[pair * 2 + a // 2]
            off = slope * blocks_apart
            s = _nt_dot(qm[a], kj) - slope * rel
            if diagonal:
                s = jnp.where(rel_i >= 0, s, NEG)
            m_old = m_s[a]
            m_new = jnp.maximum(m_old, jnp.max(s, axis=1, keepdims=True) - off)
            p = jnp.exp(s - (m_new + off))
            alpha = jnp.exp(m_old - m_new)
            l_s[a] = alpha * l_s[a] + jnp.sum(p, axis=1, keepdims=True)
            acc_s[a] = alpha * acc_s[a] + _dot(p.astype(BF16), vj)
            m_s[a] = m_new

    def body(j, carry):
        block(j, False)
        return carry

    lax.fori_loop(0, i, body, 0)
    block(i, True)

    lam = (jnp.exp(jnp.sum(lq1_ref[...] * lk1_ref[...], keepdims=True))
           - jnp.exp(jnp.sum(lq2_ref[...] * lk2_ref[...], keepdims=True)) + lam_init)
    o = [acc_s[a] / l_s[a] for a in range(4)]
    lane = lax.broadcasted_iota(jnp.int32, (tq, LANES), 1)
    o_ref[0] = jnp.where(lane < A_V, o[0] - lam * o[1], o[2] - lam * o[3])


def _diff_attn(q, k, v, lq1, lk1, lq2, lk2, lam_init):
    b, s, _ = q.shape
    tq = min(ATT_TILE, s)
    pairs = A_W // LANES
    slopes = 2.0 ** (-8.0 * jnp.arange(1, A_HEADS + 1, dtype=F32) / A_HEADS)
    smem = pl.BlockSpec(memory_space=pltpu.SMEM)
    vec = pl.BlockSpec((1, A_QK), lambda bi, p, i: (0, 0))
    return pl.pallas_call(
        functools.partial(_diff_attn_kernel, tq=tq, lam_init=lam_init),
        grid=(b, pairs, s // tq),
        in_specs=[
            smem, vec, vec, vec, vec,
            pl.BlockSpec((1, tq, LANES), lambda bi, p, i: (bi, i, p)),
            pl.BlockSpec((1, s, LANES), lambda bi, p, i: (bi, 0, p)),
            pl.BlockSpec((1, s, LANES), lambda bi, p, i: (bi, 0, p)),
        ],
        out_specs=pl.BlockSpec((1, tq, LANES), lambda bi, p, i: (bi, i, p)),
        out_shape=jax.ShapeDtypeStruct((b, s, A_W), F32),
        scratch_shapes=[
            pltpu.VMEM((4, tq, 1), F32),
            pltpu.VMEM((4, tq, 1), F32),
            pltpu.VMEM((4, tq, LANES), F32),
        ],
        compiler_params=_params("parallel", "parallel", "arbitrary"),
        name="diff_attn",
    )(slopes, lq1.reshape(1, A_QK), lk1.reshape(1, A_QK), lq2.reshape(1, A_QK),
      lk2.reshape(1, A_QK), q, k, v)


def _split3(x):
    hi = x.astype(BF16)
    r1 = x - hi.astype(F32)
    mid = r1.astype(BF16)
    lo = (r1 - mid.astype(F32)).astype(BF16)
    return hi, mid, lo


def _sb_attn_kernel(q_ref, k_ref, v_ref, o_ref, r_s, acc_s, *, tq):
    i = pl.program_id(2)
    q = q_ref[0]
    lane = lax.broadcasted_iota(jnp.int32, (tq, LANES), 1)
    qm = [jnp.where(lane < SB_DIM, q, jnp.zeros_like(q)),
          jnp.where(lane >= SB_DIM, q, jnp.zeros_like(q))]
    row = lax.broadcasted_iota(jnp.int32, (tq, tq), 0)
    col = lax.broadcasted_iota(jnp.int32, (tq, tq), 1)
    strict = col < row
    later = (row > col).astype(BF16)

    r_s[...] = jnp.zeros(r_s.shape, F32)
    acc_s[...] = jnp.zeros(acc_s.shape, F32)

    def block(j, diagonal):
        start = pl.multiple_of(j * tq, tq)
        kj = k_ref[0, pl.ds(start, tq), :]
        vj = v_ref[0, pl.ds(start, tq), :]
        for h in range(2):
            z = _nt_dot(qm[h], kj)
            log_beta = jnp.minimum(z, 0.0) - jnp.log(1.0 + jnp.exp(-jnp.abs(z)))
            log_1mb = log_beta - z
            if diagonal:
                log_1mb = jnp.where(strict, log_1mb, 0.0)
            hi, mid, lo = _split3(log_1mb)
            after = _dot(hi, later) + _dot(mid, later) + _dot(lo, later)
            run = r_s[h]
            a = jnp.exp(log_beta + after + run)
            if diagonal:
                a = jnp.where(strict, a, 0.0)
            acc_s[h] = acc_s[h] + _dot(a.astype(BF16), vj)
            r_s[h] = run + jnp.sum(log_1mb, axis=1, keepdims=True)

    block(i, True)

    def cond(carry):
        j, live = carry
        return jnp.logical_and(j >= 0, live)

    def body(carry):
        j, _ = carry
        block(j, False)
        return j - 1, jnp.max(r_s[...]) > SB_DONE

    lax.while_loop(cond, body, (i - 1, jnp.max(r_s[...]) > SB_DONE))
    o_ref[0] = jnp.where(lane < SB_DIM, acc_s[0], acc_s[1])


def _sb_attn(q, k, v):
    b, s, _ = q.shape
    tq = min(ATT_TILE, s)
    pairs = SB_W // LANES
    return pl.pallas_call(
        functools.partial(_sb_attn_kernel, tq=tq),
        grid=(b, pairs, s // tq),
        in_specs=[
            pl.BlockSpec((1, tq, LANES), lambda bi, p, i: (bi, i, p)),
            pl.BlockSpec((1, s, LANES), lambda bi, p, i: (bi, 0, p)),
            pl.BlockSpec((1, s, LANES), lambda bi, p, i: (bi, 0, p)),
        ],
        out_specs=pl.BlockSpec((1, tq, LANES), lambda bi, p, i: (bi, i, p)),
        out_shape=jax.ShapeDtypeStruct((b, s, SB_W), F32),
        scratch_shapes=[
            pltpu.VMEM((2, tq, 1), F32),
            pltpu.VMEM((2, tq, LANES), F32),
        ],
        compiler_params=_params("parallel", "parallel", "arbitrary"),
        name="sb_attn",
    )(q, k, v)


def _swa_attn_kernel(slopes_ref, sinks_ref, q_ref, kc_ref, kp_ref, vc_ref, vp_ref, o_ref, *, tq):
    i = pl.program_id(1)
    half = tq // 2
    row = lax.broadcasted_iota(jnp.int32, (half, tq), 0)
    col = lax.broadcasted_iota(jnp.int32, (half, tq), 1)
    dist_i = row + half - col
    dist = dist_i.astype(F32)
    in_window = jnp.logical_and(dist_i >= 0, dist_i < WINDOW)
    first_key = jnp.where(i > 0, 0, half)
    first = jnp.logical_and(in_window, col >= first_key)
    lane = lax.broadcasted_iota(jnp.int32, (half, LANES), 1)

    kwin = [jnp.concatenate([kp_ref[0, half:, :], kc_ref[0, :half, :]], axis=0), kc_ref[0]]
    vwin = [jnp.concatenate([vp_ref[0, half:, :], vc_ref[0, :half, :]], axis=0), vc_ref[0]]
    for hf in range(2):
        valid = first if hf == 0 else in_window
        rows = slice(hf * half, (hf + 1) * half)
        for t in range(SWA_REP):
            q = q_ref[0, rows, t * LANES:(t + 1) * LANES]
            outs = []
            for g in range(SWA_KV_HEADS):
                head = g * SWA_REP + t
                qg = jnp.where((lane // SWA_DIM) == g, q, jnp.zeros_like(q))
                s = _nt_dot(qg, kwin[hf]) - slopes_ref[head] * dist
                s = jnp.where(valid, s, NEG)
                sink = sinks_ref[head]
                m = jnp.maximum(jnp.max(s, axis=1, keepdims=True), sink)
                p = jnp.exp(s - m)
                den = jnp.sum(p, axis=1, keepdims=True) + jnp.exp(sink - m)
                outs.append(_dot(p.astype(BF16), vwin[hf]) / den)
            o_ref[0, rows, t * LANES:(t + 1) * LANES] = jnp.where(lane < SWA_DIM, outs[0], outs[1])


def _swa_attn(q, k, v, sinks):
    b, s, _ = q.shape
    tq = min(ATT_TILE, s)
    slopes = 2.0 ** (-8.0 * jnp.arange(1, SWA_Q_HEADS + 1, dtype=F32) / SWA_Q_HEADS)
    smem = pl.BlockSpec(memory_space=pltpu.SMEM)
    cur = pl.BlockSpec((1, tq, SWA_KV_W), lambda bi, i: (bi, i, 0))
    prev = pl.BlockSpec((1, tq, SWA_KV_W), lambda bi, i: (bi, jnp.maximum(i - 1, 0), 0))
    return pl.pallas_call(
        functools.partial(_swa_attn_kernel, tq=tq),
        grid=(b, s // tq),
        in_specs=[
            smem, smem,
            pl.BlockSpec((1, tq, SWA_Q_W), lambda bi, i: (bi, i, 0)),
            cur, prev, cur, prev,
        ],
        out_specs=pl.BlockSpec((1, tq, SWA_Q_W), lambda bi, i: (bi, i, 0)),
        out_shape=jax.ShapeDtypeStruct((b, s, SWA_Q_W), F32),
        compiler_params=_params("parallel", "parallel"),
        name="swa_attn",
    )(slopes, sinks.astype(F32), q, k, k, v, v)


def _out_proj_kernel(ya_ref, yb_ref, yc_ref, x_ref, ga_ref, gb_ref, gc_ref, ln2_ref, mod_ref,
                     wa_ref, wb_ref, wc_ref, x1_ref, h2_ref, *, a_scale):
    ya = ya_ref[0]
    head = lax.broadcasted_iota(jnp.int32, ya.shape, 1) // A_V
    sq = ya * ya
    inv = jnp.zeros_like(ya)
    for h in range(A_HEADS):
        mine = head == h
        ms = jnp.sum(jnp.where(mine, sq, 0.0), axis=-1, keepdims=True) * (1.0 / A_V)
        inv = jnp.where(mine, lax.rsqrt(ms + EPS), inv)
    na = ya * inv * ga_ref[...] * a_scale
    nb = _rms(yb_ref[0], gb_ref[...])
    nc = _rms(yc_ref[0], gc_ref[...])
    mixed = (_dot(na.astype(BF16), wa_ref[...]) + _dot(nb.astype(BF16), wb_ref[...])
             + _dot(nc.astype(BF16), wc_ref[...]))
    x1 = x_ref[0] + mod_ref[0, 2:3, :] * mixed
    x1_ref[0] = x1
    h2 = _rms(x1, ln2_ref[...]) * (1.0 + mod_ref[0, 4:5, :]) + mod_ref[0, 3:4, :]
    h2_ref[0] = h2.astype(BF16)


def _out_proj(ya, yb, yc, x, ga, gb, gc, ln2, mod, wa, wb, wc, a_scale):
    b, s, d = x.shape
    tm = min(ROW_TILE, s)
    row = lambda wd: pl.BlockSpec((1, tm, wd), lambda bi, i: (bi, i, 0))
    const = lambda r, cdim: pl.BlockSpec((r, cdim), lambda bi, i: (0, 0))
    return pl.pallas_call(
        functools.partial(_out_proj_kernel, a_scale=a_scale),
        grid=(b, s // tm),
        in_specs=[
            row(A_W), row(SB_W), row(SWA_Q_W), row(d),
            const(1, A_W), const(1, SB_W), const(1, SWA_Q_W), const(1, d),
            pl.BlockSpec((1, N_MOD, d), lambda bi, i: (bi, 0, 0)),
            const(A_W, d), const(SB_W, d), const(SWA_Q_W, d),
        ],
        out_specs=[row(d), row(d)],
        out_shape=[jax.ShapeDtypeStruct((b, s, d), F32), jax.ShapeDtypeStruct((b, s, d), BF16)],
        compiler_params=_params("parallel", "parallel"),
        name="out_proj",
    )(ya, yb, yc, x, ga, gb, gc, ln2.reshape(1, d), mod, wa, wb, wc)


def _ffn_kernel(h2_ref, x1_ref, mod_ref, wg_ref, wu_ref, wd_ref, fg_ref, o_ref, *, final):
    h2 = h2_ref[0]
    gate = _dot(h2, wg_ref[...])
    up = _dot(h2, wu_ref[...])
    act = gate * jax.nn.sigmoid(gate) * up
    out = x1_ref[0] + mod_ref[0, 5:6, :] * _dot(act.astype(BF16), wd_ref[...])
    if final:
        out = _rms(out, fg_ref[...])
    o_ref[0] = out


def _ffn(h2, x1, mod, wg, wu, wd, fg, final):
    b, s, d = x1.shape
    f = wg.shape[1]
    tm = min(FFN_TILE, s)
    row = pl.BlockSpec((1, tm, d), lambda bi, i: (bi, i, 0))
    return pl.pallas_call(
        functools.partial(_ffn_kernel, final=final),
        grid=(b, s // tm),
        in_specs=[
            row, row,
            pl.BlockSpec((1, N_MOD, d), lambda bi, i: (bi, 0, 0)),
            pl.BlockSpec((d, f), lambda bi, i: (0, 0)),
            pl.BlockSpec((d, f), lambda bi, i: (0, 0)),
            pl.BlockSpec((f, d), lambda bi, i: (0, 0)),
            pl.BlockSpec((1, d), lambda bi, i: (0, 0)),
        ],
        out_specs=row,
        out_shape=jax.ShapeDtypeStruct((b, s, d), F32),
        compiler_params=_params("parallel", "parallel"),
        name="ffn",
    )(h2, x1, mod, wg, wu, wd, fg.reshape(1, d))


def _swa_perm():
    cols = []
    for t in range(SWA_REP):
        for g in range(SWA_KV_HEADS):
            head = g * SWA_REP + t
            cols.extend(range(head * SWA_DIM, (head + 1) * SWA_DIM))
    return jnp.asarray(cols, jnp.int32)


def _prep_w_in(w):
    bounds = [0]
    for wd in IN_SPLITS:
        bounds.append(bounds[-1] + wd)
    parts = [w[:, bounds[n]:bounds[n + 1]] for n in range(len(IN_SPLITS))]
    parts[0] = parts[0] * (A_QK ** -0.5)
    parts[3] = parts[3] * (SB_DIM ** -0.5)
    parts[6] = (parts[6] * (SWA_DIM ** -0.5))[:, _swa_perm()]
    return jnp.concatenate(parts, axis=1).astype(BF16)


def kernel(x, c, ln1_g, ln2_g, w_mod, b_mod, w_in, lam_q1, lam_k1, lam_q2, lam_k2, diff_norm_g,
           sb_norm_g, swa_norm_g, swa_sinks, w_out, w_gate, w_up, w_down, final_g):
    depth = w_in.shape[0]
    perm = _swa_perm()
    mod = _modulation(c, w_mod, b_mod)
    for l in range(depth):
        lam_init = 0.8 - 0.6 * math.exp(-0.3 * l)
        qa, ka, va, qb, kb, vb, qc, kc, vc = _in_proj(x, ln1_g[l], mod[l], _prep_w_in(w_in[l]))
        ya = _diff_attn(qa, ka, va, lam_q1[l], lam_k1[l], lam_q2[l], lam_k2[l], lam_init)
        yb = _sb_attn(qb, kb, vb)
        yc = _swa_attn(qc, kc, vc, swa_sinks[l])
        wo = w_out[l].astype(BF16)
        x1, h2 = _out_proj(
            ya, yb, yc, x,
            jnp.tile(diff_norm_g[l], A_HEADS).reshape(1, A_W),
            sb_norm_g[l].reshape(1, SB_W),
            swa_norm_g[l][perm].reshape(1, SWA_Q_W),
            ln2_g[l], mod[l],
            wo[:A_W], wo[A_W:A_W + SB_W], wo[A_W + SB_W:][perm],
            1.0 - lam_init)
        x = _ffn(h2, x1, mod[l], w_gate[l].astype(BF16), w_up[l].astype(BF16),
                 w_down[l].astype(BF16), final_g, l == depth - 1)
    return x
```

```python
import functools
import math

import numpy as np
import jax
import jax.numpy as jnp
from jax import lax
from jax.experimental import pallas as pl
from jax.experimental.pallas import tpu as pltpu

F32 = jnp.float32
BF16 = jnp.bfloat16

N_MOD = 6
EPS = 1e-6
A_HEADS = 4
A_QK = 32
A_V = 64
SB_HEADS = 4
SB_DIM = 64
SWA_Q_HEADS = 8
SWA_KV_HEADS = 2
SWA_REP = 4
SWA_DIM = 64
WINDOW = 128
A_W = 256
SB_W = 256
SWA_Q_W = 512
SWA_KV_W = 128

LANES = 128
NEG = -1e30
SB_DONE = -110.0

VMEM_LIMIT = 56 * 1024 * 1024

ROW_TILE = 512
FFN_TILE = 256
ATT_TILE = 256
A_TILE = ROW_TILE

A_AUG = 2 * A_QK
A_VT = 80
A_KW = A_HEADS * LANES
A_TW = A_HEADS * (LANES + A_VT)


def _params(*sem):
    return pltpu.CompilerParams(dimension_semantics=sem, vmem_limit_bytes=VMEM_LIMIT)


def _nt_dot(a, b):
    return lax.dot_general(a, b, (((1,), (1,)), ((), ())), preferred_element_type=F32)


def _dot(a, b):
    return jnp.dot(a, b, preferred_element_type=F32)


def _rms(x, g):
    ms = jnp.mean(x * x, axis=-1, keepdims=True)
    return x * lax.rsqrt(ms + EPS) * g


def _alibi_slopes(n):
    return 2.0 ** (-8.0 * np.arange(1, n + 1, dtype=np.float64) / n)


def _mod_kernel(c_ref, w_ref, b_ref, o_ref):
    cv = c_ref[...]
    ca = cv * jax.nn.sigmoid(cv)
    o_ref[0] = jnp.dot(ca, w_ref[0], preferred_element_type=F32,
                       precision=lax.Precision.HIGHEST) + b_ref[0]


def _modulation(c, w_mod, b_mod):
    depth, d, n = w_mod.shape
    b = c.shape[0]
    rows = 8
    cp = jnp.zeros((rows, d), F32).at[:b].set(c)
    tn = 1024
    out = pl.pallas_call(
        _mod_kernel,
        grid=(depth, n // tn),
        in_specs=[
            pl.BlockSpec((rows, d), lambda l, j: (0, 0)),
            pl.BlockSpec((1, d, tn), lambda l, j: (l, 0, j)),
            pl.BlockSpec((1, 1, tn), lambda l, j: (l, 0, j)),
        ],
        out_specs=pl.BlockSpec((1, rows, tn), lambda l, j: (l, 0, j)),
        out_shape=jax.ShapeDtypeStruct((depth, rows, n), F32),
        compiler_params=_params("parallel", "parallel"),
        name="modulation",
    )(cp, w_mod, b_mod.reshape(depth, 1, n))
    return out[:, :b].reshape(depth, b, N_MOD, d)


ROW_SPLITS = (A_KW, SB_W, SB_W, SB_W, SWA_Q_W, SWA_KV_W, SWA_KV_W)


def _in_proj_kernel(x_ref, g_ref, mod_ref, w_ref, wt_ref, kaug_ref, taug_ref, *out_refs):
    row_refs, (qt_ref, vt_ref) = out_refs[:len(ROW_SPLITS)], out_refs[len(ROW_SPLITS):]
    x = x_ref[0]
    h = _rms(x, g_ref[...]) * (1.0 + mod_ref[0, 1:2, :]) + mod_ref[0, 0:1, :]
    hb = h.astype(BF16)
    proj = _dot(hb, w_ref[...])
    start = 0
    for n, (ref, width) in enumerate(zip(row_refs, ROW_SPLITS)):
        part = proj[:, start:start + width]
        if n == 0:
            part = part + kaug_ref[...]
        ref[0] = part.astype(BF16)
        start += width
    proj_t = _nt_dot(wt_ref[...], hb) + taug_ref[...]
    for hd in range(A_HEADS):
        qt_ref[0, hd, 0] = proj_t[hd * LANES:(hd + 1) * LANES].astype(BF16)
        v0 = A_KW + hd * A_VT
        vt_ref[0, hd, 0] = proj_t[v0:v0 + A_VT].astype(BF16)


def _alibi_constants(t):
    slopes = _alibi_slopes(A_HEADS)
    idx = np.arange(t)
    lo, hi = idx % 256, idx - idx % 256
    kaug = np.zeros((t, A_KW), np.float32)
    taug = np.zeros((A_TW, t), np.float32)
    for hd in range(A_HEADS):
        k0 = hd * LANES + A_AUG
        kaug[:, k0 + 0] = 1.0
        kaug[:, k0 + 1] = 1.0
        kaug[:, k0 + 2] = slopes[hd] * lo
        kaug[:, k0 + 3] = slopes[hd] * hi
        taug[k0 + 0] = -slopes[hd] * lo
        taug[k0 + 1] = -slopes[hd] * hi
        taug[k0 + 2] = 1.0
        taug[k0 + 3] = 1.0
        taug[A_KW + hd * A_VT + A_V] = 1.0
    return jnp.asarray(kaug), jnp.asarray(taug)


def _in_proj(x, g, mod, w, wt):
    b, s, d = x.shape
    tm = A_TILE
    nb = s // tm
    n = w.shape[1]
    kaug, taug = _alibi_constants(tm)
    const = lambda shape: pl.BlockSpec(shape, lambda bi, i: (0,) * len(shape))
    return pl.pallas_call(
        _in_proj_kernel,
        grid=(b, nb),
        in_specs=[
            pl.BlockSpec((1, tm, d), lambda bi, i: (bi, i, 0)),
            const((1, d)),
            pl.BlockSpec((1, N_MOD, d), lambda bi, i: (bi, 0, 0)),
            const((d, n)), const((A_TW, d)), const((tm, A_KW)), const((A_TW, tm)),
        ],
        out_specs=[pl.BlockSpec((1, tm, wd), lambda bi, i: (bi, i, 0)) for wd in ROW_SPLITS] + [
            pl.BlockSpec((1, A_HEADS, 1, LANES, tm), lambda bi, i: (bi, 0, i, 0, 0)),
            pl.BlockSpec((1, A_HEADS, 1, A_VT, tm), lambda bi, i: (bi, 0, i, 0, 0)),
        ],
        out_shape=[jax.ShapeDtypeStruct((b, s, wd), BF16) for wd in ROW_SPLITS] + [
            jax.ShapeDtypeStruct((b, A_HEADS, nb, LANES, tm), BF16),
            jax.ShapeDtypeStruct((b, A_HEADS, nb, A_VT, tm), BF16),
        ],
        compiler_params=_params("parallel", "parallel"),
        name="in_proj",
    )(x, g.reshape(1, d), mod, w, wt, kaug, taug)


def _diff_attn_kernel(slopes_ref, lq1_ref, lk1_ref, lq2_ref, lk2_ref, qt_ref, k_ref, vt_ref,
                      o_ref, m_s, acc_s, *, t, lam_init):
    hd = pl.program_id(1)
    i = pl.program_id(2)
    qt = qt_ref[0, 0, 0]
    feat = lax.broadcasted_iota(jnp.int32, (LANES, t), 0)
    zero = jnp.zeros_like(qt)
    q_maps = (jnp.where(jnp.logical_or(feat < A_QK, feat >= A_AUG), qt, zero),
              jnp.where(feat >= A_QK, qt, zero))
    slope = slopes_ref[hd]
    causal = (lax.broadcasted_iota(jnp.int32, (t, t), 0)
              <= lax.broadcasted_iota(jnp.int32, (t, t), 1))

    m_s[...] = jnp.full(m_s.shape, NEG, F32)
    acc_s[...] = jnp.zeros(acc_s.shape, F32)

    def block(j, diagonal):
        kj = k_ref[0, pl.ds(pl.multiple_of(j * t, t), t), :]
        vtj = vt_ref[0, 0, j]
        off = slope * ((i - j) * t).astype(F32)
        for a in range(2):
            s = _dot(kj, q_maps[a])
            if diagonal:
                s = jnp.where(causal, s, NEG)
            m_old = m_s[a]
            m_new = jnp.maximum(m_old, jnp.max(s, axis=0, keepdims=True) - off)
            p = jnp.exp(s - (m_new + off))
            alpha = jnp.exp(m_old - m_new)
            acc_s[a] = alpha * acc_s[a] + _dot(vtj, p.astype(BF16))
            m_s[a] = m_new

    block(i, True)

    def body(n, carry):
        block(i - 1 - n, False)
        return carry

    lax.fori_loop(0, i, body, 0)

    lam = (jnp.exp(jnp.sum(lq1_ref[...] * lk1_ref[...], keepdims=True))
           - jnp.exp(jnp.sum(lq2_ref[...] * lk2_ref[...], keepdims=True)) + lam_init)
    outs = [acc_s[a, :A_V, :] / acc_s[a, A_V:A_V + 1, :] for a in range(2)]
    y_t = outs[0] - lam * outs[1]
    y_t = jnp.concatenate([y_t, jnp.zeros((LANES - A_V, t), F32)], axis=0)
    o_ref[0] = y_t.T


def _diff_attn(qt, k, vt, lq1, lk1, lq2, lk2, lam_init):
    b, s, _ = k.shape
    t = A_TILE
    nb = s // t
    slopes = jnp.asarray(_alibi_slopes(A_HEADS), F32)
    smem = pl.BlockSpec(memory_space=pltpu.SMEM)
    vec = pl.BlockSpec((1, A_QK), lambda bi, h, i: (0, 0))
    return pl.pallas_call(
        functools.partial(_diff_attn_kernel, t=t, lam_init=lam_init),
        grid=(b, A_HEADS, nb),
        in_specs=[
            smem, vec, vec, vec, vec,
            pl.BlockSpec((1, 1, 1, LANES, t), lambda bi, h, i: (bi, h, i, 0, 0)),
            pl.BlockSpec((1, s, LANES), lambda bi, h, i: (bi, 0, h)),
            pl.BlockSpec((1, 1, nb, A_VT, t), lambda bi, h, i: (bi, h, 0, 0, 0)),
        ],
        out_specs=pl.BlockSpec((1, t, LANES), lambda bi, h, i: (bi, i, h)),
        out_shape=jax.ShapeDtypeStruct((b, s, A_KW), F32),
        scratch_shapes=[
            pltpu.VMEM((2, 1, t), F32),
            pltpu.VMEM((2, A_VT, t), F32),
        ],
        compiler_params=_params("parallel", "parallel", "arbitrary"),
        name="diff_attn",
    )(slopes, lq1.reshape(1, A_QK), lk1.reshape(1, A_QK), lq2.reshape(1, A_QK),
      lk2.reshape(1, A_QK), qt, k, vt)


def _split3(x):
    hi = x.astype(BF16)
    r1 = x - hi.astype(F32)
    mid = r1.astype(BF16)
    lo = (r1 - mid.astype(F32)).astype(BF16)
    return hi, mid, lo


def _sb_attn_kernel(q_ref, k_ref, v_ref, o_ref, r_s, acc_s, *, tq):
    i = pl.program_id(2)
    q = q_ref[0]
    lane = lax.broadcasted_iota(jnp.int32, (tq, LANES), 1)
    qm = [jnp.where(lane < SB_DIM, q, jnp.zeros_like(q)),
          jnp.where(lane >= SB_DIM, q, jnp.zeros_like(q))]
    row = lax.broadcasted_iota(jnp.int32, (tq, tq), 0)
    col = lax.broadcasted_iota(jnp.int32, (tq, tq), 1)
    strict = col < row
    later = (row > col).astype(BF16)

    r_s[...] = jnp.zeros(r_s.shape, F32)
    acc_s[...] = jnp.zeros(acc_s.shape, F32)

    def block(j, diagonal):
        start = pl.multiple_of(j * tq, tq)
        kj = k_ref[0, pl.ds(start, tq), :]
        vj = v_ref[0, pl.ds(start, tq), :]
        for h in range(2):
            z = _nt_dot(qm[h], kj)
            log_beta = jnp.minimum(z, 0.0) - jnp.log(1.0 + jnp.exp(-jnp.abs(z)))
            log_1mb = log_beta - z
            if diagonal:
                log_1mb = jnp.where(strict, log_1mb, 0.0)
            hi, mid, lo = _split3(log_1mb)
            after = _dot(hi, later) + _dot(mid, later) + _dot(lo, later)
            run = r_s[h]
            a = jnp.exp(log_beta + after + run)
            if diagonal:
                a = jnp.where(strict, a, 0.0)
            acc_s[h] = acc_s[h] + _dot(a.astype(BF16), vj)
            r_s[h] = run + jnp.sum(log_1mb, axis=1, keepdims=True)

    block(i, True)

    def cond(carry):
        j, live = carry
        return jnp.logical_and(j >= 0, live)

    def body(carry):
        j, _ = carry
        block(j, False)
        return j - 1, jnp.max(r_s[...]) > SB_DONE

    lax.while_loop(cond, body, (i - 1, jnp.max(r_s[...]) > SB_DONE))
    o_ref[0] = jnp.where(lane < SB_DIM, acc_s[0], acc_s[1])


def _sb_attn(q, k, v):
    b, s, _ = q.shape
    tq = min(ATT_TILE, s)
    pairs = SB_W // LANES
    return pl.pallas_call(
        functools.partial(_sb_attn_kernel, tq=tq),
        grid=(b, pairs, s // tq),
        in_specs=[
            pl.BlockSpec((1, tq, LANES), lambda bi, p, i: (bi, i, p)),
            pl.BlockSpec((1, s, LANES), lambda bi, p, i: (bi, 0, p)),
            pl.BlockSpec((1, s, LANES), lambda bi, p, i: (bi, 0, p)),
        ],
        out_specs=pl.BlockSpec((1, tq, LANES), lambda bi, p, i: (bi, i, p)),
        out_shape=jax.ShapeDtypeStruct((b, s, SB_W), F32),
        scratch_shapes=[
            pltpu.VMEM((2, tq, 1), F32),
            pltpu.VMEM((2, tq, LANES), F32),
        ],
        compiler_params=_params("parallel", "parallel", "arbitrary"),
        name="sb_attn",
    )(q, k, v)


def _swa_attn_kernel(slopes_ref, sinks_ref, q_ref, kc_ref, kp_ref, vc_ref, vp_ref, o_ref, *, tq):
    i = pl.program_id(1)
    half = tq // 2
    row = lax.broadcasted_iota(jnp.int32, (half, tq), 0)
    col = lax.broadcasted_iota(jnp.int32, (half, tq), 1)
    dist_i = row + half - col
    dist = dist_i.astype(F32)
    in_window = jnp.logical_and(dist_i >= 0, dist_i < WINDOW)
    first_key = jnp.where(i > 0, 0, half)
    first = jnp.logical_and(in_window, col >= first_key)
    lane = lax.broadcasted_iota(jnp.int32, (half, LANES), 1)

    kwin = [jnp.concatenate([kp_ref[0, half:, :], kc_ref[0, :half, :]], axis=0), kc_ref[0]]
    vwin = [jnp.concatenate([vp_ref[0, half:, :], vc_ref[0, :half, :]], axis=0), vc_ref[0]]
    for hf in range(2):
        valid = first if hf == 0 else in_window
        rows = slice(hf * half, (hf + 1) * half)
        for t in range(SWA_REP):
            q = q_ref[0, rows, t * LANES:(t + 1) * LANES]
            outs = []
            for g in range(SWA_KV_HEADS):
                head = g * SWA_REP + t
                qg = jnp.where((lane // SWA_DIM) == g, q, jnp.zeros_like(q))
                s = _nt_dot(qg, kwin[hf]) - slopes_ref[head] * dist
                s = jnp.where(valid, s, NEG)
                sink = sinks_ref[head]
                m = jnp.maximum(jnp.max(s, axis=1, keepdims=True), sink)
                p = jnp.exp(s - m)
                den = jnp.sum(p, axis=1, keepdims=True) + jnp.exp(sink - m)
                outs.append(_dot(p.astype(BF16), vwin[hf]) / den)
            o_ref[0, rows, t * LANES:(t + 1) * LANES] = jnp.where(lane < SWA_DIM, outs[0], outs[1])


def _swa_attn(q, k, v, sinks):
    b, s, _ = q.shape
    tq = min(ATT_TILE, s)
    slopes = jnp.asarray(_alibi_slopes(SWA_Q_HEADS), F32)
    smem = pl.BlockSpec(memory_space=pltpu.SMEM)
    cur = pl.BlockSpec((1, tq, SWA_KV_W), lambda bi, i: (bi, i, 0))
    prev = pl.BlockSpec((1, tq, SWA_KV_W), lambda bi, i: (bi, jnp.maximum(i - 1, 0), 0))
    return pl.pallas_call(
        functools.partial(_swa_attn_kernel, tq=tq),
        grid=(b, s // tq),
        in_specs=[
            smem, smem,
            pl.BlockSpec((1, tq, SWA_Q_W), lambda bi, i: (bi, i, 0)),
            cur, prev, cur, prev,
        ],
        out_specs=pl.BlockSpec((1, tq, SWA_Q_W), lambda bi, i: (bi, i, 0)),
        out_shape=jax.ShapeDtypeStruct((b, s, SWA_Q_W), F32),
        compiler_params=_params("parallel", "parallel"),
        name="swa_attn",
    )(slopes, sinks.astype(F32), q, k, k, v, v)


def _out_proj_kernel(ya_ref, yb_ref, yc_ref, x_ref, ga_ref, gb_ref, gc_ref, ln2_ref, mod_ref,
                     wa_ref, wb_ref, wc_ref, x1_ref, h2_ref, *, a_scale):
    mixed = _dot(_rms(yb_ref[0], gb_ref[...]).astype(BF16), wb_ref[...])
    mixed += _dot(_rms(yc_ref[0], gc_ref[...]).astype(BF16), wc_ref[...])
    for h in range(A_HEADS):
        cols = slice(h * LANES, (h + 1) * LANES)
        ya = ya_ref[0, :, cols]
        ms = jnp.sum(ya * ya, axis=-1, keepdims=True) * (1.0 / A_V)
        na = ya * lax.rsqrt(ms + EPS) * ga_ref[:, cols] * a_scale
        mixed += _dot(na.astype(BF16), wa_ref[cols, :])
    x1 = x_ref[0] + mod_ref[0, 2:3, :] * mixed
    x1_ref[0] = x1
    h2 = _rms(x1, ln2_ref[...]) * (1.0 + mod_ref[0, 4:5, :]) + mod_ref[0, 3:4, :]
    h2_ref[0] = h2.astype(BF16)


def _out_proj(ya, yb, yc, x, ga, gb, gc, ln2, mod, wa, wb, wc, a_scale):
    b, s, d = x.shape
    tm = min(ROW_TILE, s)
    row = lambda wd: pl.BlockSpec((1, tm, wd), lambda bi, i: (bi, i, 0))
    const = lambda r, cdim: pl.BlockSpec((r, cdim), lambda bi, i: (0, 0))
    return pl.pallas_call(
        functools.partial(_out_proj_kernel, a_scale=a_scale),
        grid=(b, s // tm),
        in_specs=[
            row(A_KW), row(SB_W), row(SWA_Q_W), row(d),
            const(1, A_KW), const(1, SB_W), const(1, SWA_Q_W), const(1, d),
            pl.BlockSpec((1, N_MOD, d), lambda bi, i: (bi, 0, 0)),
            const(A_KW, d), const(SB_W, d), const(SWA_Q_W, d),
        ],
        out_specs=[row(d), row(d)],
        out_shape=[jax.ShapeDtypeStruct((b, s, d), F32), jax.ShapeDtypeStruct((b, s, d), BF16)],
        compiler_params=_params("parallel", "parallel"),
        name="out_proj",
    )(ya, yb, yc, x, ga, gb, gc, ln2.reshape(1, d), mod, wa, wb, wc)


def _ffn_kernel(h2_ref, x1_ref, mod_ref, wg_ref, wu_ref, wd_ref, fg_ref, o_ref, *, final):
    h2 = h2_ref[0]
    gate = _dot(h2, wg_ref[...])
    up = _dot(h2, wu_ref[...])
    act = gate * jax.nn.sigmoid(gate) * up
    out = x1_ref[0] + mod_ref[0, 5:6, :] * _dot(act.astype(BF16), wd_ref[...])
    if final:
        out = _rms(out, fg_ref[...])
    o_ref[0] = out


def _ffn(h2, x1, mod, wg, wu, wd, fg, final):
    b, s, d = x1.shape
    f = wg.shape[1]
    tm = min(FFN_TILE, s)
    row = pl.BlockSpec((1, tm, d), lambda bi, i: (bi, i, 0))
    return pl.pallas_call(
        functools.partial(_ffn_kernel, final=final),
        grid=(b, s // tm),
        in_specs=[
            row, row,
            pl.BlockSpec((1, N_MOD, d), lambda bi, i: (bi, 0, 0)),
            pl.BlockSpec((d, f), lambda bi, i: (0, 0)),
            pl.BlockSpec((d, f), lambda bi, i: (0, 0)),
            pl.BlockSpec((f, d), lambda bi, i: (0, 0)),
            pl.BlockSpec((1, d), lambda bi, i: (0, 0)),
        ],
        out_specs=row,
        out_shape=jax.ShapeDtypeStruct((b, s, d), F32),
        compiler_params=_params("parallel", "parallel"),
        name="ffn",
    )(h2, x1, mod, wg, wu, wd, fg.reshape(1, d))


def _swa_perm():
    cols = []
    for t in range(SWA_REP):
        for g in range(SWA_KV_HEADS):
            head = g * SWA_REP + t
            cols.extend(range(head * SWA_DIM, (head + 1) * SWA_DIM))
    return jnp.asarray(cols, jnp.int32)


def _pad_heads(w, per_head, width):
    d = w.shape[0]
    w = w.reshape(d, A_HEADS, per_head)
    return jnp.pad(w, ((0, 0), (0, 0), (0, width - per_head))).reshape(d, A_HEADS * width)


def _prep_w_in(w):
    bounds = np.cumsum((0, A_W, A_W, A_W, SB_W, SB_W, SB_W, SWA_Q_W, SWA_KV_W, SWA_KV_W))
    qa, ka, va, qb, kb, vb, qc, kc, vc = [w[:, bounds[n]:bounds[n + 1]] for n in range(9)]
    row = jnp.concatenate([
        _pad_heads(ka, 2 * A_QK, LANES), qb * (SB_DIM ** -0.5), kb, vb,
        (qc * (SWA_DIM ** -0.5))[:, _swa_perm()], kc, vc], axis=1)
    transposed = jnp.concatenate([
        _pad_heads(qa * (A_QK ** -0.5), 2 * A_QK, LANES), _pad_heads(va, A_V, A_VT)], axis=1).T
    return row.astype(BF16), transposed.astype(BF16)


def kernel(x, c, ln1_g, ln2_g, w_mod, b_mod, w_in, lam_q1, lam_k1, lam_q2, lam_k2, diff_norm_g,
           sb_norm_g, swa_norm_g, swa_sinks, w_out, w_gate, w_up, w_down, final_g):
    depth = w_in.shape[0]
    d = x.shape[-1]
    perm = _swa_perm()
    mod = _modulation(c, w_mod, b_mod)
    for l in range(depth):
        lam_init = 0.8 - 0.6 * math.exp(-0.3 * l)
        w_row, w_t = _prep_w_in(w_in[l])
        ka, qb, kb, vb, qc, kc, vc, qta, vta = _in_proj(x, ln1_g[l], mod[l], w_row, w_t)
        ya = _diff_attn(qta, ka, vta, lam_q1[l], lam_k1[l], lam_q2[l], lam_k2[l], lam_init)
        yb = _sb_attn(qb, kb, vb)
        yc = _swa_attn(qc, kc, vc, swa_sinks[l])
        wo = w_out[l].astype(BF16)
        x1, h2 = _out_proj(
            ya, yb, yc, x,
            _pad_heads(diff_norm_g[l].reshape(1, A_V).repeat(A_HEADS, 0).reshape(1, A_W), A_V, LANES),
            sb_norm_g[l].reshape(1, SB_W),
            swa_norm_g[l][perm].reshape(1, SWA_Q_W),
            ln2_g[l], mod[l],
            _pad_heads(wo[:A_W].T, A_V, LANES).T, wo[A_W:A_W + SB_W], wo[A_W + SB_W:][perm],
            1.0 - lam_init)
        x = _ffn(h2, x1, mod[l], w_gate[l].astype(BF16), w_up[l].astype(BF16),
                 w_down[l].astype(BF16), final_g, l == depth - 1)
    return x
```

```python
import functools
import math

import numpy as np
import jax
import jax.numpy as jnp
from jax import lax
from jax.experimental import pallas as pl
from jax.experimental.pallas import tpu as pltpu

F32 = jnp.float32
BF16 = jnp.bfloat16

N_MOD = 6
EPS = 1e-6
A_HEADS = 4
A_QK = 32
A_V = 64
SB_HEADS = 4
SB_DIM = 64
SWA_Q_HEADS = 8
SWA_KV_HEADS = 2
SWA_REP = 4
SWA_DIM = 64
WINDOW = 128
A_W = 256
SB_W = 256
SWA_Q_W = 512
SWA_KV_W = 128

LANES = 128
NEG = -1e30
SB_DONE = -110.0

VMEM_LIMIT = 56 * 1024 * 1024

ROW_TILE = 512
FFN_TILE = 256
ATT_TILE = 256
A_TILE = ROW_TILE

LOG2E = math.log2(math.e)
A_AUG = 2 * A_QK
A_REF = A_AUG + 16
A_PIECES = 4
A_VT = 80
A_KW = A_HEADS * LANES
A_TW = A_HEADS * (LANES + A_VT)
SKIP_LOG2 = -160.0
FAST_LOG2 = 80.0
BOUND_MARGIN = 1.0


def _params(*sem):
    return pltpu.CompilerParams(dimension_semantics=sem, vmem_limit_bytes=VMEM_LIMIT)


def _nt_dot(a, b):
    return lax.dot_general(a, b, (((1,), (1,)), ((), ())), preferred_element_type=F32)


def _dot(a, b):
    return jnp.dot(a, b, preferred_element_type=F32)


def _rms(x, g):
    ms = jnp.mean(x * x, axis=-1, keepdims=True)
    return x * lax.rsqrt(ms + EPS) * g


def _alibi_slopes(n):
    return 2.0 ** (-8.0 * np.arange(1, n + 1, dtype=np.float64) / n)


def _split_bf16(x, n):
    pieces = []
    for _ in range(n - 1):
        p = x.astype(BF16)
        pieces.append(p)
        x = x - p.astype(F32)
    pieces.append(x.astype(BF16))
    return pieces


def _mod_kernel(c_ref, w_ref, b_ref, o_ref):
    cv = c_ref[...]
    ca = cv * jax.nn.sigmoid(cv)
    o_ref[0] = jnp.dot(ca, w_ref[0], preferred_element_type=F32,
                       precision=lax.Precision.HIGHEST) + b_ref[0]


def _modulation(c, w_mod, b_mod):
    depth, d, n = w_mod.shape
    b = c.shape[0]
    rows = 8
    cp = jnp.zeros((rows, d), F32).at[:b].set(c)
    tn = 1024
    out = pl.pallas_call(
        _mod_kernel,
        grid=(depth, n // tn),
        in_specs=[
            pl.BlockSpec((rows, d), lambda l, j: (0, 0)),
            pl.BlockSpec((1, d, tn), lambda l, j: (l, 0, j)),
            pl.BlockSpec((1, 1, tn), lambda l, j: (l, 0, j)),
        ],
        out_specs=pl.BlockSpec((1, rows, tn), lambda l, j: (l, 0, j)),
        out_shape=jax.ShapeDtypeStruct((depth, rows, n), F32),
        compiler_params=_params("parallel", "parallel"),
        name="modulation",
    )(cp, w_mod, b_mod.reshape(depth, 1, n))
    return out[:, :b].reshape(depth, b, N_MOD, d)


ROW_SPLITS = (A_KW, SB_W, SB_W, SB_W, SWA_Q_W, SWA_KV_W, SWA_KV_W)


def _in_proj_kernel(x_ref, g_ref, mod_ref, w_ref, wt_ref, kaug_ref, taug_ref, *out_refs):
    row_refs, (qt_ref, vt_ref, kn_ref) = out_refs[:len(ROW_SPLITS)], out_refs[len(ROW_SPLITS):]
    x = x_ref[0]
    h = _rms(x, g_ref[...]) * (1.0 + mod_ref[0, 1:2, :]) + mod_ref[0, 0:1, :]
    hb = h.astype(BF16)
    proj = _dot(hb, w_ref[...])
    start = 0
    for n, (ref, width) in enumerate(zip(row_refs, ROW_SPLITS)):
        part = proj[:, start:start + width]
        if n == 0:
            part = part + kaug_ref[...]
            keys = part.astype(BF16)
        ref[0] = part.astype(BF16)
        start += width
    feat = lax.broadcasted_iota(jnp.int32, (x.shape[0], LANES), 1)
    out_lane = lax.broadcasted_iota(jnp.int32, kn_ref.shape[2:], 1)
    norms = jnp.zeros(kn_ref.shape[2:], F32)
    for hd in range(A_HEADS):
        kf = keys[:, hd * LANES:(hd + 1) * LANES].astype(F32)
        sq = jnp.sum(jnp.where(feat < A_AUG, kf * kf, 0.0), axis=1, keepdims=True)
        norms = jnp.where(out_lane == hd, jnp.sqrt(jnp.max(sq, axis=0, keepdims=True)), norms)
    kn_ref[0, 0] = norms
    proj_t = _nt_dot(wt_ref[...], hb) + taug_ref[...]
    for hd in range(A_HEADS):
        qt_ref[0, hd, 0] = proj_t[hd * LANES:(hd + 1) * LANES].astype(BF16)
        v0 = A_KW + hd * A_VT
        vt_ref[0, hd, 0] = proj_t[v0:v0 + A_VT].astype(BF16)


def _bf16_pieces(value, n):
    pieces = []
    rest = float(value)
    for _ in range(n):
        p = float(np.asarray(rest, np.float32).astype(jnp.bfloat16).astype(np.float64))
        pieces.append(p)
        rest -= p
    return pieces


def _slopes_log2(hd):
    pieces = _bf16_pieces(_alibi_slopes(A_HEADS)[hd] * LOG2E, A_PIECES)
    return sum(pieces), pieces


def _alibi_constants(t):
    idx = np.arange(t)
    lo, hi = idx % 256, idx - idx % 256
    kaug = np.zeros((t, A_KW), np.float32)
    taug = np.zeros((A_TW, t), np.float32)
    for hd in range(A_HEADS):
        _, pieces = _slopes_log2(hd)
        k0 = hd * LANES + A_AUG
        for n, piece in enumerate(pieces):
            taug[k0 + n] = -lo
            taug[k0 + A_PIECES + n] = -hi
            kaug[:, k0 + n] = piece
            kaug[:, k0 + A_PIECES + n] = piece
            taug[k0 + 2 * A_PIECES + n] = piece
            taug[k0 + 3 * A_PIECES + n] = piece
            kaug[:, k0 + 2 * A_PIECES + n] = lo
            kaug[:, k0 + 3 * A_PIECES + n] = hi
        r0 = hd * LANES + A_REF
        kaug[:, r0:r0 + 3] = 1.0
        taug[A_KW + hd * A_VT + A_V] = 1.0
    return jnp.asarray(kaug), jnp.asarray(taug)


def _in_proj(x, g, mod, w, wt):
    b, s, d = x.shape
    tm = A_TILE
    nb = s // tm
    n = w.shape[1]
    kaug, taug = _alibi_constants(tm)
    const = lambda shape: pl.BlockSpec(shape, lambda bi, i: (0,) * len(shape))
    return pl.pallas_call(
        _in_proj_kernel,
        grid=(b, nb),
        in_specs=[
            pl.BlockSpec((1, tm, d), lambda bi, i: (bi, i, 0)),
            const((1, d)),
            pl.BlockSpec((1, N_MOD, d), lambda bi, i: (bi, 0, 0)),
            const((d, n)), const((A_TW, d)), const((tm, A_KW)), const((A_TW, tm)),
        ],
        out_specs=[pl.BlockSpec((1, tm, wd), lambda bi, i: (bi, i, 0)) for wd in ROW_SPLITS] + [
            pl.BlockSpec((1, A_HEADS, 1, LANES, tm), lambda bi, i: (bi, 0, i, 0, 0)),
            pl.BlockSpec((1, A_HEADS, 1, A_VT, tm), lambda bi, i: (bi, 0, i, 0, 0)),
            pl.BlockSpec((1, 1, 8, LANES), lambda bi, i: (bi, i, 0, 0)),
        ],
        out_shape=[jax.ShapeDtypeStruct((b, s, wd), BF16) for wd in ROW_SPLITS] + [
            jax.ShapeDtypeStruct((b, A_HEADS, nb, LANES, tm), BF16),
            jax.ShapeDtypeStruct((b, A_HEADS, nb, A_VT, tm), BF16),
            jax.ShapeDtypeStruct((b, nb, 8, LANES), F32),
        ],
        compiler_params=_params("parallel", "parallel"),
        name="in_proj",
    )(x, g.reshape(1, d), mod, w, wt, kaug, taug)


def _diff_attn_kernel(kn_ref, slopes_ref, lq1_ref, lk1_ref, lq2_ref, lk2_ref, qt_ref, k_ref,
                      vt_ref, o_ref, q_s, m_s, acc_s, *, t, nb, lam_init):
    bi = pl.program_id(0)
    hd = pl.program_id(1)
    i = pl.program_id(2)
    slope = slopes_ref[hd]
    qt = qt_ref[0, 0, 0]
    feat = lax.broadcasted_iota(jnp.int32, (LANES, t), 0)
    zero = jnp.zeros_like(qt)
    q_s[0] = jnp.where(jnp.logical_or(feat < A_QK, feat >= A_AUG), qt, zero)
    q_s[1] = jnp.where(feat >= A_QK, qt, zero)
    qsq = qt.astype(F32) * qt.astype(F32)
    n0 = jnp.sum(jnp.where(feat < A_QK, qsq, 0.0), axis=0, keepdims=True)
    n1 = jnp.sum(jnp.where(jnp.logical_and(feat >= A_QK, feat < A_AUG), qsq, 0.0),
                 axis=0, keepdims=True)
    q_norm = jnp.max(jnp.sqrt(jnp.maximum(n0, n1)))
    causal = (lax.broadcasted_iota(jnp.int32, (t, t), 0)
              <= lax.broadcasted_iota(jnp.int32, (t, t), 1))

    def scores(j, a):
        kj = k_ref[0, pl.ds(pl.multiple_of(j * t, t), t), :]
        return _dot(kj, q_s[a])

    def exact_block(j, first):
        vtj = vt_ref[0, 0, j]
        for a in range(2):
            s = scores(j, a)
            if first:
                s = jnp.where(causal, s, NEG)
            top = jnp.max(s, axis=0, keepdims=True)
            shift = top if first else jnp.maximum(top, 0.0)
            pv = _dot(vtj, jnp.exp2(s - shift).astype(BF16))
            if first:
                acc_s[a] = pv
                m_s[a] = shift
            else:
                acc_s[a] = jnp.exp2(-shift) * acc_s[a] + pv
                m_s[a] = m_s[a] + shift

    def fast_block(j):
        vtj = vt_ref[0, 0, j]
        for a in range(2):
            acc_s[a] = acc_s[a] + _dot(vtj, jnp.exp2(scores(j, a)).astype(BF16))

    def set_reference(j):
        off = slope * ((i - j) * t).astype(F32)
        row = lax.broadcasted_iota(jnp.int32, (16, t), 0)
        for a in range(2):
            hi, mid, lo = [p.astype(F32) for p in _split_bf16(-(m_s[a] + off), 3)]
            tile = jnp.where(row == 0, hi, jnp.where(row == 1, mid, jnp.where(row == 2, lo, 0.0)))
            q_s[a, A_REF:A_REF + 16, :] = tile.astype(BF16)

    exact_block(i, True)

    def body(n, m_low):
        j = i - 1 - n
        k_norm = kn_ref[(bi * A_HEADS + hd) * nb + j]
        reach = (q_norm * k_norm + slope * (t - (i - j) * t).astype(F32) + BOUND_MARGIN) - m_low

        def visit():
            set_reference(j)

            def fast():
                fast_block(j)
                return m_low

            def exact():
                exact_block(j, False)
                return jnp.min(m_s[...])

            return lax.cond(reach <= FAST_LOG2, fast, exact)

        return lax.cond(reach < SKIP_LOG2, lambda: m_low, visit)

    lax.fori_loop(0, i, body, jnp.min(m_s[...]))

    lam = (jnp.exp(jnp.sum(lq1_ref[...] * lk1_ref[...], keepdims=True))
           - jnp.exp(jnp.sum(lq2_ref[...] * lk2_ref[...], keepdims=True)) + lam_init)
    outs = [acc_s[a, :A_V, :] / acc_s[a, A_V:A_V + 1, :] for a in range(2)]
    y_t = outs[0] - lam * outs[1]
    y_t = jnp.concatenate([y_t, jnp.zeros((LANES - A_V, t), F32)], axis=0)
    o_ref[0] = y_t.T


def _diff_attn(qt, k, vt, k_norms, lq1, lk1, lq2, lk2, lam_init):
    b, s, _ = k.shape
    t = A_TILE
    nb = s // t
    slopes = jnp.asarray([_slopes_log2(hd)[0] for hd in range(A_HEADS)], F32)
    smem = pl.BlockSpec(memory_space=pltpu.SMEM)
    vec = pl.BlockSpec((1, A_QK), lambda bi, h, i: (0, 0))
    return pl.pallas_call(
        functools.partial(_diff_attn_kernel, t=t, nb=nb, lam_init=lam_init),
        grid=(b, A_HEADS, nb),
        in_specs=[
            smem, smem, vec, vec, vec, vec,
            pl.BlockSpec((1, 1, 1, LANES, t), lambda bi, h, i: (bi, h, i, 0, 0)),
            pl.BlockSpec((1, s, LANES), lambda bi, h, i: (bi, 0, h)),
            pl.BlockSpec((1, 1, nb, A_VT, t), lambda bi, h, i: (bi, h, 0, 0, 0)),
        ],
        out_specs=pl.BlockSpec((1, t, LANES), lambda bi, h, i: (bi, i, h)),
        out_shape=jax.ShapeDtypeStruct((b, s, A_KW), F32),
        scratch_shapes=[
            pltpu.VMEM((2, LANES, t), BF16),
            pltpu.VMEM((2, 1, t), F32),
            pltpu.VMEM((2, A_VT, t), F32),
        ],
        compiler_params=_params("parallel", "parallel", "arbitrary"),
        name="diff_attn",
    )(k_norms, slopes, lq1.reshape(1, A_QK), lk1.reshape(1, A_QK), lq2.reshape(1, A_QK),
      lk2.reshape(1, A_QK), qt, k, vt)


def _sb_attn_kernel(q_ref, k_ref, v_ref, o_ref, r_s, acc_s, *, tq):
    i = pl.program_id(2)
    q = q_ref[0]
    lane = lax.broadcasted_iota(jnp.int32, (tq, LANES), 1)
    qm = [jnp.where(lane < SB_DIM, q, jnp.zeros_like(q)),
          jnp.where(lane >= SB_DIM, q, jnp.zeros_like(q))]
    row = lax.broadcasted_iota(jnp.int32, (tq, tq), 0)
    col = lax.broadcasted_iota(jnp.int32, (tq, tq), 1)
    strict = col < row
    later = (row > col).astype(BF16)

    def keys(j):
        return k_ref[0, pl.ds(pl.multiple_of(j * tq, tq), tq), :]

    def values(j):
        return v_ref[0, pl.ds(pl.multiple_of(j * tq, tq), tq), :]

    def log_weights(kj, h, diagonal):
        z = _nt_dot(qm[h], kj)
        log_beta = jnp.minimum(z, 0.0) - jnp.log(1.0 + jnp.exp(-jnp.abs(z)))
        log_1mb = log_beta - z
        if diagonal:
            log_1mb = jnp.where(strict, log_1mb, 0.0)
        hi, lo = _split_bf16(log_1mb, 2)
        after = _dot(hi, later) + _dot(lo, later)
        return log_beta + after, jnp.sum(log_1mb, axis=1, keepdims=True)

    prev = jnp.maximum(i - 1, 0)
    has_prev = (i > 0).astype(F32)
    k_diag, k_prev = keys(i), keys(prev)
    for h in range(2):
        lw_d, tot_d = log_weights(k_diag, h, True)
        lw_p, tot_p = log_weights(k_prev, h, False)
        a_d = jnp.where(strict, jnp.exp(lw_d), 0.0)
        a_p = jnp.exp(lw_p + tot_d) * has_prev
        acc_s[h] = _dot(a_d.astype(BF16), values(i)) + _dot(a_p.astype(BF16), values(prev))
        r_s[h] = tot_d + tot_p * has_prev

    def cond(carry):
        j, live = carry
        return jnp.logical_and(j >= 0, live)

    def body(carry):
        j, _ = carry
        kj, vj = keys(j), values(j)
        for h in range(2):
            lw, tot = log_weights(kj, h, False)
            run = r_s[h]
            acc_s[h] = acc_s[h] + _dot(jnp.exp(lw + run).astype(BF16), vj)
            r_s[h] = run + tot
        return j - 1, jnp.max(r_s[...]) > SB_DONE

    lax.while_loop(cond, body, (i - 2, jnp.max(r_s[...]) > SB_DONE))
    o_ref[0] = jnp.where(lane < SB_DIM, acc_s[0], acc_s[1])


def _sb_attn(q, k, v):
    b, s, _ = q.shape
    tq = min(ATT_TILE, s)
    pairs = SB_W // LANES
    return pl.pallas_call(
        functools.partial(_sb_attn_kernel, tq=tq),
        grid=(b, pairs, s // tq),
        in_specs=[
            pl.BlockSpec((1, tq, LANES), lambda bi, p, i: (bi, i, p)),
            pl.BlockSpec((1, s, LANES), lambda bi, p, i: (bi, 0, p)),
            pl.BlockSpec((1, s, LANES), lambda bi, p, i: (bi, 0, p)),
        ],
        out_specs=pl.BlockSpec((1, tq, LANES), lambda bi, p, i: (bi, i, p)),
        out_shape=jax.ShapeDtypeStruct((b, s, SB_W), F32),
        scratch_shapes=[
            pltpu.VMEM((2, tq, 1), F32),
            pltpu.VMEM((2, tq, LANES), F32),
        ],
        compiler_params=_params("parallel", "parallel", "arbitrary"),
        name="sb_attn",
    )(q, k, v)


def _swa_attn_kernel(slopes_ref, sinks_ref, q_ref, kc_ref, kp_ref, vc_ref, vp_ref, o_ref, *, tq):
    i = pl.program_id(1)
    half = tq // 2
    row = lax.broadcasted_iota(jnp.int32, (half, tq), 0)
    col = lax.broadcasted_iota(jnp.int32, (half, tq), 1)
    dist_i = row + half - col
    dist = dist_i.astype(F32)
    in_window = jnp.logical_and(dist_i >= 0, dist_i < WINDOW)
    first_key = jnp.where(i > 0, 0, half)
    first = jnp.logical_and(in_window, col >= first_key)
    lane = lax.broadcasted_iota(jnp.int32, (half, LANES), 1)

    kwin = [jnp.concatenate([kp_ref[0, half:, :], kc_ref[0, :half, :]], axis=0), kc_ref[0]]
    vwin = [jnp.concatenate([vp_ref[0, half:, :], vc_ref[0, :half, :]], axis=0), vc_ref[0]]
    for hf in range(2):
        valid = first if hf == 0 else in_window
        rows = slice(hf * half, (hf + 1) * half)
        for t in range(SWA_REP):
            q = q_ref[0, rows, t * LANES:(t + 1) * LANES]
            outs = []
            for g in range(SWA_KV_HEADS):
                head = g * SWA_REP + t
                qg = jnp.where((lane // SWA_DIM) == g, q, jnp.zeros_like(q))
                s = _nt_dot(qg, kwin[hf]) - slopes_ref[head] * dist
                s = jnp.where(valid, s, NEG)
                sink = sinks_ref[head]
                m = jnp.maximum(jnp.max(s, axis=1, keepdims=True), sink)
                p = jnp.exp(s - m)
                den = jnp.sum(p, axis=1, keepdims=True) + jnp.exp(sink - m)
                outs.append(_dot(p.astype(BF16), vwin[hf]) / den)
            o_ref[0, rows, t * LANES:(t + 1) * LANES] = jnp.where(lane < SWA_DIM, outs[0], outs[1])


def _swa_attn(q, k, v, sinks):
    b, s, _ = q.shape
    tq = min(ATT_TILE, s)
    slopes = jnp.asarray(_alibi_slopes(SWA_Q_HEADS), F32)
    smem = pl.BlockSpec(memory_space=pltpu.SMEM)
    cur = pl.BlockSpec((1, tq, SWA_KV_W), lambda bi, i: (bi, i, 0))
    prev = pl.BlockSpec((1, tq, SWA_KV_W), lambda bi, i: (bi, jnp.maximum(i - 1, 0), 0))
    return pl.pallas_call(
        functools.partial(_swa_attn_kernel, tq=tq),
        grid=(b, s // tq),
        in_specs=[
            smem, smem,
            pl.BlockSpec((1, tq, SWA_Q_W), lambda bi, i: (bi, i, 0)),
            cur, prev, cur, prev,
        ],
        out_specs=pl.BlockSpec((1, tq, SWA_Q_W), lambda bi, i: (bi, i, 0)),
        out_shape=jax.ShapeDtypeStruct((b, s, SWA_Q_W), F32),
        compiler_params=_params("parallel", "parallel"),
        name="swa_attn",
    )(slopes, sinks.astype(F32), q, k, k, v, v)


def _mix_ffn_kernel(ya_ref, yb_ref, yc_ref, x_ref, ga_ref, gb_ref, gc_ref, ln2_ref, mod_ref,
                    wa_ref, wb_ref, wc_ref, wg_ref, wu_ref, wd_ref, fg_ref, o_ref,
                    *, a_scale, final):
    mixed = _dot(_rms(yb_ref[0], gb_ref[...]).astype(BF16), wb_ref[...])
    mixed += _dot(_rms(yc_ref[0], gc_ref[...]).astype(BF16), wc_ref[...])
    for h in range(A_HEADS):
        cols = slice(h * LANES, (h + 1) * LANES)
        ya = ya_ref[0, :, cols]
        ms = jnp.sum(ya * ya, axis=-1, keepdims=True) * (1.0 / A_V)
        na = ya * lax.rsqrt(ms + EPS) * ga_ref[:, cols] * a_scale
        mixed += _dot(na.astype(BF16), wa_ref[cols, :])
    x1 = x_ref[0] + mod_ref[0, 2:3, :] * mixed
    h2 = (_rms(x1, ln2_ref[...]) * (1.0 + mod_ref[0, 4:5, :]) + mod_ref[0, 3:4, :]).astype(BF16)
    gate = _dot(h2, wg_ref[...])
    up = _dot(h2, wu_ref[...])
    act = gate * jax.nn.sigmoid(gate) * up
    out = x1 + mod_ref[0, 5:6, :] * _dot(act.astype(BF16), wd_ref[...])
    if final:
        out = _rms(out, fg_ref[...])
    o_ref[0] = out


def _mix_ffn(ya, yb, yc, x, ga, gb, gc, ln2, mod, wa, wb, wc, wg, wu, wd, fg, a_scale, final):
    b, s, d = x.shape
    f = wg.shape[1]
    tm = min(FFN_TILE, s)
    row = lambda wd_: pl.BlockSpec((1, tm, wd_), lambda bi, i: (bi, i, 0))
    const = lambda r, cdim: pl.BlockSpec((r, cdim), lambda bi, i: (0, 0),
                                         pipeline_mode=pl.Buffered(1))
    return pl.pallas_call(
        functools.partial(_mix_ffn_kernel, a_scale=a_scale, final=final),
        grid=(b, s // tm),
        in_specs=[
            row(A_KW), row(SB_W), row(SWA_Q_W), row(d),
            const(1, A_KW), const(1, SB_W), const(1, SWA_Q_W), const(1, d),
            pl.BlockSpec((1, N_MOD, d), lambda bi, i: (bi, 0, 0)),
            const(A_KW, d), const(SB_W, d), const(SWA_Q_W, d),
            const(d, f), const(d, f), const(f, d), const(1, d),
        ],
        out_specs=row(d),
        out_shape=jax.ShapeDtypeStruct((b, s, d), F32),
        compiler_params=_params("parallel", "parallel"),
        name="mix_ffn",
    )(ya, yb, yc, x, ga, gb, gc, ln2.reshape(1, d), mod, wa, wb, wc, wg, wu, wd, fg.reshape(1, d))


def _swa_perm():
    cols = []
    for t in range(SWA_REP):
        for g in range(SWA_KV_HEADS):
            head = g * SWA_REP + t
            cols.extend(range(head * SWA_DIM, (head + 1) * SWA_DIM))
    return jnp.asarray(cols, jnp.int32)


def _pad_heads(w, per_head, width):
    d = w.shape[0]
    w = w.reshape(d, A_HEADS, per_head)
    return jnp.pad(w, ((0, 0), (0, 0), (0, width - per_head))).reshape(d, A_HEADS * width)


def _prep_w_in(w):
    bounds = np.cumsum((0, A_W, A_W, A_W, SB_W, SB_W, SB_W, SWA_Q_W, SWA_KV_W, SWA_KV_W))
    qa, ka, va, qb, kb, vb, qc, kc, vc = [w[:, bounds[n]:bounds[n + 1]] for n in range(9)]
    row = jnp.concatenate([
        _pad_heads(ka, 2 * A_QK, LANES), qb * (SB_DIM ** -0.5), kb, vb,
        (qc * (SWA_DIM ** -0.5))[:, _swa_perm()], kc, vc], axis=1)
    transposed = jnp.concatenate([
        _pad_heads(qa * (A_QK ** -0.5 * LOG2E), 2 * A_QK, LANES), _pad_heads(va, A_V, A_VT)], axis=1).T
    return row.astype(BF16), transposed.astype(BF16)


def kernel(x, c, ln1_g, ln2_g, w_mod, b_mod, w_in, lam_q1, lam_k1, lam_q2, lam_k2, diff_norm_g,
           sb_norm_g, swa_norm_g, swa_sinks, w_out, w_gate, w_up, w_down, final_g):
    depth = w_in.shape[0]
    perm = _swa_perm()
    mod = _modulation(c, w_mod, b_mod)
    for l in range(depth):
        lam_init = 0.8 - 0.6 * math.exp(-0.3 * l)
        w_row, w_t = _prep_w_in(w_in[l])
        ka, qb, kb, vb, qc, kc, vc, qta, vta, kn = _in_proj(x, ln1_g[l], mod[l], w_row, w_t)
        k_norms = kn[:, :, 0, :A_HEADS].transpose(0, 2, 1).reshape(-1)
        ya = _diff_attn(qta, ka, vta, k_norms, lam_q1[l], lam_k1[l], lam_q2[l], lam_k2[l], lam_init)
        yb = _sb_attn(qb, kb, vb)
        yc = _swa_attn(qc, kc, vc, swa_sinks[l])
        wo = w_out[l].astype(BF16)
        x = _mix_ffn(
            ya, yb, yc, x,
            _pad_heads(diff_norm_g[l].reshape(1, A_V).repeat(A_HEADS, 0).reshape(1, A_W), A_V, LANES),
            sb_norm_g[l].reshape(1, SB_W),
            swa_norm_g[l][perm].reshape(1, SWA_Q_W),
            ln2_g[l], mod[l],
            _pad_heads(wo[:A_W].T, A_V, LANES).T, wo[A_W:A_W + SB_W], wo[A_W + SB_W:][perm],
            w_gate[l].astype(BF16), w_up[l].astype(BF16), w_down[l].astype(BF16), final_g,
            1.0 - lam_init, l == depth - 1)
    return x
```

```python
import functools
import math

import numpy as np
import jax
import jax.numpy as jnp
from jax import lax
from jax.experimental import pallas as pl
from jax.experimental.pallas import tpu as pltpu

F32 = jnp.float32
BF16 = jnp.bfloat16

N_MOD = 6
EPS = 1e-6
A_HEADS = 4
A_QK = 32
A_V = 64
SB_HEADS = 4
SB_DIM = 64
SWA_Q_HEADS = 8
SWA_KV_HEADS = 2
SWA_REP = 4
SWA_DIM = 64
WINDOW = 128
A_W = 256
SB_W = 256
SWA_Q_W = 512
SWA_KV_W = 128

LANES = 128
NEG = -1e30

VMEM_LIMIT = 56 * 1024 * 1024

ROW_TILE = 512
FFN_TILE = 256
ATT_TILE = 256
A_TILE = ROW_TILE

LOG2E = math.log2(math.e)
A_AUG = 2 * A_QK
A_REF = A_AUG + 16
A_PIECES = 4
A_VT = 80
A_KW = A_HEADS * LANES
A_TW = A_HEADS * (LANES + A_VT)
SKIP_LOG2 = -160.0
FAST_LOG2 = 80.0
BOUND_MARGIN = 1.0


def _params(*sem):
    return pltpu.CompilerParams(dimension_semantics=sem, vmem_limit_bytes=VMEM_LIMIT)


def _nt_dot(a, b):
    return lax.dot_general(a, b, (((1,), (1,)), ((), ())), preferred_element_type=F32)


def _dot(a, b):
    return jnp.dot(a, b, preferred_element_type=F32)


def _rms(x, g):
    ms = jnp.mean(x * x, axis=-1, keepdims=True)
    return x * lax.rsqrt(ms + EPS) * g


def _alibi_slopes(n):
    return 2.0 ** (-8.0 * np.arange(1, n + 1, dtype=np.float64) / n)


def _split_bf16(x, n):
    pieces = []
    for _ in range(n - 1):
        p = x.astype(BF16)
        pieces.append(p)
        x = x - p.astype(F32)
    pieces.append(x.astype(BF16))
    return pieces


def _mod_kernel(c_ref, w_ref, b_ref, o_ref):
    cv = c_ref[...]
    ca = cv * jax.nn.sigmoid(cv)
    o_ref[0] = jnp.dot(ca, w_ref[0], preferred_element_type=F32,
                       precision=lax.Precision.HIGHEST) + b_ref[0]


def _modulation(c, w_mod, b_mod):
    depth, d, n = w_mod.shape
    b = c.shape[0]
    rows = 8
    cp = jnp.zeros((rows, d), F32).at[:b].set(c)
    tn = 1024
    out = pl.pallas_call(
        _mod_kernel,
        grid=(depth, n // tn),
        in_specs=[
            pl.BlockSpec((rows, d), lambda l, j: (0, 0)),
            pl.BlockSpec((1, d, tn), lambda l, j: (l, 0, j)),
            pl.BlockSpec((1, 1, tn), lambda l, j: (l, 0, j)),
        ],
        out_specs=pl.BlockSpec((1, rows, tn), lambda l, j: (l, 0, j)),
        out_shape=jax.ShapeDtypeStruct((depth, rows, n), F32),
        compiler_params=_params("parallel", "parallel"),
        name="modulation",
    )(cp, w_mod, b_mod.reshape(depth, 1, n))
    return out[:, :b].reshape(depth, b, N_MOD, d)


ROW_SPLITS = (A_KW, SB_W, SB_W, SB_W, SWA_Q_W, SWA_KV_W, SWA_KV_W)


def _in_proj_kernel(x_ref, g_ref, mod_ref, w_ref, wt_ref, kaug_ref, taug_ref, *out_refs):
    row_refs, (qt_ref, vt_ref, kn_ref) = out_refs[:len(ROW_SPLITS)], out_refs[len(ROW_SPLITS):]
    x = x_ref[0]
    h = _rms(x, g_ref[...]) * (1.0 + mod_ref[0, 1:2, :]) + mod_ref[0, 0:1, :]
    hb = h.astype(BF16)
    proj = _dot(hb, w_ref[...])
    start = 0
    for n, (ref, width) in enumerate(zip(row_refs, ROW_SPLITS)):
        part = proj[:, start:start + width]
        if n == 0:
            part = part + kaug_ref[...]
            keys = part.astype(BF16)
        ref[0] = part.astype(BF16)
        start += width
    feat = lax.broadcasted_iota(jnp.int32, (x.shape[0], LANES), 1)
    out_lane = lax.broadcasted_iota(jnp.int32, kn_ref.shape[2:], 1)
    norms = jnp.zeros(kn_ref.shape[2:], F32)
    for hd in range(A_HEADS):
        kf = keys[:, hd * LANES:(hd + 1) * LANES].astype(F32)
        sq = jnp.sum(jnp.where(feat < A_AUG, kf * kf, 0.0), axis=1, keepdims=True)
        norms = jnp.where(out_lane == hd, jnp.sqrt(jnp.max(sq, axis=0, keepdims=True)), norms)
    kn_ref[0, 0] = norms
    proj_t = _nt_dot(wt_ref[...], hb) + taug_ref[...]
    for hd in range(A_HEADS):
        qt_ref[0, hd, 0] = proj_t[hd * LANES:(hd + 1) * LANES].astype(BF16)
        v0 = A_KW + hd * A_VT
        vt_ref[0, hd, 0] = proj_t[v0:v0 + A_VT].astype(BF16)


def _bf16_pieces(value, n):
    pieces = []
    rest = float(value)
    for _ in range(n):
        p = float(np.asarray(rest, np.float32).astype(jnp.bfloat16).astype(np.float64))
        pieces.append(p)
        rest -= p
    return pieces


def _slopes_log2(hd):
    pieces = _bf16_pieces(_alibi_slopes(A_HEADS)[hd] * LOG2E, A_PIECES)
    return sum(pieces), pieces


def _alibi_constants(t):
    idx = np.arange(t)
    lo, hi = idx % 256, idx - idx % 256
    kaug = np.zeros((t, A_KW), np.float32)
    taug = np.zeros((A_TW, t), np.float32)
    for hd in range(A_HEADS):
        _, pieces = _slopes_log2(hd)
        k0 = hd * LANES + A_AUG
        for n, piece in enumerate(pieces):
            taug[k0 + n] = -lo
            taug[k0 + A_PIECES + n] = -hi
            kaug[:, k0 + n] = piece
            kaug[:, k0 + A_PIECES + n] = piece
            taug[k0 + 2 * A_PIECES + n] = piece
            taug[k0 + 3 * A_PIECES + n] = piece
            kaug[:, k0 + 2 * A_PIECES + n] = lo
            kaug[:, k0 + 3 * A_PIECES + n] = hi
        r0 = hd * LANES + A_REF
        kaug[:, r0:r0 + 3] = 1.0
        taug[A_KW + hd * A_VT + A_V] = 1.0
    return jnp.asarray(kaug), jnp.asarray(taug)


def _in_proj(x, g, mod, w, wt):
    b, s, d = x.shape
    tm = A_TILE
    nb = s // tm
    n = w.shape[1]
    kaug, taug = _alibi_constants(tm)
    const = lambda shape: pl.BlockSpec(shape, lambda bi, i: (0,) * len(shape))
    return pl.pallas_call(
        _in_proj_kernel,
        grid=(b, nb),
        in_specs=[
            pl.BlockSpec((1, tm, d), lambda bi, i: (bi, i, 0)),
            const((1, d)),
            pl.BlockSpec((1, N_MOD, d), lambda bi, i: (bi, 0, 0)),
            const((d, n)), const((A_TW, d)), const((tm, A_KW)), const((A_TW, tm)),
        ],
        out_specs=[pl.BlockSpec((1, tm, wd), lambda bi, i: (bi, i, 0)) for wd in ROW_SPLITS] + [
            pl.BlockSpec((1, A_HEADS, 1, LANES, tm), lambda bi, i: (bi, 0, i, 0, 0)),
            pl.BlockSpec((1, A_HEADS, 1, A_VT, tm), lambda bi, i: (bi, 0, i, 0, 0)),
            pl.BlockSpec((1, 1, 8, LANES), lambda bi, i: (bi, i, 0, 0)),
        ],
        out_shape=[jax.ShapeDtypeStruct((b, s, wd), BF16) for wd in ROW_SPLITS] + [
            jax.ShapeDtypeStruct((b, A_HEADS, nb, LANES, tm), BF16),
            jax.ShapeDtypeStruct((b, A_HEADS, nb, A_VT, tm), BF16),
            jax.ShapeDtypeStruct((b, nb, 8, LANES), F32),
        ],
        compiler_params=_params("parallel", "parallel"),
        name="in_proj",
    )(x, g.reshape(1, d), mod, w, wt, kaug, taug)


def _diff_attn_kernel(kn_ref, slopes_ref, lq1_ref, lk1_ref, lq2_ref, lk2_ref, qt_ref, k_ref,
                      vt_ref, o_ref, q_s, m_s, acc_s, *, t, nb, lam_init):
    bi = pl.program_id(0)
    hd = pl.program_id(1)
    i = pl.program_id(2)
    slope = slopes_ref[hd]
    qt = qt_ref[0, 0, 0]
    feat = lax.broadcasted_iota(jnp.int32, (LANES, t), 0)
    zero = jnp.zeros_like(qt)
    for slot in range(2):
        q_s[slot, 0] = jnp.where(jnp.logical_or(feat < A_QK, feat >= A_AUG), qt, zero)
        q_s[slot, 1] = jnp.where(feat >= A_QK, qt, zero)
    qsq = qt.astype(F32) * qt.astype(F32)
    n0 = jnp.sum(jnp.where(feat < A_QK, qsq, 0.0), axis=0, keepdims=True)
    n1 = jnp.sum(jnp.where(jnp.logical_and(feat >= A_QK, feat < A_AUG), qsq, 0.0),
                 axis=0, keepdims=True)
    q_norm = jnp.max(jnp.sqrt(jnp.maximum(n0, n1)))
    causal = (lax.broadcasted_iota(jnp.int32, (t, t), 0)
              <= lax.broadcasted_iota(jnp.int32, (t, t), 1))

    def scores(j, a, slot):
        kj = k_ref[0, pl.ds(pl.multiple_of(j * t, t), t), :]
        return _dot(kj, q_s[slot, a])

    def exact_block(j, first):
        vtj = vt_ref[0, 0, j]
        for a in range(2):
            s = scores(j, a, 0)
            if first:
                s = jnp.where(causal, s, NEG)
            top = jnp.max(s, axis=0, keepdims=True)
            shift = top if first else jnp.maximum(top, 0.0)
            pv = _dot(vtj, jnp.exp2(s - shift).astype(BF16))
            if first:
                acc_s[a] = pv
                m_s[a] = shift
            else:
                acc_s[a] = jnp.exp2(-shift) * acc_s[a] + pv
                m_s[a] = m_s[a] + shift

    def fast_weights(j, a, slot):
        return jnp.exp2(scores(j, a, slot).astype(BF16))

    def fast_block(j):
        vtj = vt_ref[0, 0, j]
        for a in range(2):
            acc_s[a] = acc_s[a] + _dot(vtj, fast_weights(j, a, 0))

    def fast_pair(j):
        set_reference(j, 0)
        set_reference(j - 1, 1)
        vt2 = jnp.concatenate([vt_ref[0, 0, j], vt_ref[0, 0, j - 1]], axis=1)
        for a in range(2):
            p2 = jnp.concatenate([fast_weights(j, a, 0), fast_weights(j - 1, a, 1)], axis=0)
            acc_s[a] = acc_s[a] + _dot(vt2, p2)

    def set_reference(j, slot):
        off = slope * ((i - j) * t).astype(F32)
        row = lax.broadcasted_iota(jnp.int32, (16, t), 0)
        for a in range(2):
            hi, mid, lo = [p.astype(F32) for p in _split_bf16(-(m_s[a] + off), 3)]
            tile = jnp.where(row == 0, hi, jnp.where(row == 1, mid, jnp.where(row == 2, lo, 0.0)))
            q_s[slot, a, A_REF:A_REF + 16, :] = tile.astype(BF16)

    exact_block(i, True)

    def reach_of(j, m_low):
        k_norm = kn_ref[(bi * A_HEADS + hd) * nb + j]
        return (q_norm * k_norm + slope * (t - (i - j) * t).astype(F32) + BOUND_MARGIN) - m_low

    def single(j, m_low):
        reach = reach_of(j, m_low)

        def visit():
            set_reference(j, 0)

            def fast():
                fast_block(j)
                return m_low

            def exact():
                exact_block(j, False)
                return jnp.min(m_s[...])

            return lax.cond(reach <= FAST_LOG2, fast, exact)

        return lax.cond(reach < SKIP_LOG2, lambda: m_low, visit)

    def pair_body(n, m_low):
        j = i - 1 - 2 * n

        def is_fast(reach):
            return jnp.logical_and(reach >= SKIP_LOG2, reach <= FAST_LOG2)

        def pair():
            fast_pair(j)
            return m_low

        def one_by_one():
            return single(j - 1, single(j, m_low))

        both = jnp.logical_and(is_fast(reach_of(j, m_low)), is_fast(reach_of(j - 1, m_low)))
        return lax.cond(both, pair, one_by_one)

    m_low = lax.fori_loop(0, i // 2, pair_body, jnp.min(m_s[...]))
    lax.cond(i % 2 == 1, lambda: single(0, m_low), lambda: m_low)

    lam = (jnp.exp(jnp.sum(lq1_ref[...] * lk1_ref[...], keepdims=True))
           - jnp.exp(jnp.sum(lq2_ref[...] * lk2_ref[...], keepdims=True)) + lam_init)
    outs = [acc_s[a, :A_V, :] / acc_s[a, A_V:A_V + 1, :] for a in range(2)]
    y_t = outs[0] - lam * outs[1]
    y_t = jnp.concatenate([y_t, jnp.zeros((LANES - A_V, t), F32)], axis=0)
    o_ref[0] = y_t.T


def _diff_attn(qt, k, vt, k_norms, lq1, lk1, lq2, lk2, lam_init):
    b, s, _ = k.shape
    t = A_TILE
    nb = s // t
    slopes = jnp.asarray([_slopes_log2(hd)[0] for hd in range(A_HEADS)], F32)
    smem = pl.BlockSpec(memory_space=pltpu.SMEM)
    vec = pl.BlockSpec((1, A_QK), lambda bi, h, i: (0, 0))
    return pl.pallas_call(
        functools.partial(_diff_attn_kernel, t=t, nb=nb, lam_init=lam_init),
        grid=(b, A_HEADS, nb),
        in_specs=[
            smem, smem, vec, vec, vec, vec,
            pl.BlockSpec((1, 1, 1, LANES, t), lambda bi, h, i: (bi, h, i, 0, 0)),
            pl.BlockSpec((1, s, LANES), lambda bi, h, i: (bi, 0, h)),
            pl.BlockSpec((1, 1, nb, A_VT, t), lambda bi, h, i: (bi, h, 0, 0, 0)),
        ],
        out_specs=pl.BlockSpec((1, t, LANES), lambda bi, h, i: (bi, i, h)),
        out_shape=jax.ShapeDtypeStruct((b, s, A_KW), F32),
        scratch_shapes=[
            pltpu.VMEM((2, 2, LANES, t), BF16),
            pltpu.VMEM((2, 1, t), F32),
            pltpu.VMEM((2, A_VT, t), F32),
        ],
        compiler_params=_params("parallel", "parallel", "arbitrary"),
        name="diff_attn",
    )(k_norms, slopes, lq1.reshape(1, A_QK), lk1.reshape(1, A_QK), lq2.reshape(1, A_QK),
      lk2.reshape(1, A_QK), qt, k, vt)


def _sb_attn_kernel(q_ref, k_ref, v_ref, o_ref, r_s, acc_s, *, tq):
    i = pl.program_id(2)
    q = q_ref[0]
    lane = lax.broadcasted_iota(jnp.int32, (tq, LANES), 1)
    qm = [jnp.where(lane < SB_DIM, q, jnp.zeros_like(q)),
          jnp.where(lane >= SB_DIM, q, jnp.zeros_like(q))]
    row = lax.broadcasted_iota(jnp.int32, (tq, tq), 0)
    col = lax.broadcasted_iota(jnp.int32, (tq, tq), 1)
    strict = col < row
    later = (row > col).astype(BF16)

    def keys(j):
        return k_ref[0, pl.ds(pl.multiple_of(j * tq, tq), tq), :]

    def values(j):
        return v_ref[0, pl.ds(pl.multiple_of(j * tq, tq), tq), :]

    def log_weights(kj, h, diagonal):
        z = _nt_dot(qm[h], kj)
        log_beta = jnp.minimum(z, 0.0) - jnp.log2(1.0 + jnp.exp2(-jnp.abs(z)))
        log_1mb = log_beta - z
        if diagonal:
            log_1mb = jnp.where(strict, log_1mb, 0.0)
        after = _dot(jnp.concatenate(_split_bf16(log_1mb, 2), axis=1), later2)
        return log_beta + after, jnp.sum(log_1mb, axis=1, keepdims=True)

    later2 = jnp.concatenate([later, later], axis=0)
    prev = jnp.maximum(i - 1, 0)
    has_prev = (i > 0).astype(F32)
    k_diag, k_prev = keys(i), keys(prev)
    v_both = jnp.concatenate([values(i), values(prev)], axis=0)
    for h in range(2):
        lw_d, tot_d = log_weights(k_diag, h, True)
        lw_p, tot_p = log_weights(k_prev, h, False)
        a_d = jnp.where(strict, jnp.exp2(lw_d), 0.0)
        a_p = jnp.exp2(lw_p + tot_d) * has_prev
        acc_s[h] = _dot(jnp.concatenate([a_d.astype(BF16), a_p.astype(BF16)], axis=1), v_both)
        r_s[h] = tot_d + tot_p * has_prev

    def cond(carry):
        j, live = carry
        return jnp.logical_and(j >= 0, live)

    def body(carry):
        j, _ = carry
        kj, vj = keys(j), values(j)
        for h in range(2):
            lw, tot = log_weights(kj, h, False)
            run = r_s[h]
            acc_s[h] = acc_s[h] + _dot(jnp.exp2(lw + run).astype(BF16), vj)
            r_s[h] = run + tot
        return j - 1, jnp.max(r_s[...]) > SKIP_LOG2

    lax.while_loop(cond, body, (i - 2, jnp.max(r_s[...]) > SKIP_LOG2))
    o_ref[0] = jnp.where(lane < SB_DIM, acc_s[0], acc_s[1])


def _sb_attn(q, k, v):
    b, s, _ = q.shape
    tq = min(ATT_TILE, s)
    pairs = SB_W // LANES
    return pl.pallas_call(
        functools.partial(_sb_attn_kernel, tq=tq),
        grid=(b, pairs, s // tq),
        in_specs=[
            pl.BlockSpec((1, tq, LANES), lambda bi, p, i: (bi, i, p)),
            pl.BlockSpec((1, s, LANES), lambda bi, p, i: (bi, 0, p)),
            pl.BlockSpec((1, s, LANES), lambda bi, p, i: (bi, 0, p)),
        ],
        out_specs=pl.BlockSpec((1, tq, LANES), lambda bi, p, i: (bi, i, p)),
        out_shape=jax.ShapeDtypeStruct((b, s, SB_W), F32),
        scratch_shapes=[
            pltpu.VMEM((2, tq, 1), F32),
            pltpu.VMEM((2, tq, LANES), F32),
        ],
        compiler_params=_params("parallel", "parallel", "arbitrary"),
        name="sb_attn",
    )(q, k, v)


def _swa_attn_kernel(sinks_ref, bias_ref, q_ref, kc_ref, kp_ref, vc_ref, vp_ref, o_ref, *, tq):
    i = pl.program_id(1)
    half = tq // 2
    col = lax.broadcasted_iota(jnp.int32, (half, tq), 1)
    has_key = col >= jnp.where(i > 0, 0, half)
    lane = lax.broadcasted_iota(jnp.int32, (half, LANES), 1)

    kwin = [jnp.concatenate([kp_ref[0, half:, :], kc_ref[0, :half, :]], axis=0), kc_ref[0]]
    vwin = [jnp.concatenate([vp_ref[0, half:, :], vc_ref[0, :half, :]], axis=0), vc_ref[0]]
    for hf in range(2):
        rows = slice(hf * half, (hf + 1) * half)
        for t in range(SWA_REP):
            q = q_ref[0, rows, t * LANES:(t + 1) * LANES]
            outs = []
            for g in range(SWA_KV_HEADS):
                head = g * SWA_REP + t
                qg = jnp.where((lane // SWA_DIM) == g, q, jnp.zeros_like(q))
                s = _nt_dot(qg, kwin[hf]) + bias_ref[head]
                if hf == 0:
                    s = jnp.where(has_key, s, NEG)
                sink = sinks_ref[head] * LOG2E
                m = jnp.maximum(jnp.max(s, axis=1, keepdims=True), sink)
                p = jnp.exp2(s - m)
                den = jnp.sum(p, axis=1, keepdims=True) + jnp.exp2(sink - m)
                outs.append(_dot(p.astype(BF16), vwin[hf]) / den)
            o_ref[0, rows, t * LANES:(t + 1) * LANES] = jnp.where(lane < SWA_DIM, outs[0], outs[1])


def _swa_bias(tq):
    half = tq // 2
    dist = np.arange(half)[:, None] + half - np.arange(tq)[None, :]
    in_window = (dist >= 0) & (dist < WINDOW)
    slopes = _alibi_slopes(SWA_Q_HEADS) * LOG2E
    bias = np.where(in_window[None], -slopes[:, None, None] * dist[None], NEG)
    return jnp.asarray(bias, F32)


def _swa_attn(q, k, v, sinks):
    b, s, _ = q.shape
    tq = min(ATT_TILE, s)
    smem = pl.BlockSpec(memory_space=pltpu.SMEM)
    cur = pl.BlockSpec((1, tq, SWA_KV_W), lambda bi, i: (bi, i, 0))
    prev = pl.BlockSpec((1, tq, SWA_KV_W), lambda bi, i: (bi, jnp.maximum(i - 1, 0), 0))
    return pl.pallas_call(
        functools.partial(_swa_attn_kernel, tq=tq),
        grid=(b, s // tq),
        in_specs=[
            smem,
            pl.BlockSpec((SWA_Q_HEADS, tq // 2, tq), lambda bi, i: (0, 0, 0)),
            pl.BlockSpec((1, tq, SWA_Q_W), lambda bi, i: (bi, i, 0)),
            cur, prev, cur, prev,
        ],
        out_specs=pl.BlockSpec((1, tq, SWA_Q_W), lambda bi, i: (bi, i, 0)),
        out_shape=jax.ShapeDtypeStruct((b, s, SWA_Q_W), F32),
        compiler_params=_params("parallel", "parallel"),
        name="swa_attn",
    )(sinks.astype(F32), _swa_bias(tq), q, k, k, v, v)


def _mix_ffn_kernel(ya_ref, yb_ref, yc_ref, x_ref, ga_ref, gb_ref, gc_ref, ln2_ref, mod_ref,
                    wa_ref, wb_ref, wc_ref, wg_ref, wu_ref, wd_ref, fg_ref, o_ref,
                    *, a_scale, final):
    mixed = _dot(_rms(yb_ref[0], gb_ref[...]).astype(BF16), wb_ref[...])
    mixed += _dot(_rms(yc_ref[0], gc_ref[...]).astype(BF16), wc_ref[...])
    for h in range(A_HEADS):
        cols = slice(h * LANES, (h + 1) * LANES)
        ya = ya_ref[0, :, cols]
        ms = jnp.sum(ya * ya, axis=-1, keepdims=True) * (1.0 / A_V)
        na = ya * lax.rsqrt(ms + EPS) * ga_ref[:, cols] * a_scale
        mixed += _dot(na.astype(BF16), wa_ref[cols, :])
    x1 = x_ref[0] + mod_ref[0, 2:3, :] * mixed
    h2 = (_rms(x1, ln2_ref[...]) * (1.0 + mod_ref[0, 4:5, :]) + mod_ref[0, 3:4, :]).astype(BF16)
    gate = _dot(h2, wg_ref[...])
    up = _dot(h2, wu_ref[...])
    act = gate * jax.nn.sigmoid(gate) * up
    out = x1 + mod_ref[0, 5:6, :] * _dot(act.astype(BF16), wd_ref[...])
    if final:
        out = _rms(out, fg_ref[...])
    o_ref[0] = out


def _mix_ffn(ya, yb, yc, x, ga, gb, gc, ln2, mod, wa, wb, wc, wg, wu, wd, fg, a_scale, final):
    b, s, d = x.shape
    f = wg.shape[1]
    tm = min(FFN_TILE, s)
    row = lambda wd_: pl.BlockSpec((1, tm, wd_), lambda bi, i: (bi, i, 0))
    const = lambda r, cdim: pl.BlockSpec((r, cdim), lambda bi, i: (0, 0),
                                         pipeline_mode=pl.Buffered(1))
    return pl.pallas_call(
        functools.partial(_mix_ffn_kernel, a_scale=a_scale, final=final),
        grid=(b, s // tm),
        in_specs=[
            row(A_KW), row(SB_W), row(SWA_Q_W), row(d),
            const(1, A_KW), const(1, SB_W), const(1, SWA_Q_W), const(1, d),
            pl.BlockSpec((1, N_MOD, d), lambda bi, i: (bi, 0, 0)),
            const(A_KW, d), const(SB_W, d), const(SWA_Q_W, d),
            const(d, f), const(d, f), const(f, d), const(1, d),
        ],
        out_specs=row(d),
        out_shape=jax.ShapeDtypeStruct((b, s, d), F32),
        compiler_params=_params("parallel", "parallel"),
        name="mix_ffn",
    )(ya, yb, yc, x, ga, gb, gc, ln2.reshape(1, d), mod, wa, wb, wc, wg, wu, wd, fg.reshape(1, d))


def _swa_perm():
    cols = []
    for t in range(SWA_REP):
        for g in range(SWA_KV_HEADS):
            head = g * SWA_REP + t
            cols.extend(range(head * SWA_DIM, (head + 1) * SWA_DIM))
    return jnp.asarray(cols, jnp.int32)


def _pad_heads(w, per_head, width):
    d = w.shape[0]
    w = w.reshape(d, A_HEADS, per_head)
    return jnp.pad(w, ((0, 0), (0, 0), (0, width - per_head))).reshape(d, A_HEADS * width)


def _prep_w_in(w):
    bounds = np.cumsum((0, A_W, A_W, A_W, SB_W, SB_W, SB_W, SWA_Q_W, SWA_KV_W, SWA_KV_W))
    qa, ka, va, qb, kb, vb, qc, kc, vc = [w[:, bounds[n]:bounds[n + 1]] for n in range(9)]
    row = jnp.concatenate([
        _pad_heads(ka, 2 * A_QK, LANES), qb * (SB_DIM ** -0.5 * LOG2E), kb, vb,
        (qc * (SWA_DIM ** -0.5 * LOG2E))[:, _swa_perm()], kc, vc], axis=1)
    transposed = jnp.concatenate([
        _pad_heads(qa * (A_QK ** -0.5 * LOG2E), 2 * A_QK, LANES), _pad_heads(va, A_V, A_VT)], axis=1).T
    return row.astype(BF16), transposed.astype(BF16)


def kernel(x, c, ln1_g, ln2_g, w_mod, b_mod, w_in, lam_q1, lam_k1, lam_q2, lam_k2, diff_norm_g,
           sb_norm_g, swa_norm_g, swa_sinks, w_out, w_gate, w_up, w_down, final_g):
    depth = w_in.shape[0]
    perm = _swa_perm()
    mod = _modulation(c, w_mod, b_mod)
    for l in range(depth):
        lam_init = 0.8 - 0.6 * math.exp(-0.3 * l)
        w_row, w_t = _prep_w_in(w_in[l])
        ka, qb, kb, vb, qc, kc, vc, qta, vta, kn = _in_proj(x, ln1_g[l], mod[l], w_row, w_t)
        k_norms = kn[:, :, 0, :A_HEADS].transpose(0, 2, 1).reshape(-1)
        ya = _diff_attn(qta, ka, vta, k_norms, lam_q1[l], lam_k1[l], lam_q2[l], lam_k2[l], lam_init)
        yb = _sb_attn(qb, kb, vb)
        yc = _swa_attn(qc, kc, vc, swa_sinks[l])
        wo = w_out[l].astype(BF16)
        x = _mix_ffn(
            ya, yb, yc, x,
            _pad_heads(diff_norm_g[l].reshape(1, A_V).repeat(A_HEADS, 0).reshape(1, A_W), A_V, LANES),
            sb_norm_g[l].reshape(1, SB_W),
            swa_norm_g[l][perm].reshape(1, SWA_Q_W),
            ln2_g[l], mod[l],
            _pad_heads(wo[:A_W].T, A_V, LANES).T, wo[A_W:A_W + SB_W], wo[A_W + SB_W:][perm],
            w_gate[l].astype(BF16), w_up[l].astype(BF16), w_down[l].astype(BF16), final_g,
            1.0 - lam_init, l == depth - 1)
    return x
```

```python
import functools
import math

import numpy as np
import jax
import jax.numpy as jnp
from jax import lax
from jax.experimental import pallas as pl
from jax.experimental.pallas import tpu as pltpu

F32 = jnp.float32
BF16 = jnp.bfloat16

N_MOD = 6
EPS = 1e-6
A_HEADS = 4
A_QK = 32
A_V = 64
SB_HEADS = 4
SB_DIM = 64
SWA_Q_HEADS = 8
SWA_KV_HEADS = 2
SWA_REP = 4
SWA_DIM = 64
WINDOW = 128
A_W = 256
SB_W = 256
SWA_Q_W = 512
SWA_KV_W = 128

LANES = 128
NEG = -1e30

VMEM_LIMIT = 56 * 1024 * 1024

ROW_TILE = 512
FFN_TILE = 256
ATT_TILE = 256
A_TILE = ROW_TILE

LOG2E = math.log2(math.e)
A_AUG = 2 * A_QK
A_REF = A_AUG + 16
A_PIECES = 4
A_VT = 80
A_KW = A_HEADS * LANES
A_TW = A_HEADS * (LANES + A_VT)
SKIP_LOG2 = -160.0
FAST_LOG2 = 80.0
A_GROUPS = (4, 2)
ZERO_REF_LOG2 = 60.0
BOUND_MARGIN = 1.0


def _params(*sem):
    return pltpu.CompilerParams(dimension_semantics=sem, vmem_limit_bytes=VMEM_LIMIT)


def _nt_dot(a, b):
    return lax.dot_general(a, b, (((1,), (1,)), ((), ())), preferred_element_type=F32)


def _dot(a, b):
    return jnp.dot(a, b, preferred_element_type=F32)


def _rms(x, g):
    ms = jnp.mean(x * x, axis=-1, keepdims=True)
    return x * lax.rsqrt(ms + EPS) * g


def _alibi_slopes(n):
    return 2.0 ** (-8.0 * np.arange(1, n + 1, dtype=np.float64) / n)


def _split_bf16(x, n):
    pieces = []
    for _ in range(n - 1):
        p = x.astype(BF16)
        pieces.append(p)
        x = x - p.astype(F32)
    pieces.append(x.astype(BF16))
    return pieces


def _mod_kernel(c_ref, w_ref, b_ref, o_ref):
    cv = c_ref[...]
    ca = cv * jax.nn.sigmoid(cv)
    o_ref[0] = jnp.dot(ca, w_ref[0], preferred_element_type=F32,
                       precision=lax.Precision.HIGHEST) + b_ref[0]


def _modulation(c, w_mod, b_mod):
    depth, d, n = w_mod.shape
    b = c.shape[0]
    rows = 8
    cp = jnp.zeros((rows, d), F32).at[:b].set(c)
    tn = 1024
    out = pl.pallas_call(
        _mod_kernel,
        grid=(depth, n // tn),
        in_specs=[
            pl.BlockSpec((rows, d), lambda l, j: (0, 0)),
            pl.BlockSpec((1, d, tn), lambda l, j: (l, 0, j)),
            pl.BlockSpec((1, 1, tn), lambda l, j: (l, 0, j)),
        ],
        out_specs=pl.BlockSpec((1, rows, tn), lambda l, j: (l, 0, j)),
        out_shape=jax.ShapeDtypeStruct((depth, rows, n), F32),
        compiler_params=_params("parallel", "parallel"),
        name="modulation",
    )(cp, w_mod, b_mod.reshape(depth, 1, n))
    return out[:, :b].reshape(depth, b, N_MOD, d)


ROW_SPLITS = (A_KW, SB_W, SB_W, SB_W, SWA_Q_W, SWA_KV_W, SWA_KV_W)


def _in_proj_kernel(x_ref, g_ref, mod_ref, w_ref, wt_ref, kaug_ref, taug_ref, *out_refs):
    row_refs, (qt_ref, vt_ref, kn_ref) = out_refs[:len(ROW_SPLITS)], out_refs[len(ROW_SPLITS):]
    x = x_ref[0]
    h = _rms(x, g_ref[...]) * (1.0 + mod_ref[0, 1:2, :]) + mod_ref[0, 0:1, :]
    hb = h.astype(BF16)
    proj = _dot(hb, w_ref[...])
    start = 0
    for n, (ref, width) in enumerate(zip(row_refs, ROW_SPLITS)):
        part = proj[:, start:start + width]
        if n == 0:
            part = part + kaug_ref[...]
            keys = part.astype(BF16)
        ref[0] = part.astype(BF16)
        start += width
    feat = lax.broadcasted_iota(jnp.int32, (x.shape[0], LANES), 1)
    out_lane = lax.broadcasted_iota(jnp.int32, kn_ref.shape[2:], 1)
    norms = jnp.zeros(kn_ref.shape[2:], F32)
    for hd in range(A_HEADS):
        kf = keys[:, hd * LANES:(hd + 1) * LANES].astype(F32)
        sq = jnp.sum(jnp.where(feat < A_AUG, kf * kf, 0.0), axis=1, keepdims=True)
        norms = jnp.where(out_lane == hd, jnp.sqrt(jnp.max(sq, axis=0, keepdims=True)), norms)
    kn_ref[0, 0] = norms
    proj_t = _nt_dot(wt_ref[...], hb) + taug_ref[...]
    for hd in range(A_HEADS):
        qt_ref[0, hd, 0] = proj_t[hd * LANES:(hd + 1) * LANES].astype(BF16)
        v0 = A_KW + hd * A_VT
        vt_ref[0, hd, 0] = proj_t[v0:v0 + A_VT].astype(BF16)


def _bf16_pieces(value, n):
    pieces = []
    rest = float(value)
    for _ in range(n):
        p = float(np.asarray(rest, np.float32).astype(jnp.bfloat16).astype(np.float64))
        pieces.append(p)
        rest -= p
    return pieces


def _slopes_log2(hd):
    pieces = _bf16_pieces(_alibi_slopes(A_HEADS)[hd] * LOG2E, A_PIECES)
    return sum(pieces), pieces


def _alibi_constants(t):
    idx = np.arange(t)
    lo, hi = idx % 256, idx - idx % 256
    kaug = np.zeros((t, A_KW), np.float32)
    taug = np.zeros((A_TW, t), np.float32)
    for hd in range(A_HEADS):
        _, pieces = _slopes_log2(hd)
        k0 = hd * LANES + A_AUG
        for n, piece in enumerate(pieces):
            taug[k0 + n] = -lo
            taug[k0 + A_PIECES + n] = -hi
            kaug[:, k0 + n] = piece
            kaug[:, k0 + A_PIECES + n] = piece
            taug[k0 + 2 * A_PIECES + n] = piece
            taug[k0 + 3 * A_PIECES + n] = piece
            kaug[:, k0 + 2 * A_PIECES + n] = lo
            kaug[:, k0 + 3 * A_PIECES + n] = hi
        r0 = hd * LANES + A_REF
        kaug[:, r0:r0 + 3] = 1.0
        taug[A_KW + hd * A_VT + A_V] = 1.0
    return jnp.asarray(kaug), jnp.asarray(taug)


def _in_proj(x, g, mod, w, wt):
    b, s, d = x.shape
    tm = A_TILE
    nb = s // tm
    n = w.shape[1]
    kaug, taug = _alibi_constants(tm)
    const = lambda shape: pl.BlockSpec(shape, lambda bi, i: (0,) * len(shape))
    return pl.pallas_call(
        _in_proj_kernel,
        grid=(b, nb),
        in_specs=[
            pl.BlockSpec((1, tm, d), lambda bi, i: (bi, i, 0)),
            const((1, d)),
            pl.BlockSpec((1, N_MOD, d), lambda bi, i: (bi, 0, 0)),
            const((d, n)), const((A_TW, d)), const((tm, A_KW)), const((A_TW, tm)),
        ],
        out_specs=[pl.BlockSpec((1, tm, wd), lambda bi, i: (bi, i, 0)) for wd in ROW_SPLITS] + [
            pl.BlockSpec((1, A_HEADS, 1, LANES, tm), lambda bi, i: (bi, 0, i, 0, 0)),
            pl.BlockSpec((1, A_HEADS, 1, A_VT, tm), lambda bi, i: (bi, 0, i, 0, 0)),
            pl.BlockSpec((1, 1, 8, LANES), lambda bi, i: (bi, i, 0, 0)),
        ],
        out_shape=[jax.ShapeDtypeStruct((b, s, wd), BF16) for wd in ROW_SPLITS] + [
            jax.ShapeDtypeStruct((b, A_HEADS, nb, LANES, tm), BF16),
            jax.ShapeDtypeStruct((b, A_HEADS, nb, A_VT, tm), BF16),
            jax.ShapeDtypeStruct((b, nb, 8, LANES), F32),
        ],
        compiler_params=_params("parallel", "parallel"),
        name="in_proj",
    )(x, g.reshape(1, d), mod, w, wt, kaug, taug)


def _diff_attn_kernel(kn_ref, slopes_ref, lq1_ref, lk1_ref, lq2_ref, lk2_ref, qt_ref, k_ref,
                      vt_ref, o_ref, q_s, m_s, acc_s, *, t, nb, lam_init):
    bi = pl.program_id(0)
    hd = pl.program_id(1)
    i = pl.program_id(2)
    slope = slopes_ref[hd]
    qt = qt_ref[0, 0, 0]
    feat = lax.broadcasted_iota(jnp.int32, (LANES, t), 0)
    zero = jnp.zeros_like(qt)
    for slot in range(max(A_GROUPS)):
        q_s[slot, 0] = jnp.where(jnp.logical_or(feat < A_QK, feat >= A_AUG), qt, zero)
        q_s[slot, 1] = jnp.where(feat >= A_QK, qt, zero)
    qsq = qt.astype(F32) * qt.astype(F32)
    n0 = jnp.sum(jnp.where(feat < A_QK, qsq, 0.0), axis=0, keepdims=True)
    n1 = jnp.sum(jnp.where(jnp.logical_and(feat >= A_QK, feat < A_AUG), qsq, 0.0),
                 axis=0, keepdims=True)
    q_norm = jnp.max(jnp.sqrt(jnp.maximum(n0, n1)))
    causal = (lax.broadcasted_iota(jnp.int32, (t, t), 0)
              <= lax.broadcasted_iota(jnp.int32, (t, t), 1))

    def scores(j, a, slot):
        kj = k_ref[0, pl.ds(pl.multiple_of(j * t, t), t), :]
        return _dot(kj, q_s[slot, a])

    def exact_block(j, first):
        vtj = vt_ref[0, 0, j]
        for a in range(2):
            s = scores(j, a, 0)
            if first:
                s = jnp.where(causal, s, NEG)
            top = jnp.max(s, axis=0, keepdims=True)
            shift = top if first else jnp.maximum(top, 0.0)
            pv = _dot(vtj, jnp.exp2(s - shift).astype(BF16))
            if first:
                acc_s[a] = pv
                m_s[a] = shift
            else:
                acc_s[a] = jnp.exp2(-shift) * acc_s[a] + pv
                m_s[a] = m_s[a] + shift

    def fast_weights(j, a, slot):
        return jnp.exp2(scores(j, a, slot)).astype(BF16)

    def fast_block(j):
        vtj = vt_ref[0, 0, j]
        for a in range(2):
            acc_s[a] = acc_s[a] + _dot(vtj, fast_weights(j, a, 0))

    def fast_group(j, size):
        for u in range(size):
            set_reference(j - u, u)
        vts = jnp.concatenate([vt_ref[0, 0, j - u] for u in range(size)], axis=1)
        for a in range(2):
            ps = jnp.concatenate([fast_weights(j - u, a, u) for u in range(size)], axis=0)
            acc_s[a] = acc_s[a] + _dot(vts, ps)

    def set_reference(j, slot):
        off = slope * ((i - j) * t).astype(F32)
        row = lax.broadcasted_iota(jnp.int32, (16, t), 0)
        for a in range(2):
            hi, mid, lo = [p.astype(F32) for p in _split_bf16(-(m_s[a] + off), 3)]
            tile = jnp.where(row == 0, hi, jnp.where(row == 1, mid, jnp.where(row == 2, lo, 0.0)))
            q_s[slot, a, A_REF:A_REF + 16, :] = tile.astype(BF16)

    def plain_diagonal():
        vtj = vt_ref[0, 0, i]
        for a in range(2):
            s = jnp.where(causal, scores(i, a, 0), NEG)
            acc_s[a] = _dot(vtj, jnp.exp2(s).astype(BF16))
            m_s[a] = jnp.zeros((1, t), F32)

    diag_reach = q_norm * kn_ref[(bi * A_HEADS + hd) * nb + i] + BOUND_MARGIN
    lax.cond(diag_reach <= ZERO_REF_LOG2, plain_diagonal, lambda: exact_block(i, True))

    def reach_of(j, m_low):
        k_norm = kn_ref[(bi * A_HEADS + hd) * nb + j]
        return (q_norm * k_norm + slope * (t - (i - j) * t).astype(F32) + BOUND_MARGIN) - m_low

    def single(j, m_low):
        reach = reach_of(j, m_low)

        def visit():
            set_reference(j, 0)

            def fast():
                fast_block(j)
                return m_low

            def exact():
                exact_block(j, False)
                return jnp.min(m_s[...])

            return lax.cond(reach <= FAST_LOG2, fast, exact)

        return lax.cond(reach < SKIP_LOG2, lambda: m_low, visit)

    def sweep(size, count, top, m_low):
        def body(n, m_low):
            j = top - size * n
            all_fast = jnp.bool_(True)
            for u in range(size):
                reach = reach_of(j - u, m_low)
                all_fast = jnp.logical_and(
                    all_fast, jnp.logical_and(reach >= SKIP_LOG2, reach <= FAST_LOG2))

            def group():
                fast_group(j, size)
                return m_low

            def one_by_one():
                return lax.fori_loop(0, size, lambda u, m: single(j - u, m), m_low)

            return lax.cond(all_fast, group, one_by_one)

        return lax.fori_loop(0, count, body, m_low)

    m_low = jnp.min(m_s[...])
    left = i
    for size in A_GROUPS:
        m_low = sweep(size, left // size, left - 1, m_low)
        left = left % size
    lax.fori_loop(0, left, lambda u, m: single(left - 1 - u, m), m_low)

    lam = (jnp.exp(jnp.sum(lq1_ref[...] * lk1_ref[...], keepdims=True))
           - jnp.exp(jnp.sum(lq2_ref[...] * lk2_ref[...], keepdims=True)) + lam_init)
    outs = [acc_s[a, :A_V, :] / acc_s[a, A_V:A_V + 1, :] for a in range(2)]
    y_t = outs[0] - lam * outs[1]
    y_t = jnp.concatenate([y_t, jnp.zeros((LANES - A_V, t), F32)], axis=0)
    o_ref[0] = y_t.T


def _diff_attn(qt, k, vt, k_norms, lq1, lk1, lq2, lk2, lam_init):
    b, s, _ = k.shape
    t = A_TILE
    nb = s // t
    slopes = jnp.asarray([_slopes_log2(hd)[0] for hd in range(A_HEADS)], F32)
    smem = pl.BlockSpec(memory_space=pltpu.SMEM)
    vec = pl.BlockSpec((1, A_QK), lambda bi, h, i: (0, 0))
    return pl.pallas_call(
        functools.partial(_diff_attn_kernel, t=t, nb=nb, lam_init=lam_init),
        grid=(b, A_HEADS, nb),
        in_specs=[
            smem, smem, vec, vec, vec, vec,
            pl.BlockSpec((1, 1, 1, LANES, t), lambda bi, h, i: (bi, h, i, 0, 0)),
            pl.BlockSpec((1, s, LANES), lambda bi, h, i: (bi, 0, h)),
            pl.BlockSpec((1, 1, nb, A_VT, t), lambda bi, h, i: (bi, h, 0, 0, 0)),
        ],
        out_specs=pl.BlockSpec((1, t, LANES), lambda bi, h, i: (bi, i, h)),
        out_shape=jax.ShapeDtypeStruct((b, s, A_KW), F32),
        scratch_shapes=[
            pltpu.VMEM((max(A_GROUPS), 2, LANES, t), BF16),
            pltpu.VMEM((2, 1, t), F32),
            pltpu.VMEM((2, A_VT, t), F32),
        ],
        compiler_params=_params("parallel", "parallel", "arbitrary"),
        name="diff_attn",
    )(k_norms, slopes, lq1.reshape(1, A_QK), lk1.reshape(1, A_QK), lq2.reshape(1, A_QK),
      lk2.reshape(1, A_QK), qt, k, vt)


def _sb_attn_kernel(q_ref, k_ref, v_ref, o_ref, r_s, acc_s, *, tq):
    i = pl.program_id(2)
    q = q_ref[0]
    lane = lax.broadcasted_iota(jnp.int32, (tq, LANES), 1)
    qm = [jnp.where(lane < SB_DIM, q, jnp.zeros_like(q)),
          jnp.where(lane >= SB_DIM, q, jnp.zeros_like(q))]
    row = lax.broadcasted_iota(jnp.int32, (tq, tq), 0)
    col = lax.broadcasted_iota(jnp.int32, (tq, tq), 1)
    strict = col < row
    later = (row > col).astype(BF16)

    def keys(j):
        return k_ref[0, pl.ds(pl.multiple_of(j * tq, tq), tq), :]

    def values(j):
        return v_ref[0, pl.ds(pl.multiple_of(j * tq, tq), tq), :]

    def log_weights(kj, h, diagonal):
        z = _nt_dot(qm[h], kj)
        log_beta = jnp.minimum(z, 0.0) - jnp.log2(1.0 + jnp.exp2(-jnp.abs(z)))
        log_1mb = log_beta - z
        if diagonal:
            log_1mb = jnp.where(strict, log_1mb, 0.0)
        after = _dot(jnp.concatenate(_split_bf16(log_1mb, 2), axis=1), later2)
        return log_beta + after, jnp.sum(log_1mb, axis=1, keepdims=True)

    later2 = jnp.concatenate([later, later], axis=0)
    prev = jnp.maximum(i - 1, 0)
    has_prev = (i > 0).astype(F32)
    k_diag, k_prev = keys(i), keys(prev)
    v_both = jnp.concatenate([values(i), values(prev)], axis=0)
    for h in range(2):
        lw_d, tot_d = log_weights(k_diag, h, True)
        lw_p, tot_p = log_weights(k_prev, h, False)
        a_d = jnp.where(strict, jnp.exp2(lw_d), 0.0)
        a_p = jnp.exp2(lw_p + tot_d) * has_prev
        acc_s[h] = _dot(jnp.concatenate([a_d.astype(BF16), a_p.astype(BF16)], axis=1), v_both)
        r_s[h] = tot_d + tot_p * has_prev

    def cond(carry):
        j, live = carry
        return jnp.logical_and(j >= 0, live)

    def body(carry):
        j, _ = carry
        kj, vj = keys(j), values(j)
        for h in range(2):
            lw, tot = log_weights(kj, h, False)
            run = r_s[h]
            acc_s[h] = acc_s[h] + _dot(jnp.exp2(lw + run).astype(BF16), vj)
            r_s[h] = run + tot
        return j - 1, jnp.max(r_s[...]) > SKIP_LOG2

    lax.while_loop(cond, body, (i - 2, jnp.max(r_s[...]) > SKIP_LOG2))
    o_ref[0] = jnp.where(lane < SB_DIM, acc_s[0], acc_s[1])


def _sb_attn(q, k, v):
    b, s, _ = q.shape
    tq = min(ATT_TILE, s)
    pairs = SB_W // LANES
    return pl.pallas_call(
        functools.partial(_sb_attn_kernel, tq=tq),
        grid=(b, pairs, s // tq),
        in_specs=[
            pl.BlockSpec((1, tq, LANES), lambda bi, p, i: (bi, i, p)),
            pl.BlockSpec((1, s, LANES), lambda bi, p, i: (bi, 0, p)),
            pl.BlockSpec((1, s, LANES), lambda bi, p, i: (bi, 0, p)),
        ],
        out_specs=pl.BlockSpec((1, tq, LANES), lambda bi, p, i: (bi, i, p)),
        out_shape=jax.ShapeDtypeStruct((b, s, SB_W), F32),
        scratch_shapes=[
            pltpu.VMEM((2, tq, 1), F32),
            pltpu.VMEM((2, tq, LANES), F32),
        ],
        compiler_params=_params("parallel", "parallel", "arbitrary"),
        name="sb_attn",
    )(q, k, v)


def _swa_attn_kernel(sinks_ref, bias_ref, q_ref, kc_ref, kp_ref, vc_ref, vp_ref, o_ref, *, tq):
    i = pl.program_id(1)
    half = tq // 2
    col = lax.broadcasted_iota(jnp.int32, (half, tq), 1)
    has_key = col >= jnp.where(i > 0, 0, half)
    lane = lax.broadcasted_iota(jnp.int32, (half, LANES), 1)

    kwin = [jnp.concatenate([kp_ref[0, half:, :], kc_ref[0, :half, :]], axis=0), kc_ref[0]]
    vwin = [jnp.concatenate([vp_ref[0, half:, :], vc_ref[0, :half, :]], axis=0), vc_ref[0]]
    for hf in range(2):
        rows = slice(hf * half, (hf + 1) * half)
        for t in range(SWA_REP):
            q = q_ref[0, rows, t * LANES:(t + 1) * LANES]
            outs = []
            for g in range(SWA_KV_HEADS):
                head = g * SWA_REP + t
                qg = jnp.where((lane // SWA_DIM) == g, q, jnp.zeros_like(q))
                s = _nt_dot(qg, kwin[hf]) + bias_ref[head]
                if hf == 0:
                    s = jnp.where(has_key, s, NEG)
                sink = sinks_ref[head] * LOG2E
                m = jnp.maximum(jnp.max(s, axis=1, keepdims=True), sink)
                p = jnp.exp2(s - m)
                den = jnp.sum(p, axis=1, keepdims=True) + jnp.exp2(sink - m)
                outs.append(_dot(p.astype(BF16), vwin[hf]) / den)
            o_ref[0, rows, t * LANES:(t + 1) * LANES] = jnp.where(lane < SWA_DIM, outs[0], outs[1])


def _swa_bias(tq):
    half = tq // 2
    dist = np.arange(half)[:, None] + half - np.arange(tq)[None, :]
    in_window = (dist >= 0) & (dist < WINDOW)
    slopes = _alibi_slopes(SWA_Q_HEADS) * LOG2E
    bias = np.where(in_window[None], -slopes[:, None, None] * dist[None], NEG)
    return jnp.asarray(bias, F32)


def _swa_attn(q, k, v, sinks):
    b, s, _ = q.shape
    tq = min(ATT_TILE, s)
    smem = pl.BlockSpec(memory_space=pltpu.SMEM)
    cur = pl.BlockSpec((1, tq, SWA_KV_W), lambda bi, i: (bi, i, 0))
    prev = pl.BlockSpec((1, tq, SWA_KV_W), lambda bi, i: (bi, jnp.maximum(i - 1, 0), 0))
    return pl.pallas_call(
        functools.partial(_swa_attn_kernel, tq=tq),
        grid=(b, s // tq),
        in_specs=[
            smem,
            pl.BlockSpec((SWA_Q_HEADS, tq // 2, tq), lambda bi, i: (0, 0, 0)),
            pl.BlockSpec((1, tq, SWA_Q_W), lambda bi, i: (bi, i, 0)),
            cur, prev, cur, prev,
        ],
        out_specs=pl.BlockSpec((1, tq, SWA_Q_W), lambda bi, i: (bi, i, 0)),
        out_shape=jax.ShapeDtypeStruct((b, s, SWA_Q_W), F32),
        compiler_params=_params("parallel", "parallel"),
        name="swa_attn",
    )(sinks.astype(F32), _swa_bias(tq), q, k, k, v, v)


def _mix_ffn_kernel(ya_ref, yb_ref, yc_ref, x_ref, ga_ref, gb_ref, gc_ref, ln2_ref, mod_ref,
                    wa_ref, wb_ref, wc_ref, wg_ref, wu_ref, wd_ref, fg_ref, o_ref,
                    *, a_scale, final):
    mixed = _dot(_rms(yb_ref[0], gb_ref[...]).astype(BF16), wb_ref[...])
    mixed += _dot(_rms(yc_ref[0], gc_ref[...]).astype(BF16), wc_ref[...])
    normed = []
    for h in range(A_HEADS):
        cols = slice(h * LANES, (h + 1) * LANES)
        ya = ya_ref[0, :, cols]
        ms = jnp.sum(ya * ya, axis=-1, keepdims=True) * (1.0 / A_V)
        normed.append(ya * lax.rsqrt(ms + EPS) * ga_ref[:, cols] * a_scale)
    packed = [normed[h] + pltpu.roll(normed[h + 1], A_V, 1) for h in range(0, A_HEADS, 2)]
    mixed += _dot(jnp.concatenate(packed, axis=1).astype(BF16), wa_ref[...])
    x1 = x_ref[0] + mod_ref[0, 2:3, :] * mixed
    h2 = (_rms(x1, ln2_ref[...]) * (1.0 + mod_ref[0, 4:5, :]) + mod_ref[0, 3:4, :]).astype(BF16)
    gate = _dot(h2, wg_ref[...])
    up = _dot(h2, wu_ref[...])
    act = gate * jax.nn.sigmoid(gate) * up
    out = x1 + mod_ref[0, 5:6, :] * _dot(act.astype(BF16), wd_ref[...])
    if final:
        out = _rms(out, fg_ref[...])
    o_ref[0] = out


def _mix_ffn(ya, yb, yc, x, ga, gb, gc, ln2, mod, wa, wb, wc, wg, wu, wd, fg, a_scale, final):
    b, s, d = x.shape
    f = wg.shape[1]
    tm = min(FFN_TILE, s)
    row = lambda wd_: pl.BlockSpec((1, tm, wd_), lambda bi, i: (bi, i, 0))
    const = lambda r, cdim: pl.BlockSpec((r, cdim), lambda bi, i: (0, 0),
                                         pipeline_mode=pl.Buffered(1))
    return pl.pallas_call(
        functools.partial(_mix_ffn_kernel, a_scale=a_scale, final=final),
        grid=(b, s // tm),
        in_specs=[
            row(A_KW), row(SB_W), row(SWA_Q_W), row(d),
            const(1, A_KW), const(1, SB_W), const(1, SWA_Q_W), const(1, d),
            pl.BlockSpec((1, N_MOD, d), lambda bi, i: (bi, 0, 0)),
            const(A_W, d), const(SB_W, d), const(SWA_Q_W, d),
            const(d, f), const(d, f), const(f, d), const(1, d),
        ],
        out_specs=row(d),
        out_shape=jax.ShapeDtypeStruct((b, s, d), F32),
        compiler_params=_params("parallel", "parallel"),
        name="mix_ffn",
    )(ya, yb, yc, x, ga, gb, gc, ln2.reshape(1, d), mod, wa, wb, wc, wg, wu, wd, fg.reshape(1, d))


def _swa_perm():
    cols = []
    for t in range(SWA_REP):
        for g in range(SWA_KV_HEADS):
            head = g * SWA_REP + t
            cols.extend(range(head * SWA_DIM, (head + 1) * SWA_DIM))
    return jnp.asarray(cols, jnp.int32)


def _pad_heads(w, per_head, width):
    d = w.shape[0]
    w = w.reshape(d, A_HEADS, per_head)
    return jnp.pad(w, ((0, 0), (0, 0), (0, width - per_head))).reshape(d, A_HEADS * width)


def _prep_w_in(w):
    bounds = np.cumsum((0, A_W, A_W, A_W, SB_W, SB_W, SB_W, SWA_Q_W, SWA_KV_W, SWA_KV_W))
    qa, ka, va, qb, kb, vb, qc, kc, vc = [w[:, bounds[n]:bounds[n + 1]] for n in range(9)]
    row = jnp.concatenate([
        _pad_heads(ka, 2 * A_QK, LANES), qb * (SB_DIM ** -0.5 * LOG2E), kb, vb,
        (qc * (SWA_DIM ** -0.5 * LOG2E))[:, _swa_perm()], kc, vc], axis=1)
    transposed = jnp.concatenate([
        _pad_heads(qa * (A_QK ** -0.5 * LOG2E), 2 * A_QK, LANES), _pad_heads(va, A_V, A_VT)], axis=1).T
    return row.astype(BF16), transposed.astype(BF16)


def kernel(x, c, ln1_g, ln2_g, w_mod, b_mod, w_in, lam_q1, lam_k1, lam_q2, lam_k2, diff_norm_g,
           sb_norm_g, swa_norm_g, swa_sinks, w_out, w_gate, w_up, w_down, final_g):
    depth = w_in.shape[0]
    perm = _swa_perm()
    mod = _modulation(c, w_mod, b_mod)
    for l in range(depth):
        lam_init = 0.8 - 0.6 * math.exp(-0.3 * l)
        w_row, w_t = _prep_w_in(w_in[l])
        ka, qb, kb, vb, qc, kc, vc, qta, vta, kn = _in_proj(x, ln1_g[l], mod[l], w_row, w_t)
        k_norms = kn[:, :, 0, :A_HEADS].transpose(0, 2, 1).reshape(-1)
        ya = _diff_attn(qta, ka, vta, k_norms, lam_q1[l], lam_k1[l], lam_q2[l], lam_k2[l], lam_init)
        yb = _sb_attn(qb, kb, vb)
        yc = _swa_attn(qc, kc, vc, swa_sinks[l])
        wo = w_out[l].astype(BF16)
        x = _mix_ffn(
            ya, yb, yc, x,
            _pad_heads(diff_norm_g[l].reshape(1, A_V).repeat(A_HEADS, 0).reshape(1, A_W), A_V, LANES),
            sb_norm_g[l].reshape(1, SB_W),
            swa_norm_g[l][perm].reshape(1, SWA_Q_W),
            ln2_g[l], mod[l],
            wo[:A_W], wo[A_W:A_W + SB_W], wo[A_W + SB_W:][perm],
            w_gate[l].astype(BF16), w_up[l].astype(BF16), w_down[l].astype(BF16), final_g,
            1.0 - lam_init, l == depth - 1)
    return x
```

```python
import functools
import math

import numpy as np
import jax
import jax.numpy as jnp
from jax import lax
from jax.experimental import pallas as pl
from jax.experimental.pallas import tpu as pltpu

F32 = jnp.float32
BF16 = jnp.bfloat16

N_MOD = 6
EPS = 1e-6
A_HEADS = 4
A_QK = 32
A_V = 64
SB_HEADS = 4
SB_DIM = 64
SWA_Q_HEADS = 8
SWA_KV_HEADS = 2
SWA_REP = 4
SWA_DIM = 64
WINDOW = 128
A_W = 256
SB_W = 256
SWA_Q_W = 512
SWA_KV_W = 128

LANES = 128
NEG = -1e30

VMEM_LIMIT = 56 * 1024 * 1024

ROW_TILE = 512
FFN_TILE = 256
ATT_TILE = 256
A_TILE = ROW_TILE

LOG2E = math.log2(math.e)
A_AUG = 2 * A_QK
A_REF = A_AUG + 16
A_PIECES = 4
A_VT = 80
A_KW = A_HEADS * LANES
A_TW = A_HEADS * (LANES + A_VT)
SKIP_LOG2 = -160.0
FAST_LOG2 = 80.0
A_GROUPS = (8, 4, 2)
ZERO_REF_LOG2 = 60.0
BOUND_MARGIN = 1.0


def _params(*sem):
    return pltpu.CompilerParams(dimension_semantics=sem, vmem_limit_bytes=VMEM_LIMIT)


def _nt_dot(a, b):
    return lax.dot_general(a, b, (((1,), (1,)), ((), ())), preferred_element_type=F32)


def _dot(a, b):
    return jnp.dot(a, b, preferred_element_type=F32)


def _rms(x, g):
    ms = jnp.mean(x * x, axis=-1, keepdims=True)
    return x * lax.rsqrt(ms + EPS) * g


def _alibi_slopes(n):
    return 2.0 ** (-8.0 * np.arange(1, n + 1, dtype=np.float64) / n)


def _split_bf16(x, n):
    pieces = []
    for _ in range(n - 1):
        p = x.astype(BF16)
        pieces.append(p)
        x = x - p.astype(F32)
    pieces.append(x.astype(BF16))
    return pieces


def _mod_kernel(c_ref, w_ref, b_ref, o_ref):
    cv = c_ref[...]
    ca = cv * jax.nn.sigmoid(cv)
    o_ref[0] = jnp.dot(ca, w_ref[0], preferred_element_type=F32,
                       precision=lax.Precision.HIGHEST) + b_ref[0]


def _modulation(c, w_mod, b_mod):
    depth, d, n = w_mod.shape
    b = c.shape[0]
    rows = 8
    cp = jnp.zeros((rows, d), F32).at[:b].set(c)
    tn = 1024
    out = pl.pallas_call(
        _mod_kernel,
        grid=(depth, n // tn),
        in_specs=[
            pl.BlockSpec((rows, d), lambda l, j: (0, 0)),
            pl.BlockSpec((1, d, tn), lambda l, j: (l, 0, j)),
            pl.BlockSpec((1, 1, tn), lambda l, j: (l, 0, j)),
        ],
        out_specs=pl.BlockSpec((1, rows, tn), lambda l, j: (l, 0, j)),
        out_shape=jax.ShapeDtypeStruct((depth, rows, n), F32),
        compiler_params=_params("parallel", "parallel"),
        name="modulation",
    )(cp, w_mod, b_mod.reshape(depth, 1, n))
    return out[:, :b].reshape(depth, b, N_MOD, d)


ROW_SPLITS = (A_KW, SB_W, SB_W, SB_W, SWA_Q_W, SWA_KV_W, SWA_KV_W)


def _in_proj_kernel(x_ref, g_ref, mod_ref, w_ref, wt_ref, kaug_ref, taug_ref, *out_refs):
    row_refs, (qt_ref, vt_ref, kn_ref) = out_refs[:len(ROW_SPLITS)], out_refs[len(ROW_SPLITS):]
    x = x_ref[0]
    h = _rms(x, g_ref[...]) * (1.0 + mod_ref[0, 1:2, :]) + mod_ref[0, 0:1, :]
    hb = h.astype(BF16)
    proj = _dot(hb, w_ref[...])
    start = A_W
    for ref, width in zip(row_refs[1:], ROW_SPLITS[1:]):
        ref[0] = proj[:, start:start + width].astype(BF16)
        start += width
    feat = lax.broadcasted_iota(jnp.int32, (x.shape[0], LANES), 1)
    out_lane = lax.broadcasted_iota(jnp.int32, kn_ref.shape[2:], 1)
    norms = jnp.zeros(kn_ref.shape[2:], F32)
    for hd in range(A_HEADS):
        pair = proj[:, (hd // 2) * LANES:(hd // 2 + 1) * LANES]
        if hd % 2:
            pair = pltpu.roll(pair, A_AUG, 1)
        cols = slice(hd * LANES, (hd + 1) * LANES)
        keys = (jnp.where(feat < A_AUG, pair, 0.0) + kaug_ref[:, cols]).astype(BF16)
        row_refs[0][0, :, cols] = keys
        kf = keys.astype(F32)
        sq = jnp.sum(jnp.where(feat < A_AUG, kf * kf, 0.0), axis=1, keepdims=True)
        norms = jnp.where(out_lane == hd, jnp.sqrt(jnp.max(sq, axis=0, keepdims=True)), norms)
    kn_ref[0, 0] = norms
    proj_t = _nt_dot(wt_ref[...], hb)
    for hd in range(A_HEADS):
        q_rows = proj_t[hd * A_AUG:(hd + 1) * A_AUG]
        q_const = taug_ref[hd * LANES + A_AUG:(hd + 1) * LANES, :]
        qt_ref[0, hd, 0] = jnp.concatenate([q_rows, q_const], axis=0).astype(BF16)
        v_rows = proj_t[A_W + hd * A_V:A_W + (hd + 1) * A_V]
        v_const = taug_ref[A_KW + hd * A_VT + A_V:A_KW + (hd + 1) * A_VT, :]
        vt_ref[0, hd, 0] = jnp.concatenate([v_rows, v_const], axis=0).astype(BF16)


def _bf16_pieces(value, n):
    pieces = []
    rest = float(value)
    for _ in range(n):
        p = float(np.asarray(rest, np.float32).astype(jnp.bfloat16).astype(np.float64))
        pieces.append(p)
        rest -= p
    return pieces


def _slopes_log2(hd):
    pieces = _bf16_pieces(_alibi_slopes(A_HEADS)[hd] * LOG2E, A_PIECES)
    return sum(pieces), pieces


def _alibi_constants(t):
    idx = np.arange(t)
    lo, hi = idx % 256, idx - idx % 256
    kaug = np.zeros((t, A_KW), np.float32)
    taug = np.zeros((A_TW, t), np.float32)
    for hd in range(A_HEADS):
        _, pieces = _slopes_log2(hd)
        k0 = hd * LANES + A_AUG
        for n, piece in enumerate(pieces):
            taug[k0 + n] = -lo
            taug[k0 + A_PIECES + n] = -hi
            kaug[:, k0 + n] = piece
            kaug[:, k0 + A_PIECES + n] = piece
            taug[k0 + 2 * A_PIECES + n] = piece
            taug[k0 + 3 * A_PIECES + n] = piece
            kaug[:, k0 + 2 * A_PIECES + n] = lo
            kaug[:, k0 + 3 * A_PIECES + n] = hi
        r0 = hd * LANES + A_REF
        kaug[:, r0:r0 + 3] = 1.0
        taug[A_KW + hd * A_VT + A_V] = 1.0
    return jnp.asarray(kaug), jnp.asarray(taug)


def _in_proj(x, g, mod, w, wt):
    b, s, d = x.shape
    tm = A_TILE
    nb = s // tm
    n = w.shape[1]
    kaug, taug = _alibi_constants(tm)
    const = lambda shape: pl.BlockSpec(shape, lambda bi, i: (0,) * len(shape))
    return pl.pallas_call(
        _in_proj_kernel,
        grid=(b, nb),
        in_specs=[
            pl.BlockSpec((1, tm, d), lambda bi, i: (bi, i, 0)),
            const((1, d)),
            pl.BlockSpec((1, N_MOD, d), lambda bi, i: (bi, 0, 0)),
            const((d, n)), const(wt.shape), const((tm, A_KW)), const((A_TW, tm)),
        ],
        out_specs=[pl.BlockSpec((1, tm, wd), lambda bi, i: (bi, i, 0)) for wd in ROW_SPLITS] + [
            pl.BlockSpec((1, A_HEADS, 1, LANES, tm), lambda bi, i: (bi, 0, i, 0, 0)),
            pl.BlockSpec((1, A_HEADS, 1, A_VT, tm), lambda bi, i: (bi, 0, i, 0, 0)),
            pl.BlockSpec((1, 1, 8, LANES), lambda bi, i: (bi, i, 0, 0)),
        ],
        out_shape=[jax.ShapeDtypeStruct((b, s, wd), BF16) for wd in ROW_SPLITS] + [
            jax.ShapeDtypeStruct((b, A_HEADS, nb, LANES, tm), BF16),
            jax.ShapeDtypeStruct((b, A_HEADS, nb, A_VT, tm), BF16),
            jax.ShapeDtypeStruct((b, nb, 8, LANES), F32),
        ],
        compiler_params=_params("parallel", "parallel"),
        name="in_proj",
    )(x, g.reshape(1, d), mod, w, wt, kaug, taug)


def _diff_attn_kernel(kn_ref, slopes_ref, lq1_ref, lk1_ref, lq2_ref, lk2_ref, qt_ref, k_ref,
                      vt_ref, o_ref, q_s, m_s, acc_s, *, t, nb, lam_init):
    bi = pl.program_id(0)
    hd = pl.program_id(1)
    i = pl.program_id(2)
    slope = slopes_ref[hd]
    qt = qt_ref[0, 0, 0]
    feat = lax.broadcasted_iota(jnp.int32, (LANES, t), 0)
    zero = jnp.zeros_like(qt)
    for slot in range(max(A_GROUPS)):
        q_s[slot, 0] = jnp.where(jnp.logical_or(feat < A_QK, feat >= A_AUG), qt, zero)
        q_s[slot, 1] = jnp.where(feat >= A_QK, qt, zero)
    qsq = qt.astype(F32) * qt.astype(F32)
    n0 = jnp.sum(jnp.where(feat < A_QK, qsq, 0.0), axis=0, keepdims=True)
    n1 = jnp.sum(jnp.where(jnp.logical_and(feat >= A_QK, feat < A_AUG), qsq, 0.0),
                 axis=0, keepdims=True)
    q_norm = jnp.max(jnp.sqrt(jnp.maximum(n0, n1)))
    causal = (lax.broadcasted_iota(jnp.int32, (t, t), 0)
              <= lax.broadcasted_iota(jnp.int32, (t, t), 1))

    def scores(j, a, slot):
        kj = k_ref[0, pl.ds(pl.multiple_of(j * t, t), t), :]
        return _dot(kj, q_s[slot, a])

    def exact_block(j, first):
        vtj = vt_ref[0, 0, j]
        for a in range(2):
            s = scores(j, a, 0)
            if first:
                s = jnp.where(causal, s, NEG)
            top = jnp.max(s, axis=0, keepdims=True)
            shift = top if first else jnp.maximum(top, 0.0)
            pv = _dot(vtj, jnp.exp2(s - shift).astype(BF16))
            if first:
                acc_s[a] = pv
                m_s[a] = shift
            else:
                acc_s[a] = jnp.exp2(-shift) * acc_s[a] + pv
                m_s[a] = m_s[a] + shift

    def fast_weights(j, a, slot):
        return jnp.exp2(scores(j, a, slot)).astype(BF16)

    def fast_block(j):
        vtj = vt_ref[0, 0, j]
        for a in range(2):
            acc_s[a] = acc_s[a] + _dot(vtj, fast_weights(j, a, 0))

    def fast_group(j, size):
        for u in range(size):
            set_reference(j - u, u)
        vts = jnp.concatenate([vt_ref[0, 0, j - u] for u in range(size)], axis=1)
        for a in range(2):
            ps = jnp.concatenate([fast_weights(j - u, a, u) for u in range(size)], axis=0)
            acc_s[a] = acc_s[a] + _dot(vts, ps)

    def set_reference(j, slot):
        off = slope * ((i - j) * t).astype(F32)
        row = lax.broadcasted_iota(jnp.int32, (16, t), 0)
        for a in range(2):
            hi, mid, lo = [p.astype(F32) for p in _split_bf16(-(m_s[a] + off), 3)]
            tile = jnp.where(row == 0, hi, jnp.where(row == 1, mid, jnp.where(row == 2, lo, 0.0)))
            q_s[slot, a, A_REF:A_REF + 16, :] = tile.astype(BF16)

    def plain_diagonal():
        vtj = vt_ref[0, 0, i]
        for a in range(2):
            s = jnp.where(causal, scores(i, a, 0), NEG)
            acc_s[a] = _dot(vtj, jnp.exp2(s).astype(BF16))
            m_s[a] = jnp.zeros((1, t), F32)

    diag_reach = q_norm * kn_ref[(bi * A_HEADS + hd) * nb + i] + BOUND_MARGIN
    lax.cond(diag_reach <= ZERO_REF_LOG2, plain_diagonal, lambda: exact_block(i, True))

    def reach_of(j, m_low):
        k_norm = kn_ref[(bi * A_HEADS + hd) * nb + j]
        return (q_norm * k_norm + slope * (t - (i - j) * t).astype(F32) + BOUND_MARGIN) - m_low

    def single(j, m_low):
        reach = reach_of(j, m_low)

        def visit():
            set_reference(j, 0)

            def fast():
                fast_block(j)
                return m_low

            def exact():
                exact_block(j, False)
                return jnp.min(m_s[...])

            return lax.cond(reach <= FAST_LOG2, fast, exact)

        return lax.cond(reach < SKIP_LOG2, lambda: m_low, visit)

    def sweep(size, count, top, m_low):
        def body(n, m_low):
            j = top - size * n
            all_fast = jnp.bool_(True)
            for u in range(size):
                reach = reach_of(j - u, m_low)
                all_fast = jnp.logical_and(
                    all_fast, jnp.logical_and(reach >= SKIP_LOG2, reach <= FAST_LOG2))

            def group():
                fast_group(j, size)
                return m_low

            def one_by_one():
                return lax.fori_loop(0, size, lambda u, m: single(j - u, m), m_low)

            return lax.cond(all_fast, group, one_by_one)

        return lax.fori_loop(0, count, body, m_low)

    m_low = jnp.min(m_s[...])
    left = i
    for size in A_GROUPS:
        m_low = sweep(size, left // size, left - 1, m_low)
        left = left % size
    lax.fori_loop(0, left, lambda u, m: single(left - 1 - u, m), m_low)

    lam = (jnp.exp(jnp.sum(lq1_ref[...] * lk1_ref[...], keepdims=True))
           - jnp.exp(jnp.sum(lq2_ref[...] * lk2_ref[...], keepdims=True)) + lam_init)
    outs = [acc_s[a, :A_V, :] * (1.0 / acc_s[a, A_V:A_V + 1, :]) for a in range(2)]
    y_t = outs[0] - lam * outs[1]
    y_t = jnp.concatenate([y_t, jnp.zeros((LANES - A_V, t), F32)], axis=0)
    o_ref[0] = y_t.T


def _diff_attn(qt, k, vt, k_norms, lq1, lk1, lq2, lk2, lam_init):
    b, s, _ = k.shape
    t = A_TILE
    nb = s // t
    slopes = jnp.asarray([_slopes_log2(hd)[0] for hd in range(A_HEADS)], F32)
    smem = pl.BlockSpec(memory_space=pltpu.SMEM)
    vec = pl.BlockSpec((1, A_QK), lambda bi, h, i: (0, 0))
    return pl.pallas_call(
        functools.partial(_diff_attn_kernel, t=t, nb=nb, lam_init=lam_init),
        grid=(b, A_HEADS, nb),
        in_specs=[
            smem, smem, vec, vec, vec, vec,
            pl.BlockSpec((1, 1, 1, LANES, t), lambda bi, h, i: (bi, h, i, 0, 0)),
            pl.BlockSpec((1, s, LANES), lambda bi, h, i: (bi, 0, h)),
            pl.BlockSpec((1, 1, nb, A_VT, t), lambda bi, h, i: (bi, h, 0, 0, 0)),
        ],
        out_specs=pl.BlockSpec((1, t, LANES), lambda bi, h, i: (bi, i, h)),
        out_shape=jax.ShapeDtypeStruct((b, s, A_KW), F32),
        scratch_shapes=[
            pltpu.VMEM((max(A_GROUPS), 2, LANES, t), BF16),
            pltpu.VMEM((2, 1, t), F32),
            pltpu.VMEM((2, A_VT, t), F32),
        ],
        compiler_params=_params("parallel", "parallel", "arbitrary"),
        name="diff_attn",
    )(k_norms, slopes, lq1.reshape(1, A_QK), lk1.reshape(1, A_QK), lq2.reshape(1, A_QK),
      lk2.reshape(1, A_QK), qt, k, vt)


def _sb_attn_kernel(q_ref, k_ref, v_ref, o_ref, r_s, acc_s, *, tq):
    i = pl.program_id(2)
    q = q_ref[0]
    lane = lax.broadcasted_iota(jnp.int32, (tq, LANES), 1)
    qm = [jnp.where(lane < SB_DIM, q, jnp.zeros_like(q)),
          jnp.where(lane >= SB_DIM, q, jnp.zeros_like(q))]
    row = lax.broadcasted_iota(jnp.int32, (tq, tq), 0)
    col = lax.broadcasted_iota(jnp.int32, (tq, tq), 1)
    strict = col < row
    later = (row > col).astype(BF16)

    def keys(j):
        return k_ref[0, pl.ds(pl.multiple_of(j * tq, tq), tq), :]

    def values(j):
        return v_ref[0, pl.ds(pl.multiple_of(j * tq, tq), tq), :]

    def log_weights(kj, h, diagonal):
        z = _nt_dot(qm[h], kj)
        log_beta = jnp.minimum(z, 0.0) - jnp.log2(1.0 + jnp.exp2(-jnp.abs(z)))
        log_1mb = log_beta - z
        if diagonal:
            log_1mb = jnp.where(strict, log_1mb, 0.0)
        after = _dot(jnp.concatenate(_split_bf16(log_1mb, 2), axis=1), later2)
        return log_beta + after, jnp.sum(log_1mb, axis=1, keepdims=True)

    later2 = jnp.concatenate([later, later], axis=0)
    prev = jnp.maximum(i - 1, 0)
    has_prev = (i > 0).astype(F32)
    no_prev = jnp.where(i > 0, 0.0, NEG)
    k_diag, k_prev = keys(i), keys(prev)
    v_both = jnp.concatenate([values(i), values(prev)], axis=0)
    for h in range(2):
        lw_d, tot_d = log_weights(k_diag, h, True)
        lw_p, tot_p = log_weights(k_prev, h, False)
        a_d = jnp.where(strict, jnp.exp2(lw_d), 0.0)
        a_p = jnp.exp2(lw_p + (tot_d + no_prev))
        acc_s[h] = _dot(jnp.concatenate([a_d.astype(BF16), a_p.astype(BF16)], axis=1), v_both)
        r_s[h] = tot_d + tot_p * has_prev

    def cond(carry):
        j, live = carry
        return jnp.logical_and(j >= 0, live)

    def body(carry):
        j, _ = carry
        kj, vj = keys(j), values(j)
        for h in range(2):
            lw, tot = log_weights(kj, h, False)
            run = r_s[h]
            acc_s[h] = acc_s[h] + _dot(jnp.exp2(lw + run).astype(BF16), vj)
            r_s[h] = run + tot
        return j - 1, jnp.max(r_s[...]) > SKIP_LOG2

    lax.while_loop(cond, body, (i - 2, jnp.max(r_s[...]) > SKIP_LOG2))
    o_ref[0] = jnp.where(lane < SB_DIM, acc_s[0], acc_s[1])


def _sb_attn(q, k, v):
    b, s, _ = q.shape
    tq = min(ATT_TILE, s)
    pairs = SB_W // LANES
    return pl.pallas_call(
        functools.partial(_sb_attn_kernel, tq=tq),
        grid=(b, pairs, s // tq),
        in_specs=[
            pl.BlockSpec((1, tq, LANES), lambda bi, p, i: (bi, i, p)),
            pl.BlockSpec((1, s, LANES), lambda bi, p, i: (bi, 0, p)),
            pl.BlockSpec((1, s, LANES), lambda bi, p, i: (bi, 0, p)),
        ],
        out_specs=pl.BlockSpec((1, tq, LANES), lambda bi, p, i: (bi, i, p)),
        out_shape=jax.ShapeDtypeStruct((b, s, SB_W), F32),
        scratch_shapes=[
            pltpu.VMEM((2, tq, 1), F32),
            pltpu.VMEM((2, tq, LANES), F32),
        ],
        compiler_params=_params("parallel", "parallel", "arbitrary"),
        name="sb_attn",
    )(q, k, v)


def _swa_attn_kernel(sinks_ref, bias_ref, q_ref, kc_ref, kp_ref, vc_ref, vp_ref, o_ref, *, tq):
    i = pl.program_id(1)
    half = tq // 2
    col = lax.broadcasted_iota(jnp.int32, (half, tq), 1)
    has_key = col >= jnp.where(i > 0, 0, half)
    lane = lax.broadcasted_iota(jnp.int32, (half, LANES), 1)

    kwin = [jnp.concatenate([kp_ref[0, half:, :], kc_ref[0, :half, :]], axis=0), kc_ref[0]]
    vwin = [jnp.concatenate([vp_ref[0, half:, :], vc_ref[0, :half, :]], axis=0), vc_ref[0]]
    for hf in range(2):
        rows = slice(hf * half, (hf + 1) * half)
        for t in range(SWA_REP):
            q = q_ref[0, rows, t * LANES:(t + 1) * LANES]
            outs = []
            for g in range(SWA_KV_HEADS):
                head = g * SWA_REP + t
                qg = jnp.where((lane // SWA_DIM) == g, q, jnp.zeros_like(q))
                s = _nt_dot(qg, kwin[hf]) + bias_ref[head]
                if hf == 0:
                    s = jnp.where(has_key, s, NEG)
                sink = sinks_ref[head] * LOG2E
                m = jnp.maximum(jnp.max(s, axis=1, keepdims=True), sink)
                p = jnp.exp2(s - m)
                den = jnp.sum(p, axis=1, keepdims=True) + jnp.exp2(sink - m)
                outs.append(_dot(p.astype(BF16), vwin[hf]) / den)
            o_ref[0, rows, t * LANES:(t + 1) * LANES] = jnp.where(lane < SWA_DIM, outs[0], outs[1])


def _swa_bias(tq):
    half = tq // 2
    dist = np.arange(half)[:, None] + half - np.arange(tq)[None, :]
    in_window = (dist >= 0) & (dist < WINDOW)
    slopes = _alibi_slopes(SWA_Q_HEADS) * LOG2E
    bias = np.where(in_window[None], -slopes[:, None, None] * dist[None], NEG)
    return jnp.asarray(bias, F32)


def _swa_attn(q, k, v, sinks):
    b, s, _ = q.shape
    tq = min(ATT_TILE, s)
    smem = pl.BlockSpec(memory_space=pltpu.SMEM)
    cur = pl.BlockSpec((1, tq, SWA_KV_W), lambda bi, i: (bi, i, 0))
    prev = pl.BlockSpec((1, tq, SWA_KV_W), lambda bi, i: (bi, jnp.maximum(i - 1, 0), 0))
    return pl.pallas_call(
        functools.partial(_swa_attn_kernel, tq=tq),
        grid=(b, s // tq),
        in_specs=[
            smem,
            pl.BlockSpec((SWA_Q_HEADS, tq // 2, tq), lambda bi, i: (0, 0, 0)),
            pl.BlockSpec((1, tq, SWA_Q_W), lambda bi, i: (bi, i, 0)),
            cur, prev, cur, prev,
        ],
        out_specs=pl.BlockSpec((1, tq, SWA_Q_W), lambda bi, i: (bi, i, 0)),
        out_shape=jax.ShapeDtypeStruct((b, s, SWA_Q_W), F32),
        compiler_params=_params("parallel", "parallel"),
        name="swa_attn",
    )(sinks.astype(F32), _swa_bias(tq), q, k, k, v, v)


def _mix_ffn_kernel(ya_ref, yb_ref, yc_ref, x_ref, ga_ref, gb_ref, gc_ref, ln2_ref, mod_ref,
                    wa_ref, wb_ref, wc_ref, wg_ref, wu_ref, wd_ref, fg_ref, o_ref,
                    *, a_scale, final):
    mixed = _dot(_rms(yb_ref[0], gb_ref[...]).astype(BF16), wb_ref[...])
    mixed += _dot(_rms(yc_ref[0], gc_ref[...]).astype(BF16), wc_ref[...])
    normed = []
    for h in range(A_HEADS):
        cols = slice(h * LANES, (h + 1) * LANES)
        ya = ya_ref[0, :, cols]
        ms = jnp.sum(ya * ya, axis=-1, keepdims=True) * (1.0 / A_V)
        normed.append(ya * lax.rsqrt(ms + EPS) * ga_ref[:, cols] * a_scale)
    packed = [normed[h] + pltpu.roll(normed[h + 1], A_V, 1) for h in range(0, A_HEADS, 2)]
    mixed += _dot(jnp.concatenate(packed, axis=1).astype(BF16), wa_ref[...])
    x1 = x_ref[0] + mod_ref[0, 2:3, :] * mixed
    h2 = (_rms(x1, ln2_ref[...]) * (1.0 + mod_ref[0, 4:5, :]) + mod_ref[0, 3:4, :]).astype(BF16)
    gate = _dot(h2, wg_ref[...])
    up = _dot(h2, wu_ref[...])
    act = gate * jax.nn.sigmoid(gate) * up
    out = x1 + mod_ref[0, 5:6, :] * _dot(act.astype(BF16), wd_ref[...])
    if final:
        out = _rms(out, fg_ref[...])
    o_ref[0] = out


def _mix_ffn(ya, yb, yc, x, ga, gb, gc, ln2, mod, wa, wb, wc, wg, wu, wd, fg, a_scale, final):
    b, s, d = x.shape
    f = wg.shape[1]
    tm = min(FFN_TILE, s)
    row = lambda wd_: pl.BlockSpec((1, tm, wd_), lambda bi, i: (bi, i, 0))
    const = lambda r, cdim: pl.BlockSpec((r, cdim), lambda bi, i: (0, 0),
                                         pipeline_mode=pl.Buffered(1))
    return pl.pallas_call(
        functools.partial(_mix_ffn_kernel, a_scale=a_scale, final=final),
        grid=(b, s // tm),
        in_specs=[
            row(A_KW), row(SB_W), row(SWA_Q_W), row(d),
            const(1, A_KW), const(1, SB_W), const(1, SWA_Q_W), const(1, d),
            pl.BlockSpec((1, N_MOD, d), lambda bi, i: (bi, 0, 0)),
            const(A_W, d), const(SB_W, d), const(SWA_Q_W, d),
            const(d, f), const(d, f), const(f, d), const(1, d),
        ],
        out_specs=row(d),
        out_shape=jax.ShapeDtypeStruct((b, s, d), F32),
        compiler_params=_params("parallel", "parallel"),
        name="mix_ffn",
    )(ya, yb, yc, x, ga, gb, gc, ln2.reshape(1, d), mod, wa, wb, wc, wg, wu, wd, fg.reshape(1, d))


def _swa_perm():
    cols = []
    for t in range(SWA_REP):
        for g in range(SWA_KV_HEADS):
            head = g * SWA_REP + t
            cols.extend(range(head * SWA_DIM, (head + 1) * SWA_DIM))
    return jnp.asarray(cols, jnp.int32)


def _pad_heads(w, per_head, width):
    d = w.shape[0]
    w = w.reshape(d, A_HEADS, per_head)
    return jnp.pad(w, ((0, 0), (0, 0), (0, width - per_head))).reshape(d, A_HEADS * width)


def _prep_w_in(w):
    bounds = np.cumsum((0, A_W, A_W, A_W, SB_W, SB_W, SB_W, SWA_Q_W, SWA_KV_W, SWA_KV_W))
    qa, ka, va, qb, kb, vb, qc, kc, vc = [w[:, bounds[n]:bounds[n + 1]] for n in range(9)]
    row = jnp.concatenate([
        ka, qb * (SB_DIM ** -0.5 * LOG2E), kb, vb,
        (qc * (SWA_DIM ** -0.5 * LOG2E))[:, _swa_perm()], kc, vc], axis=1)
    transposed = jnp.concatenate([
        qa * (A_QK ** -0.5 * LOG2E), va], axis=1).T
    return row.astype(BF16), transposed.astype(BF16)


def kernel(x, c, ln1_g, ln2_g, w_mod, b_mod, w_in, lam_q1, lam_k1, lam_q2, lam_k2, diff_norm_g,
           sb_norm_g, swa_norm_g, swa_sinks, w_out, w_gate, w_up, w_down, final_g):
    depth = w_in.shape[0]
    perm = _swa_perm()
    mod = _modulation(c, w_mod, b_mod)
    for l in range(depth):
        lam_init = 0.8 - 0.6 * math.exp(-0.3 * l)
        w_row, w_t = _prep_w_in(w_in[l])
        ka, qb, kb, vb, qc, kc, vc, qta, vta, kn = _in_proj(x, ln1_g[l], mod[l], w_row, w_t)
        k_norms = kn[:, :, 0, :A_HEADS].transpose(0, 2, 1).reshape(-1)
        ya = _diff_attn(qta, ka, vta, k_norms, lam_q1[l], lam_k1[l], lam_q2[l], lam_k2[l], lam_init)
        yb = _sb_attn(qb, kb, vb)
        yc = _swa_attn(qc, kc, vc, swa_sinks[l])
        wo = w_out[l].astype(BF16)
        x = _mix_ffn(
            ya, yb, yc, x,
            _pad_heads(diff_norm_g[l].reshape(1, A_V).repeat(A_HEADS, 0).reshape(1, A_W), A_V, LANES),
            sb_norm_g[l].reshape(1, SB_W),
            swa_norm_g[l][perm].reshape(1, SWA_Q_W),
            ln2_g[l], mod[l],
            wo[:A_W], wo[A_W:A_W + SB_W], wo[A_W + SB_W:][perm],
            w_gate[l].astype(BF16), w_up[l].astype(BF16), w_down[l].astype(BF16), final_g,
            1.0 - lam_init, l == depth - 1)
    return x
```

```python
import functools
import math

import numpy as np
import jax
import jax.numpy as jnp
from jax import lax
from jax.experimental import pallas as pl
from jax.experimental.pallas import tpu as pltpu

F32 = jnp.float32
BF16 = jnp.bfloat16

N_MOD = 6
EPS = 1e-6
A_HEADS = 4
A_QK = 32
A_V = 64
SB_HEADS = 4
SB_DIM = 64
SWA_Q_HEADS = 8
SWA_KV_HEADS = 2
SWA_REP = 4
SWA_DIM = 64
WINDOW = 128
A_W = 256
SB_W = 256
SWA_Q_W = 512
SWA_KV_W = 128

LANES = 128
NEG = -1e30

VMEM_LIMIT = 56 * 1024 * 1024

ROW_TILE = 512
FFN_TILE = 256
ATT_TILE = 256
A_TILE = ROW_TILE

LOG2E = math.log2(math.e)
A_AUG = 2 * A_QK
A_REF = A_AUG + 16
A_PIECES = 4
A_VT = 80
A_KW = A_HEADS * LANES
A_TW = A_HEADS * (LANES + A_VT)
SKIP_LOG2 = -160.0
FAST_LOG2 = 80.0
A_GROUPS = (4, 2)
ZERO_REF_LOG2 = 60.0
BOUND_MARGIN = 1.0


def _params(*sem):
    return pltpu.CompilerParams(dimension_semantics=sem, vmem_limit_bytes=VMEM_LIMIT)


def _nt_dot(a, b):
    return lax.dot_general(a, b, (((1,), (1,)), ((), ())), preferred_element_type=F32)


def _dot(a, b):
    return jnp.dot(a, b, preferred_element_type=F32)


def _rms(x, g):
    ms = jnp.mean(x * x, axis=-1, keepdims=True)
    return x * lax.rsqrt(ms + EPS) * g


def _alibi_slopes(n):
    return 2.0 ** (-8.0 * np.arange(1, n + 1, dtype=np.float64) / n)


def _split_bf16(x, n):
    pieces = []
    for _ in range(n - 1):
        p = x.astype(BF16)
        pieces.append(p)
        x = x - p.astype(F32)
    pieces.append(x.astype(BF16))
    return pieces


def _mod_kernel(c_ref, w_ref, b_ref, o_ref):
    cv = c_ref[...]
    ca = cv * jax.nn.sigmoid(cv)
    o_ref[0] = jnp.dot(ca, w_ref[0], preferred_element_type=F32,
                       precision=lax.Precision.HIGHEST) + b_ref[0]


def _modulation(c, w_mod, b_mod):
    depth, d, n = w_mod.shape
    b = c.shape[0]
    rows = 8
    cp = jnp.zeros((rows, d), F32).at[:b].set(c)
    tn = 1024
    out = pl.pallas_call(
        _mod_kernel,
        grid=(depth, n // tn),
        in_specs=[
            pl.BlockSpec((rows, d), lambda l, j: (0, 0)),
            pl.BlockSpec((1, d, tn), lambda l, j: (l, 0, j)),
            pl.BlockSpec((1, 1, tn), lambda l, j: (l, 0, j)),
        ],
        out_specs=pl.BlockSpec((1, rows, tn), lambda l, j: (l, 0, j)),
        out_shape=jax.ShapeDtypeStruct((depth, rows, n), F32),
        compiler_params=_params("parallel", "parallel"),
        name="modulation",
    )(cp, w_mod, b_mod.reshape(depth, 1, n))
    return out[:, :b].reshape(depth, b, N_MOD, d)


ROW_SPLITS = (A_KW, SB_W, SB_W, SB_W, SWA_Q_W, SWA_KV_W, SWA_KV_W)


def _in_proj_kernel(x_ref, g_ref, mod_ref, w_ref, wt_ref, kaug_ref, taug_ref, *out_refs):
    row_refs, (qt_ref, vt_ref, kn_ref) = out_refs[:len(ROW_SPLITS)], out_refs[len(ROW_SPLITS):]
    x = x_ref[0]
    h = _rms(x, g_ref[...]) * (1.0 + mod_ref[0, 1:2, :]) + mod_ref[0, 0:1, :]
    hb = h.astype(BF16)
    proj = _dot(hb, w_ref[...])
    start = A_W
    for ref, width in zip(row_refs[1:], ROW_SPLITS[1:]):
        ref[0] = proj[:, start:start + width].astype(BF16)
        start += width
    feat = lax.broadcasted_iota(jnp.int32, (x.shape[0], LANES), 1)
    out_lane = lax.broadcasted_iota(jnp.int32, kn_ref.shape[2:], 1)
    norms = jnp.zeros(kn_ref.shape[2:], F32)
    for hd in range(A_HEADS):
        pair = proj[:, (hd // 2) * LANES:(hd // 2 + 1) * LANES]
        if hd % 2:
            pair = pltpu.roll(pair, A_AUG, 1)
        cols = slice(hd * LANES, (hd + 1) * LANES)
        keys = (jnp.where(feat < A_AUG, pair, 0.0) + kaug_ref[:, cols]).astype(BF16)
        row_refs[0][0, :, cols] = keys
        kf = keys.astype(F32)
        sq = jnp.sum(jnp.where(feat < A_AUG, kf * kf, 0.0), axis=1, keepdims=True)
        norms = jnp.where(out_lane == hd, jnp.sqrt(jnp.max(sq, axis=0, keepdims=True)), norms)
    kn_ref[0, 0] = norms
    proj_t = _nt_dot(wt_ref[...], hb)
    for hd in range(A_HEADS):
        q_rows = proj_t[hd * A_AUG:(hd + 1) * A_AUG]
        q_const = taug_ref[hd * LANES + A_AUG:(hd + 1) * LANES, :]
        qt_ref[0, hd, 0] = jnp.concatenate([q_rows, q_const], axis=0).astype(BF16)
        v_rows = proj_t[A_W + hd * A_V:A_W + (hd + 1) * A_V]
        v_const = taug_ref[A_KW + hd * A_VT + A_V:A_KW + (hd + 1) * A_VT, :]
        vt_ref[0, hd, 0] = jnp.concatenate([v_rows, v_const], axis=0).astype(BF16)


def _bf16_pieces(value, n):
    pieces = []
    rest = float(value)
    for _ in range(n):
        p = float(np.asarray(rest, np.float32).astype(jnp.bfloat16).astype(np.float64))
        pieces.append(p)
        rest -= p
    return pieces


def _slopes_log2(hd):
    pieces = _bf16_pieces(_alibi_slopes(A_HEADS)[hd] * LOG2E, A_PIECES)
    return sum(pieces), pieces


def _alibi_constants(t):
    idx = np.arange(t)
    lo, hi = idx % 256, idx - idx % 256
    kaug = np.zeros((t, A_KW), np.float32)
    taug = np.zeros((A_TW, t), np.float32)
    for hd in range(A_HEADS):
        _, pieces = _slopes_log2(hd)
        k0 = hd * LANES + A_AUG
        for n, piece in enumerate(pieces):
            taug[k0 + n] = -lo
            taug[k0 + A_PIECES + n] = -hi
            kaug[:, k0 + n] = piece
            kaug[:, k0 + A_PIECES + n] = piece
            taug[k0 + 2 * A_PIECES + n] = piece
            taug[k0 + 3 * A_PIECES + n] = piece
            kaug[:, k0 + 2 * A_PIECES + n] = lo
            kaug[:, k0 + 3 * A_PIECES + n] = hi
        r0 = hd * LANES + A_REF
        kaug[:, r0:r0 + 3] = 1.0
        taug[A_KW + hd * A_VT + A_V] = 1.0
    return jnp.asarray(kaug), jnp.asarray(taug)


def _in_proj(x, g, mod, w, wt):
    b, s, d = x.shape
    tm = A_TILE
    nb = s // tm
    n = w.shape[1]
    kaug, taug = _alibi_constants(tm)
    const = lambda shape: pl.BlockSpec(shape, lambda bi, i: (0,) * len(shape))
    return pl.pallas_call(
        _in_proj_kernel,
        grid=(b, nb),
        in_specs=[
            pl.BlockSpec((1, tm, d), lambda bi, i: (bi, i, 0)),
            const((1, d)),
            pl.BlockSpec((1, N_MOD, d), lambda bi, i: (bi, 0, 0)),
            const((d, n)), const(wt.shape), const((tm, A_KW)), const((A_TW, tm)),
        ],
        out_specs=[pl.BlockSpec((1, tm, wd), lambda bi, i: (bi, i, 0)) for wd in ROW_SPLITS] + [
            pl.BlockSpec((1, A_HEADS, 1, LANES, tm), lambda bi, i: (bi, 0, i, 0, 0)),
            pl.BlockSpec((1, A_HEADS, 1, A_VT, tm), lambda bi, i: (bi, 0, i, 0, 0)),
            pl.BlockSpec((1, 1, 8, LANES), lambda bi, i: (bi, i, 0, 0)),
        ],
        out_shape=[jax.ShapeDtypeStruct((b, s, wd), BF16) for wd in ROW_SPLITS] + [
            jax.ShapeDtypeStruct((b, A_HEADS, nb, LANES, tm), BF16),
            jax.ShapeDtypeStruct((b, A_HEADS, nb, A_VT, tm), BF16),
            jax.ShapeDtypeStruct((b, nb, 8, LANES), F32),
        ],
        compiler_params=_params("parallel", "parallel"),
        name="in_proj",
    )(x, g.reshape(1, d), mod, w, wt, kaug, taug)


def _diff_attn_kernel(kn_ref, slopes_ref, lq1_ref, lk1_ref, lq2_ref, lk2_ref, qt_ref, k_ref,
                      vt_ref, o_ref, q_s, m_s, acc_s, *, t, nb, lam_init):
    bi = pl.program_id(0)
    hd = pl.program_id(1)
    i = pl.program_id(2)
    slope = slopes_ref[hd]
    qt = qt_ref[0, 0, 0]
    feat = lax.broadcasted_iota(jnp.int32, (LANES, t), 0)
    zero = jnp.zeros_like(qt)
    for slot in range(max(A_GROUPS)):
        q_s[slot, 0] = jnp.where(jnp.logical_or(feat < A_QK, feat >= A_AUG), qt, zero)
        q_s[slot, 1] = jnp.where(feat >= A_QK, qt, zero)
    qsq = qt.astype(F32) * qt.astype(F32)
    n0 = jnp.sum(jnp.where(feat < A_QK, qsq, 0.0), axis=0, keepdims=True)
    n1 = jnp.sum(jnp.where(jnp.logical_and(feat >= A_QK, feat < A_AUG), qsq, 0.0),
                 axis=0, keepdims=True)
    q_norm = jnp.max(jnp.sqrt(jnp.maximum(n0, n1)))
    causal = (lax.broadcasted_iota(jnp.int32, (t, t), 0)
              <= lax.broadcasted_iota(jnp.int32, (t, t), 1))

    def scores(j, a, slot):
        kj = k_ref[0, pl.ds(pl.multiple_of(j * t, t), t), :]
        return _dot(kj, q_s[slot, a])

    def exact_block(j, first):
        vtj = vt_ref[0, 0, j]
        for a in range(2):
            s = scores(j, a, 0)
            if first:
                s = jnp.where(causal, s, NEG)
            top = jnp.max(s, axis=0, keepdims=True)
            shift = top if first else jnp.maximum(top, 0.0)
            pv = _dot(vtj, jnp.exp2(s - shift).astype(BF16))
            if first:
                acc_s[a] = pv
                m_s[a] = shift
            else:
                acc_s[a] = jnp.exp2(-shift) * acc_s[a] + pv
                m_s[a] = m_s[a] + shift

    def fast_weights(j, a, slot):
        return jnp.exp2(scores(j, a, slot)).astype(BF16)

    def fast_block(j):
        vtj = vt_ref[0, 0, j]
        for a in range(2):
            acc_s[a] = acc_s[a] + _dot(vtj, fast_weights(j, a, 0))

    def fast_group(j, size):
        for u in range(size):
            set_reference(j - u, u)
        vts = jnp.concatenate([vt_ref[0, 0, j - u] for u in range(size)], axis=1)
        for a in range(2):
            ps = jnp.concatenate([fast_weights(j - u, a, u) for u in range(size)], axis=0)
            acc_s[a] = acc_s[a] + _dot(vts, ps)

    def set_reference(j, slot):
        off = slope * ((i - j) * t).astype(F32)
        row = lax.broadcasted_iota(jnp.int32, (16, t), 0)
        for a in range(2):
            hi, mid, lo = [p.astype(F32) for p in _split_bf16(-(m_s[a] + off), 3)]
            tile = jnp.where(row == 0, hi, jnp.where(row == 1, mid, jnp.where(row == 2, lo, 0.0)))
            q_s[slot, a, A_REF:A_REF + 16, :] = tile.astype(BF16)

    def plain_diagonal():
        vtj = vt_ref[0, 0, i]
        for a in range(2):
            s = jnp.where(causal, scores(i, a, 0), NEG)
            acc_s[a] = _dot(vtj, jnp.exp2(s).astype(BF16))
            m_s[a] = jnp.zeros((1, t), F32)

    diag_reach = q_norm * kn_ref[(bi * A_HEADS + hd) * nb + i] + BOUND_MARGIN
    lax.cond(diag_reach <= ZERO_REF_LOG2, plain_diagonal, lambda: exact_block(i, True))

    def reach_of(j, m_low):
        k_norm = kn_ref[(bi * A_HEADS + hd) * nb + j]
        return (q_norm * k_norm + slope * (t - (i - j) * t).astype(F32) + BOUND_MARGIN) - m_low

    def single(j, m_low):
        reach = reach_of(j, m_low)

        def visit():
            set_reference(j, 0)

            def fast():
                fast_block(j)
                return m_low

            def exact():
                exact_block(j, False)
                return jnp.min(m_s[...])

            return lax.cond(reach <= FAST_LOG2, fast, exact)

        return lax.cond(reach < SKIP_LOG2, lambda: m_low, visit)

    def sweep(size, count, top, m_low):
        def body(n, m_low):
            j = top - size * n
            all_fast = jnp.bool_(True)
            for u in range(size):
                reach = reach_of(j - u, m_low)
                all_fast = jnp.logical_and(
                    all_fast, jnp.logical_and(reach >= SKIP_LOG2, reach <= FAST_LOG2))

            def group():
                fast_group(j, size)
                return m_low

            def one_by_one():
                return lax.fori_loop(0, size, lambda u, m: single(j - u, m), m_low)

            return lax.cond(all_fast, group, one_by_one)

        return lax.fori_loop(0, count, body, m_low)

    m_low = jnp.min(m_s[...])
    left = i
    for size in A_GROUPS:
        m_low = sweep(size, left // size, left - 1, m_low)
        left = left % size
    lax.fori_loop(0, left, lambda u, m: single(left - 1 - u, m), m_low)

    lam = (jnp.exp(jnp.sum(lq1_ref[...] * lk1_ref[...], keepdims=True))
           - jnp.exp(jnp.sum(lq2_ref[...] * lk2_ref[...], keepdims=True)) + lam_init)
    outs = [acc_s[a, :A_V, :] * (1.0 / acc_s[a, A_V:A_V + 1, :]) for a in range(2)]
    o_ref[0, 0] = outs[0] - lam * outs[1]


def _diff_attn(qt, k, vt, k_norms, lq1, lk1, lq2, lk2, lam_init):
    b, s, _ = k.shape
    t = A_TILE
    nb = s // t
    slopes = jnp.asarray([_slopes_log2(hd)[0] for hd in range(A_HEADS)], F32)
    smem = pl.BlockSpec(memory_space=pltpu.SMEM)
    vec = pl.BlockSpec((1, A_QK), lambda bi, h, i: (0, 0))
    return pl.pallas_call(
        functools.partial(_diff_attn_kernel, t=t, nb=nb, lam_init=lam_init),
        grid=(b, A_HEADS, nb),
        in_specs=[
            smem, smem, vec, vec, vec, vec,
            pl.BlockSpec((1, 1, 1, LANES, t), lambda bi, h, i: (bi, h, i, 0, 0)),
            pl.BlockSpec((1, s, LANES), lambda bi, h, i: (bi, 0, h)),
            pl.BlockSpec((1, 1, nb, A_VT, t), lambda bi, h, i: (bi, h, 0, 0, 0)),
        ],
        out_specs=pl.BlockSpec((1, 1, A_V, t), lambda bi, h, i: (bi, h, 0, i)),
        out_shape=jax.ShapeDtypeStruct((b, A_HEADS, A_V, s), F32),
        scratch_shapes=[
            pltpu.VMEM((max(A_GROUPS), 2, LANES, t), BF16),
            pltpu.VMEM((2, 1, t), F32),
            pltpu.VMEM((2, A_VT, t), F32),
        ],
        compiler_params=_params("parallel", "parallel", "arbitrary"),
        name="diff_attn",
    )(k_norms, slopes, lq1.reshape(1, A_QK), lk1.reshape(1, A_QK), lq2.reshape(1, A_QK),
      lk2.reshape(1, A_QK), qt, k, vt)


def _sb_attn_kernel(q_ref, k_ref, v_ref, o_ref, r_s, acc_s, *, tq):
    i = pl.program_id(2)
    q = q_ref[0]
    lane = lax.broadcasted_iota(jnp.int32, (tq, LANES), 1)
    qm = [jnp.where(lane < SB_DIM, q, jnp.zeros_like(q)),
          jnp.where(lane >= SB_DIM, q, jnp.zeros_like(q))]
    row = lax.broadcasted_iota(jnp.int32, (tq, tq), 0)
    col = lax.broadcasted_iota(jnp.int32, (tq, tq), 1)
    strict = col < row
    later = (row > col).astype(BF16)

    def keys(j):
        return k_ref[0, pl.ds(pl.multiple_of(j * tq, tq), tq), :]

    def values(j):
        return v_ref[0, pl.ds(pl.multiple_of(j * tq, tq), tq), :]

    def log_weights(kj, h, diagonal):
        z = _nt_dot(qm[h], kj)
        log_beta = jnp.minimum(z, 0.0) - jnp.log2(1.0 + jnp.exp2(-jnp.abs(z)))
        log_1mb = log_beta - z
        if diagonal:
            log_1mb = jnp.where(strict, log_1mb, 0.0)
        after = _dot(jnp.concatenate(_split_bf16(log_1mb, 2), axis=1), later2)
        return log_beta + after, jnp.sum(log_1mb, axis=1, keepdims=True)

    later2 = jnp.concatenate([later, later], axis=0)
    prev = jnp.maximum(i - 1, 0)
    has_prev = (i > 0).astype(F32)
    no_prev = jnp.where(i > 0, 0.0, NEG)
    k_diag, k_prev = keys(i), keys(prev)
    v_both = jnp.concatenate([values(i), values(prev)], axis=0)
    for h in range(2):
        lw_d, tot_d = log_weights(k_diag, h, True)
        lw_p, tot_p = log_weights(k_prev, h, False)
        a_d = jnp.where(strict, jnp.exp2(lw_d), 0.0)
        a_p = jnp.exp2(lw_p + (tot_d + no_prev))
        acc_s[h] = _dot(jnp.concatenate([a_d.astype(BF16), a_p.astype(BF16)], axis=1), v_both)
        r_s[h] = tot_d + tot_p * has_prev

    def cond(carry):
        j, live = carry
        return jnp.logical_and(j >= 0, live)

    def body(carry):
        j, _ = carry
        kj, vj = keys(j), values(j)
        for h in range(2):
            lw, tot = log_weights(kj, h, False)
            run = r_s[h]
            acc_s[h] = acc_s[h] + _dot(jnp.exp2(lw + run).astype(BF16), vj)
            r_s[h] = run + tot
        return j - 1, jnp.max(r_s[...]) > SKIP_LOG2

    lax.while_loop(cond, body, (i - 2, jnp.max(r_s[...]) > SKIP_LOG2))
    o_ref[0] = jnp.where(lane < SB_DIM, acc_s[0], acc_s[1])


def _sb_attn(q, k, v):
    b, s, _ = q.shape
    tq = min(ATT_TILE, s)
    pairs = SB_W // LANES
    return pl.pallas_call(
        functools.partial(_sb_attn_kernel, tq=tq),
        grid=(b, pairs, s // tq),
        in_specs=[
            pl.BlockSpec((1, tq, LANES), lambda bi, p, i: (bi, i, p)),
            pl.BlockSpec((1, s, LANES), lambda bi, p, i: (bi, 0, p)),
            pl.BlockSpec((1, s, LANES), lambda bi, p, i: (bi, 0, p)),
        ],
        out_specs=pl.BlockSpec((1, tq, LANES), lambda bi, p, i: (bi, i, p)),
        out_shape=jax.ShapeDtypeStruct((b, s, SB_W), F32),
        scratch_shapes=[
            pltpu.VMEM((2, tq, 1), F32),
            pltpu.VMEM((2, tq, LANES), F32),
        ],
        compiler_params=_params("parallel", "parallel", "arbitrary"),
        name="sb_attn",
    )(q, k, v)


def _swa_attn_kernel(sinks_ref, bias_ref, q_ref, kc_ref, kp_ref, vc_ref, vp_ref, o_ref, *, tq):
    i = pl.program_id(1)
    half = tq // 2
    col = lax.broadcasted_iota(jnp.int32, (half, tq), 1)
    has_key = col >= jnp.where(i > 0, 0, half)
    lane = lax.broadcasted_iota(jnp.int32, (half, LANES), 1)

    kwin = [jnp.concatenate([kp_ref[0, half:, :], kc_ref[0, :half, :]], axis=0), kc_ref[0]]
    vwin = [jnp.concatenate([vp_ref[0, half:, :], vc_ref[0, :half, :]], axis=0), vc_ref[0]]
    for hf in range(2):
        rows = slice(hf * half, (hf + 1) * half)
        for t in range(SWA_REP):
            q = q_ref[0, rows, t * LANES:(t + 1) * LANES]
            outs = []
            for g in range(SWA_KV_HEADS):
                head = g * SWA_REP + t
                qg = jnp.where((lane // SWA_DIM) == g, q, jnp.zeros_like(q))
                s = _nt_dot(qg, kwin[hf]) + bias_ref[head]
                if hf == 0:
                    s = jnp.where(has_key, s, NEG)
                sink = sinks_ref[head] * LOG2E
                m = jnp.maximum(jnp.max(s, axis=1, keepdims=True), sink)
                p = jnp.exp2(s - m)
                den = jnp.sum(p, axis=1, keepdims=True) + jnp.exp2(sink - m)
                outs.append(_dot(p.astype(BF16), vwin[hf]) / den)
            o_ref[0, rows, t * LANES:(t + 1) * LANES] = jnp.where(lane < SWA_DIM, outs[0], outs[1])


def _swa_bias(tq):
    half = tq // 2
    dist = np.arange(half)[:, None] + half - np.arange(tq)[None, :]
    in_window = (dist >= 0) & (dist < WINDOW)
    slopes = _alibi_slopes(SWA_Q_HEADS) * LOG2E
    bias = np.where(in_window[None], -slopes[:, None, None] * dist[None], NEG)
    return jnp.asarray(bias, F32)


def _swa_attn(q, k, v, sinks):
    b, s, _ = q.shape
    tq = min(ATT_TILE, s)
    smem = pl.BlockSpec(memory_space=pltpu.SMEM)
    cur = pl.BlockSpec((1, tq, SWA_KV_W), lambda bi, i: (bi, i, 0))
    prev = pl.BlockSpec((1, tq, SWA_KV_W), lambda bi, i: (bi, jnp.maximum(i - 1, 0), 0))
    return pl.pallas_call(
        functools.partial(_swa_attn_kernel, tq=tq),
        grid=(b, s // tq),
        in_specs=[
            smem,
            pl.BlockSpec((SWA_Q_HEADS, tq // 2, tq), lambda bi, i: (0, 0, 0)),
            pl.BlockSpec((1, tq, SWA_Q_W), lambda bi, i: (bi, i, 0)),
            cur, prev, cur, prev,
        ],
        out_specs=pl.BlockSpec((1, tq, SWA_Q_W), lambda bi, i: (bi, i, 0)),
        out_shape=jax.ShapeDtypeStruct((b, s, SWA_Q_W), F32),
        compiler_params=_params("parallel", "parallel"),
        name="swa_attn",
    )(sinks.astype(F32), _swa_bias(tq), q, k, k, v, v)


def _mix_ffn_kernel(ya_ref, yb_ref, yc_ref, x_ref, ga_ref, gb_ref, gc_ref, ln2_ref, mod_ref,
                    wa_ref, wb_ref, wc_ref, wg_ref, wu_ref, wd_ref, fg_ref, o_ref,
                    *, a_scale, final):
    mixed = _dot(_rms(yb_ref[0], gb_ref[...]).astype(BF16), wb_ref[...])
    mixed += _dot(_rms(yc_ref[0], gc_ref[...]).astype(BF16), wc_ref[...])
    packed = []
    for h in range(0, A_HEADS, 2):
        pair = []
        for yt in (ya_ref[0, h], ya_ref[0, h + 1]):
            ms = jnp.mean(yt * yt, axis=0, keepdims=True)
            pair.append(yt * lax.rsqrt(ms + EPS))
        packed.append(jnp.concatenate(pair, axis=0).T)
    na = jnp.concatenate(packed, axis=1) * ga_ref[...] * a_scale
    mixed += _dot(na.astype(BF16), wa_ref[...])
    x1 = x_ref[0] + mod_ref[0, 2:3, :] * mixed
    h2 = (_rms(x1, ln2_ref[...]) * (1.0 + mod_ref[0, 4:5, :]) + mod_ref[0, 3:4, :]).astype(BF16)
    gate = _dot(h2, wg_ref[...])
    up = _dot(h2, wu_ref[...])
    act = gate * jax.nn.sigmoid(gate) * up
    out = x1 + mod_ref[0, 5:6, :] * _dot(act.astype(BF16), wd_ref[...])
    if final:
        out = _rms(out, fg_ref[...])
    o_ref[0] = out


def _mix_ffn(ya, yb, yc, x, ga, gb, gc, ln2, mod, wa, wb, wc, wg, wu, wd, fg, a_scale, final):
    b, s, d = x.shape
    f = wg.shape[1]
    tm = min(FFN_TILE, s)
    row = lambda wd_: pl.BlockSpec((1, tm, wd_), lambda bi, i: (bi, i, 0))
    const = lambda r, cdim: pl.BlockSpec((r, cdim), lambda bi, i: (0, 0),
                                         pipeline_mode=pl.Buffered(1))
    return pl.pallas_call(
        functools.partial(_mix_ffn_kernel, a_scale=a_scale, final=final),
        grid=(b, s // tm),
        in_specs=[
            pl.BlockSpec((1, A_HEADS, A_V, tm), lambda bi, i: (bi, 0, 0, i)),
            row(SB_W), row(SWA_Q_W), row(d),
            const(1, A_W), const(1, SB_W), const(1, SWA_Q_W), const(1, d),
            pl.BlockSpec((1, N_MOD, d), lambda bi, i: (bi, 0, 0)),
            const(A_W, d), const(SB_W, d), const(SWA_Q_W, d),
            const(d, f), const(d, f), const(f, d), const(1, d),
        ],
        out_specs=row(d),
        out_shape=jax.ShapeDtypeStruct((b, s, d), F32),
        compiler_params=_params("parallel", "parallel"),
        name="mix_ffn",
    )(ya, yb, yc, x, ga, gb, gc, ln2.reshape(1, d), mod, wa, wb, wc, wg, wu, wd, fg.reshape(1, d))


def _swa_perm():
    cols = []
    for t in range(SWA_REP):
        for g in range(SWA_KV_HEADS):
            head = g * SWA_REP + t
            cols.extend(range(head * SWA_DIM, (head + 1) * SWA_DIM))
    return jnp.asarray(cols, jnp.int32)


def _prep_w_in(w):
    bounds = np.cumsum((0, A_W, A_W, A_W, SB_W, SB_W, SB_W, SWA_Q_W, SWA_KV_W, SWA_KV_W))
    qa, ka, va, qb, kb, vb, qc, kc, vc = [w[:, bounds[n]:bounds[n + 1]] for n in range(9)]
    row = jnp.concatenate([
        ka, qb * (SB_DIM ** -0.5 * LOG2E), kb, vb,
        (qc * (SWA_DIM ** -0.5 * LOG2E))[:, _swa_perm()], kc, vc], axis=1)
    transposed = jnp.concatenate([
        qa * (A_QK ** -0.5 * LOG2E), va], axis=1).T
    return row.astype(BF16), transposed.astype(BF16)


def kernel(x, c, ln1_g, ln2_g, w_mod, b_mod, w_in, lam_q1, lam_k1, lam_q2, lam_k2, diff_norm_g,
           sb_norm_g, swa_norm_g, swa_sinks, w_out, w_gate, w_up, w_down, final_g):
    depth = w_in.shape[0]
    perm = _swa_perm()
    mod = _modulation(c, w_mod, b_mod)
    for l in range(depth):
        lam_init = 0.8 - 0.6 * math.exp(-0.3 * l)
        w_row, w_t = _prep_w_in(w_in[l])
        ka, qb, kb, vb, qc, kc, vc, qta, vta, kn = _in_proj(x, ln1_g[l], mod[l], w_row, w_t)
        k_norms = kn[:, :, 0, :A_HEADS].transpose(0, 2, 1).reshape(-1)
        ya = _diff_attn(qta, ka, vta, k_norms, lam_q1[l], lam_k1[l], lam_q2[l], lam_k2[l], lam_init)
        yb = _sb_attn(qb, kb, vb)
        yc = _swa_attn(qc, kc, vc, swa_sinks[l])
        wo = w_out[l].astype(BF16)
        x = _mix_ffn(
            ya, yb, yc, x,
            jnp.tile(diff_norm_g[l], A_HEADS).reshape(1, A_W),
            sb_norm_g[l].reshape(1, SB_W),
            swa_norm_g[l][perm].reshape(1, SWA_Q_W),
            ln2_g[l], mod[l],
            wo[:A_W], wo[A_W:A_W + SB_W], wo[A_W + SB_W:][perm],
            w_gate[l].astype(BF16), w_up[l].astype(BF16), w_down[l].astype(BF16), final_g,
            1.0 - lam_init, l == depth - 1)
    return x
```

```python
import functools
import math

import numpy as np
import jax
import jax.numpy as jnp
from jax import lax
from jax.experimental import pallas as pl
from jax.experimental.pallas import tpu as pltpu

F32 = jnp.float32
BF16 = jnp.bfloat16

N_MOD = 6
EPS = 1e-6
A_HEADS = 4
A_QK = 32
A_V = 64
SB_HEADS = 4
SB_DIM = 64
SWA_Q_HEADS = 8
SWA_KV_HEADS = 2
SWA_REP = 4
SWA_DIM = 64
WINDOW = 128
A_W = 256
SB_W = 256
SWA_Q_W = 512
SWA_KV_W = 128

LANES = 128
NEG = -1e30

VMEM_LIMIT = 56 * 1024 * 1024

ROW_TILE = 512
FFN_TILE = 512
ATT_TILE = 256
A_TILE = ROW_TILE

LOG2E = math.log2(math.e)
A_AUG = 2 * A_QK
A_REF = A_AUG + 16
A_PIECES = 4
A_VT = 80
A_KW = A_HEADS * LANES
A_TW = A_HEADS * (LANES + A_VT)
SKIP_LOG2 = -160.0
FAST_LOG2 = 80.0
A_GROUPS = (4, 2)
ZERO_REF_LOG2 = 60.0
BOUND_MARGIN = 1.0


def _params(*sem):
    return pltpu.CompilerParams(dimension_semantics=sem, vmem_limit_bytes=VMEM_LIMIT)


def _nt_dot(a, b):
    return lax.dot_general(a, b, (((1,), (1,)), ((), ())), preferred_element_type=F32)


def _dot(a, b):
    return jnp.dot(a, b, preferred_element_type=F32)


def _rms(x, g):
    ms = jnp.mean(x * x, axis=-1, keepdims=True)
    return x * lax.rsqrt(ms + EPS) * g


def _alibi_slopes(n):
    return 2.0 ** (-8.0 * np.arange(1, n + 1, dtype=np.float64) / n)


def _split_bf16(x, n):
    pieces = []
    for _ in range(n - 1):
        p = x.astype(BF16)
        pieces.append(p)
        x = x - p.astype(F32)
    pieces.append(x.astype(BF16))
    return pieces


def _mod_kernel(c_ref, w_ref, b_ref, o_ref):
    cv = c_ref[...]
    ca = cv * jax.nn.sigmoid(cv)
    o_ref[0] = jnp.dot(ca, w_ref[0], preferred_element_type=F32,
                       precision=lax.Precision.HIGHEST) + b_ref[0]


def _modulation(c, w_mod, b_mod):
    depth, d, n = w_mod.shape
    b = c.shape[0]
    rows = 8
    cp = jnp.zeros((rows, d), F32).at[:b].set(c)
    tn = 1024
    out = pl.pallas_call(
        _mod_kernel,
        grid=(depth, n // tn),
        in_specs=[
            pl.BlockSpec((rows, d), lambda l, j: (0, 0)),
            pl.BlockSpec((1, d, tn), lambda l, j: (l, 0, j)),
            pl.BlockSpec((1, 1, tn), lambda l, j: (l, 0, j)),
        ],
        out_specs=pl.BlockSpec((1, rows, tn), lambda l, j: (l, 0, j)),
        out_shape=jax.ShapeDtypeStruct((depth, rows, n), F32),
        compiler_params=_params("parallel", "parallel"),
        name="modulation",
    )(cp, w_mod, b_mod.reshape(depth, 1, n))
    return out[:, :b].reshape(depth, b, N_MOD, d)


ROW_SPLITS = (A_KW, SB_W, SB_W, SB_W, SWA_Q_W, SWA_KV_W, SWA_KV_W)


def _in_proj_kernel(x_ref, g_ref, mod_ref, w_ref, wt_ref, kaug_ref, taug_ref, *out_refs):
    row_refs, (qt_ref, vt_ref, kn_ref) = out_refs[:len(ROW_SPLITS)], out_refs[len(ROW_SPLITS):]
    x = x_ref[0]
    h = _rms(x, g_ref[...]) * (1.0 + mod_ref[0, 1:2, :]) + mod_ref[0, 0:1, :]
    hb = h.astype(BF16)
    proj = _dot(hb, w_ref[...])
    start = A_W
    for ref, width in zip(row_refs[1:], ROW_SPLITS[1:]):
        ref[0] = proj[:, start:start + width].astype(BF16)
        start += width
    feat = lax.broadcasted_iota(jnp.int32, (x.shape[0], LANES), 1)
    out_lane = lax.broadcasted_iota(jnp.int32, kn_ref.shape[2:], 1)
    norms = jnp.zeros(kn_ref.shape[2:], F32)
    for hd in range(A_HEADS):
        pair = proj[:, (hd // 2) * LANES:(hd // 2 + 1) * LANES]
        if hd % 2:
            pair = pltpu.roll(pair, A_AUG, 1)
        cols = slice(hd * LANES, (hd + 1) * LANES)
        keys = (jnp.where(feat < A_AUG, pair, 0.0) + kaug_ref[:, cols]).astype(BF16)
        row_refs[0][0, :, cols] = keys
        kf = keys.astype(F32)
        sq = jnp.sum(jnp.where(feat < A_AUG, kf * kf, 0.0), axis=1, keepdims=True)
        norms = jnp.where(out_lane == hd, jnp.sqrt(jnp.max(sq, axis=0, keepdims=True)), norms)
    kn_ref[0, 0] = norms
    proj_t = _nt_dot(wt_ref[...], hb)
    for hd in range(A_HEADS):
        q_rows = proj_t[hd * A_AUG:(hd + 1) * A_AUG]
        q_const = taug_ref[hd * LANES + A_AUG:(hd + 1) * LANES, :]
        qt_ref[0, hd, 0] = jnp.concatenate([q_rows, q_const], axis=0).astype(BF16)
        v_rows = proj_t[A_W + hd * A_V:A_W + (hd + 1) * A_V]
        v_const = taug_ref[A_KW + hd * A_VT + A_V:A_KW + (hd + 1) * A_VT, :]
        vt_ref[0, hd, 0] = jnp.concatenate([v_rows, v_const], axis=0).astype(BF16)


def _bf16_pieces(value, n):
    pieces = []
    rest = float(value)
    for _ in range(n):
        p = float(np.asarray(rest, np.float32).astype(jnp.bfloat16).astype(np.float64))
        pieces.append(p)
        rest -= p
    return pieces


def _slopes_log2(hd):
    pieces = _bf16_pieces(_alibi_slopes(A_HEADS)[hd] * LOG2E, A_PIECES)
    return sum(pieces), pieces


def _alibi_constants(t):
    idx = np.arange(t)
    lo, hi = idx % 256, idx - idx % 256
    kaug = np.zeros((t, A_KW), np.float32)
    taug = np.zeros((A_TW, t), np.float32)
    for hd in range(A_HEADS):
        _, pieces = _slopes_log2(hd)
        k0 = hd * LANES + A_AUG
        for n, piece in enumerate(pieces):
            taug[k0 + n] = -lo
            taug[k0 + A_PIECES + n] = -hi
            kaug[:, k0 + n] = piece
            kaug[:, k0 + A_PIECES + n] = piece
            taug[k0 + 2 * A_PIECES + n] = piece
            taug[k0 + 3 * A_PIECES + n] = piece
            kaug[:, k0 + 2 * A_PIECES + n] = lo
            kaug[:, k0 + 3 * A_PIECES + n] = hi
        r0 = hd * LANES + A_REF
        kaug[:, r0:r0 + 3] = 1.0
        taug[A_KW + hd * A_VT + A_V] = 1.0
    return jnp.asarray(kaug), jnp.asarray(taug)


def _in_proj(x, g, mod, w, wt):
    b, s, d = x.shape
    tm = A_TILE
    nb = s // tm
    n = w.shape[1]
    kaug, taug = _alibi_constants(tm)
    const = lambda shape: pl.BlockSpec(shape, lambda bi, i: (0,) * len(shape))
    return pl.pallas_call(
        _in_proj_kernel,
        grid=(b, nb),
        in_specs=[
            pl.BlockSpec((1, tm, d), lambda bi, i: (bi, i, 0)),
            const((1, d)),
            pl.BlockSpec((1, N_MOD, d), lambda bi, i: (bi, 0, 0)),
            const((d, n)), const(wt.shape), const((tm, A_KW)), const((A_TW, tm)),
        ],
        out_specs=[pl.BlockSpec((1, tm, wd), lambda bi, i: (bi, i, 0)) for wd in ROW_SPLITS] + [
            pl.BlockSpec((1, A_HEADS, 1, LANES, tm), lambda bi, i: (bi, 0, i, 0, 0)),
            pl.BlockSpec((1, A_HEADS, 1, A_VT, tm), lambda bi, i: (bi, 0, i, 0, 0)),
            pl.BlockSpec((1, 1, 8, LANES), lambda bi, i: (bi, i, 0, 0)),
        ],
        out_shape=[jax.ShapeDtypeStruct((b, s, wd), BF16) for wd in ROW_SPLITS] + [
            jax.ShapeDtypeStruct((b, A_HEADS, nb, LANES, tm), BF16),
            jax.ShapeDtypeStruct((b, A_HEADS, nb, A_VT, tm), BF16),
            jax.ShapeDtypeStruct((b, nb, 8, LANES), F32),
        ],
        compiler_params=_params("parallel", "parallel"),
        name="in_proj",
    )(x, g.reshape(1, d), mod, w, wt, kaug, taug)


def _diff_attn_kernel(kn_ref, slopes_ref, lq1_ref, lk1_ref, lq2_ref, lk2_ref, qt_ref, k_ref,
                      vt_ref, o_ref, q_s, m_s, acc_s, *, t, nb, lam_init):
    bi = pl.program_id(0)
    hd = pl.program_id(1)
    i = pl.program_id(2)
    slope = slopes_ref[hd]
    qt = qt_ref[0, 0, 0]
    feat = lax.broadcasted_iota(jnp.int32, (LANES, t), 0)
    zero = jnp.zeros_like(qt)
    q_s[0] = jnp.where(jnp.logical_or(feat < A_QK, feat >= A_AUG), qt, zero)
    q_s[1] = jnp.where(feat >= A_QK, qt, zero)
    qsq = qt.astype(F32) * qt.astype(F32)
    n0 = jnp.sum(jnp.where(feat < A_QK, qsq, 0.0), axis=0, keepdims=True)
    n1 = jnp.sum(jnp.where(jnp.logical_and(feat >= A_QK, feat < A_AUG), qsq, 0.0),
                 axis=0, keepdims=True)
    q_norm = jnp.max(jnp.sqrt(jnp.maximum(n0, n1)))
    key_minus_query = (lax.broadcasted_iota(jnp.int32, (t, t), 0)
                       - lax.broadcasted_iota(jnp.int32, (t, t), 1))
    causal = key_minus_query <= 0

    def reference_rows(j, a):
        off = slope * ((i - j) * t).astype(F32)
        row = lax.broadcasted_iota(jnp.int32, (16, t), 0)
        hi, mid, lo = [p.astype(F32) for p in _split_bf16(-(m_s[a] + off), 3)]
        tile = jnp.where(row == 0, hi, jnp.where(row == 1, mid, jnp.where(row == 2, lo, 0.0)))
        return tile.astype(BF16)

    def scores(j, a, ref=None):
        kj = k_ref[0, pl.ds(pl.multiple_of(j * t, t), t), :]
        qa = q_s[a]
        if ref is not None:
            qa = jnp.concatenate([qa[:A_REF], ref, qa[A_REF + 16:]], axis=0)
        return _dot(kj, qa)

    def exact_block(j, first):
        vtj = vt_ref[0, 0, j]
        for a in range(2):
            s = scores(j, a, None if first else reference_rows(j, a))
            if first:
                s = jnp.where(causal, s, NEG)
            top = jnp.max(s, axis=0, keepdims=True)
            shift = top if first else jnp.maximum(top, 0.0)
            pv = _dot(vtj, jnp.exp2(s - shift).astype(BF16))
            if first:
                acc_s[a] = pv
                m_s[a] = shift
            else:
                acc_s[a] = jnp.exp2(-shift) * acc_s[a] + pv
                m_s[a] = m_s[a] + shift

    def fast_weights(j, a, diagonal):
        s = scores(j, a, reference_rows(j, a))
        if diagonal:
            s = jnp.where(causal, s, NEG)
        return jnp.exp2(s).astype(BF16)

    def fast_group(j, size, from_diagonal=False):
        vts = jnp.concatenate([vt_ref[0, 0, j - u] for u in range(size)], axis=1)
        for a in range(2):
            ps = jnp.concatenate(
                [fast_weights(j - u, a, from_diagonal and u == 0) for u in range(size)], axis=0)
            acc_s[a] = acc_s[a] + _dot(vts, ps)

    def reach_of(j, m_low):
        k_norm = kn_ref[(bi * A_HEADS + hd) * nb + j]
        alibi = jnp.where(j == i, 0.0, slope * (t - (i - j) * t).astype(F32))
        return (q_norm * k_norm + alibi + BOUND_MARGIN) - m_low

    def all_fast(j, size, m_low):
        ok = jnp.bool_(True)
        for u in range(size):
            reach = reach_of(j - u, m_low)
            ok = jnp.logical_and(ok, jnp.logical_and(reach >= SKIP_LOG2, reach <= FAST_LOG2))
        return ok

    plain = q_norm * kn_ref[(bi * A_HEADS + hd) * nb + i] + BOUND_MARGIN <= ZERO_REF_LOG2

    def start_plain():
        m_s[...] = jnp.zeros(m_s.shape, F32)
        acc_s[...] = jnp.zeros(acc_s.shape, F32)
        lead = A_GROUPS[0]
        together = lax.cond(i + 1 >= lead, lambda: all_fast(i, lead, 0.0), lambda: jnp.bool_(False))
        lax.cond(together, lambda: fast_group(i, lead, True), lambda: fast_group(i, 1, True))
        return i + 1 - jnp.where(together, lead, 1)

    def start_exact():
        exact_block(i, True)
        return i

    left = lax.cond(plain, start_plain, start_exact)

    def single(j, m_low):
        reach = reach_of(j, m_low)

        def visit():
            def fast():
                fast_group(j, 1)
                return m_low

            def exact():
                exact_block(j, False)
                return jnp.min(m_s[...])

            return lax.cond(reach <= FAST_LOG2, fast, exact)

        return lax.cond(reach < SKIP_LOG2, lambda: m_low, visit)

    def sweep(size, count, top, m_low):
        def body(n, m_low):
            j = top - size * n

            def group():
                fast_group(j, size)
                return m_low

            def one_by_one():
                return lax.fori_loop(0, size, lambda u, m: single(j - u, m), m_low)

            return lax.cond(all_fast(j, size, m_low), group, one_by_one)

        return lax.fori_loop(0, count, body, m_low)

    m_low = jnp.min(m_s[...])
    for size in A_GROUPS:
        m_low = sweep(size, left // size, left - 1, m_low)
        left = left % size
    lax.fori_loop(0, left, lambda u, m: single(left - 1 - u, m), m_low)

    lam = (jnp.exp(jnp.sum(lq1_ref[...] * lk1_ref[...], keepdims=True))
           - jnp.exp(jnp.sum(lq2_ref[...] * lk2_ref[...], keepdims=True)) + lam_init)
    outs = [acc_s[a, :A_V, :] * (1.0 / acc_s[a, A_V:A_V + 1, :]) for a in range(2)]
    o_ref[0, 0] = outs[0] - lam * outs[1]


def _diff_attn(qt, k, vt, k_norms, lq1, lk1, lq2, lk2, lam_init):
    b, s, _ = k.shape
    t = A_TILE
    nb = s // t
    slopes = jnp.asarray([_slopes_log2(hd)[0] for hd in range(A_HEADS)], F32)
    smem = pl.BlockSpec(memory_space=pltpu.SMEM)
    vec = pl.BlockSpec((1, A_QK), lambda bi, h, i: (0, 0))
    return pl.pallas_call(
        functools.partial(_diff_attn_kernel, t=t, nb=nb, lam_init=lam_init),
        grid=(b, A_HEADS, nb),
        in_specs=[
            smem, smem, vec, vec, vec, vec,
            pl.BlockSpec((1, 1, 1, LANES, t), lambda bi, h, i: (bi, h, i, 0, 0)),
            pl.BlockSpec((1, s, LANES), lambda bi, h, i: (bi, 0, h)),
            pl.BlockSpec((1, 1, nb, A_VT, t), lambda bi, h, i: (bi, h, 0, 0, 0)),
        ],
        out_specs=pl.BlockSpec((1, 1, A_V, t), lambda bi, h, i: (bi, h, 0, i)),
        out_shape=jax.ShapeDtypeStruct((b, A_HEADS, A_V, s), F32),
        scratch_shapes=[
            pltpu.VMEM((2, LANES, t), BF16),
            pltpu.VMEM((2, 1, t), F32),
            pltpu.VMEM((2, A_VT, t), F32),
        ],
        compiler_params=_params("parallel", "parallel", "arbitrary"),
        name="diff_attn",
    )(k_norms, slopes, lq1.reshape(1, A_QK), lk1.reshape(1, A_QK), lq2.reshape(1, A_QK),
      lk2.reshape(1, A_QK), qt, k, vt)


def _sb_attn_kernel(q_ref, k_ref, v_ref, o_ref, r_s, acc_s, *, tq):
    i = pl.program_id(2)
    q = q_ref[0]
    lane = lax.broadcasted_iota(jnp.int32, (tq, LANES), 1)
    qm = [jnp.where(lane < SB_DIM, q, jnp.zeros_like(q)),
          jnp.where(lane >= SB_DIM, q, jnp.zeros_like(q))]
    row = lax.broadcasted_iota(jnp.int32, (tq, tq), 0)
    col = lax.broadcasted_iota(jnp.int32, (tq, tq), 1)
    strict = col < row
    later = (row > col).astype(BF16)

    def keys(j):
        return k_ref[0, pl.ds(pl.multiple_of(j * tq, tq), tq), :]

    def values(j):
        return v_ref[0, pl.ds(pl.multiple_of(j * tq, tq), tq), :]

    def log_weights(kj, h, diagonal):
        z = _nt_dot(qm[h], kj)
        log_beta = jnp.minimum(z, 0.0) - jnp.log2(1.0 + jnp.exp2(-jnp.abs(z)))
        log_1mb = log_beta - z
        if diagonal:
            log_1mb = jnp.where(strict, log_1mb, 0.0)
        after = _dot(jnp.concatenate(_split_bf16(log_1mb, 2), axis=1), later2)
        return log_beta + after, jnp.sum(log_1mb, axis=1, keepdims=True)

    later2 = jnp.concatenate([later, later], axis=0)
    prev = jnp.maximum(i - 1, 0)
    has_prev = (i > 0).astype(F32)
    no_prev = jnp.where(i > 0, 0.0, NEG)
    k_diag, k_prev = keys(i), keys(prev)
    v_both = jnp.concatenate([values(i), values(prev)], axis=0)
    for h in range(2):
        lw_d, tot_d = log_weights(k_diag, h, True)
        lw_p, tot_p = log_weights(k_prev, h, False)
        a_d = jnp.where(strict, jnp.exp2(lw_d), 0.0)
        a_p = jnp.exp2(lw_p + (tot_d + no_prev))
        acc_s[h] = _dot(jnp.concatenate([a_d.astype(BF16), a_p.astype(BF16)], axis=1), v_both)
        r_s[h] = tot_d + tot_p * has_prev

    def cond(carry):
        j, live = carry
        return jnp.logical_and(j >= 0, live)

    def body(carry):
        j, _ = carry
        kj, vj = keys(j), values(j)
        for h in range(2):
            lw, tot = log_weights(kj, h, False)
            run = r_s[h]
            acc_s[h] = acc_s[h] + _dot(jnp.exp2(lw + run).astype(BF16), vj)
            r_s[h] = run + tot
        return j - 1, jnp.max(r_s[...]) > SKIP_LOG2

    lax.while_loop(cond, body, (i - 2, jnp.max(r_s[...]) > SKIP_LOG2))
    o_ref[0] = jnp.where(lane < SB_DIM, acc_s[0], acc_s[1])


def _sb_attn(q, k, v):
    b, s, _ = q.shape
    tq = min(ATT_TILE, s)
    pairs = SB_W // LANES
    return pl.pallas_call(
        functools.partial(_sb_attn_kernel, tq=tq),
        grid=(b, pairs, s // tq),
        in_specs=[
            pl.BlockSpec((1, tq, LANES), lambda bi, p, i: (bi, i, p)),
            pl.BlockSpec((1, s, LANES), lambda bi, p, i: (bi, 0, p)),
            pl.BlockSpec((1, s, LANES), lambda bi, p, i: (bi, 0, p)),
        ],
        out_specs=pl.BlockSpec((1, tq, LANES), lambda bi, p, i: (bi, i, p)),
        out_shape=jax.ShapeDtypeStruct((b, s, SB_W), F32),
        scratch_shapes=[
            pltpu.VMEM((2, tq, 1), F32),
            pltpu.VMEM((2, tq, LANES), F32),
        ],
        compiler_params=_params("parallel", "parallel", "arbitrary"),
        name="sb_attn",
    )(q, k, v)


def _swa_attn_kernel(sinks_ref, bias_ref, q_ref, kc_ref, kp_ref, vc_ref, vp_ref, o_ref, *, tq):
    i = pl.program_id(1)
    half = tq // 2
    col = lax.broadcasted_iota(jnp.int32, (half, tq), 1)
    has_key = col >= jnp.where(i > 0, 0, half)
    lane = lax.broadcasted_iota(jnp.int32, (half, LANES), 1)

    kwin = [jnp.concatenate([kp_ref[0, half:, :], kc_ref[0, :half, :]], axis=0), kc_ref[0]]
    vwin = [jnp.concatenate([vp_ref[0, half:, :], vc_ref[0, :half, :]], axis=0), vc_ref[0]]
    for hf in range(2):
        rows = slice(hf * half, (hf + 1) * half)
        for t in range(SWA_REP):
            q = q_ref[0, rows, t * LANES:(t + 1) * LANES]
            outs = []
            for g in range(SWA_KV_HEADS):
                head = g * SWA_REP + t
                qg = jnp.where((lane // SWA_DIM) == g, q, jnp.zeros_like(q))
                s = _nt_dot(qg, kwin[hf]) + bias_ref[head]
                if hf == 0:
                    s = jnp.where(has_key, s, NEG)
                sink = sinks_ref[head] * LOG2E
                m = jnp.maximum(jnp.max(s, axis=1, keepdims=True), sink)
                p = jnp.exp2(s - m)
                den = jnp.sum(p, axis=1, keepdims=True) + jnp.exp2(sink - m)
                outs.append(_dot(p.astype(BF16), vwin[hf]) / den)
            o_ref[0, rows, t * LANES:(t + 1) * LANES] = jnp.where(lane < SWA_DIM, outs[0], outs[1])


def _swa_bias(tq):
    half = tq // 2
    dist = np.arange(half)[:, None] + half - np.arange(tq)[None, :]
    in_window = (dist >= 0) & (dist < WINDOW)
    slopes = _alibi_slopes(SWA_Q_HEADS) * LOG2E
    bias = np.where(in_window[None], -slopes[:, None, None] * dist[None], NEG)
    return jnp.asarray(bias, F32)


def _swa_attn(q, k, v, sinks):
    b, s, _ = q.shape
    tq = min(ATT_TILE, s)
    smem = pl.BlockSpec(memory_space=pltpu.SMEM)
    cur = pl.BlockSpec((1, tq, SWA_KV_W), lambda bi, i: (bi, i, 0))
    prev = pl.BlockSpec((1, tq, SWA_KV_W), lambda bi, i: (bi, jnp.maximum(i - 1, 0), 0))
    return pl.pallas_call(
        functools.partial(_swa_attn_kernel, tq=tq),
        grid=(b, s // tq),
        in_specs=[
            smem,
            pl.BlockSpec((SWA_Q_HEADS, tq // 2, tq), lambda bi, i: (0, 0, 0)),
            pl.BlockSpec((1, tq, SWA_Q_W), lambda bi, i: (bi, i, 0)),
            cur, prev, cur, prev,
        ],
        out_specs=pl.BlockSpec((1, tq, SWA_Q_W), lambda bi, i: (bi, i, 0)),
        out_shape=jax.ShapeDtypeStruct((b, s, SWA_Q_W), F32),
        compiler_params=_params("parallel", "parallel"),
        name="swa_attn",
    )(sinks.astype(F32), _swa_bias(tq), q, k, k, v, v)


def _mix_ffn_kernel(ya_ref, yb_ref, yc_ref, x_ref, ga_ref, gb_ref, gc_ref, ln2_ref, mod_ref,
                    wa_ref, wb_ref, wc_ref, wg_ref, wu_ref, wd_ref, fg_ref, o_ref,
                    *, a_scale, final):
    mixed = _dot(_rms(yb_ref[0], gb_ref[...]).astype(BF16), wb_ref[...])
    mixed += _dot(_rms(yc_ref[0], gc_ref[...]).astype(BF16), wc_ref[...])
    packed = []
    for h in range(0, A_HEADS, 2):
        pair = []
        for yt in (ya_ref[0, h], ya_ref[0, h + 1]):
            ms = jnp.mean(yt * yt, axis=0, keepdims=True)
            pair.append(yt * lax.rsqrt(ms + EPS))
        packed.append(jnp.concatenate(pair, axis=0).T)
    na = jnp.concatenate(packed, axis=1) * ga_ref[...] * a_scale
    mixed += _dot(na.astype(BF16), wa_ref[...])
    x1 = x_ref[0] + mod_ref[0, 2:3, :] * mixed
    h2 = (_rms(x1, ln2_ref[...]) * (1.0 + mod_ref[0, 4:5, :]) + mod_ref[0, 3:4, :]).astype(BF16)
    gate = _dot(h2, wg_ref[...])
    up = _dot(h2, wu_ref[...])
    act = gate * jax.nn.sigmoid(gate) * up
    out = x1 + mod_ref[0, 5:6, :] * _dot(act.astype(BF16), wd_ref[...])
    if final:
        out = _rms(out, fg_ref[...])
    o_ref[0] = out


def _mix_ffn(ya, yb, yc, x, ga, gb, gc, ln2, mod, wa, wb, wc, wg, wu, wd, fg, a_scale, final):
    b, s, d = x.shape
    f = wg.shape[1]
    tm = min(FFN_TILE, s)
    row = lambda wd_: pl.BlockSpec((1, tm, wd_), lambda bi, i: (bi, i, 0))
    const = lambda r, cdim: pl.BlockSpec((r, cdim), lambda bi, i: (0, 0),
                                         pipeline_mode=pl.Buffered(1))
    return pl.pallas_call(
        functools.partial(_mix_ffn_kernel, a_scale=a_scale, final=final),
        grid=(b, s // tm),
        in_specs=[
            pl.BlockSpec((1, A_HEADS, A_V, tm), lambda bi, i: (bi, 0, 0, i)),
            row(SB_W), row(SWA_Q_W), row(d),
            const(1, A_W), const(1, SB_W), const(1, SWA_Q_W), const(1, d),
            pl.BlockSpec((1, N_MOD, d), lambda bi, i: (bi, 0, 0)),
            const(A_W, d), const(SB_W, d), const(SWA_Q_W, d),
            const(d, f), const(d, f), const(f, d), const(1, d),
        ],
        out_specs=row(d),
        out_shape=jax.ShapeDtypeStruct((b, s, d), F32),
        compiler_params=_params("parallel", "parallel"),
        name="mix_ffn",
    )(ya, yb, yc, x, ga, gb, gc, ln2.reshape(1, d), mod, wa, wb, wc, wg, wu, wd, fg.reshape(1, d))


def _swa_perm():
    cols = []
    for t in range(SWA_REP):
        for g in range(SWA_KV_HEADS):
            head = g * SWA_REP + t
            cols.extend(range(head * SWA_DIM, (head + 1) * SWA_DIM))
    return jnp.asarray(cols, jnp.int32)


def _prep_w_in(w):
    bounds = np.cumsum((0, A_W, A_W, A_W, SB_W, SB_W, SB_W, SWA_Q_W, SWA_KV_W, SWA_KV_W))
    qa, ka, va, qb, kb, vb, qc, kc, vc = [w[:, bounds[n]:bounds[n + 1]] for n in range(9)]
    row = jnp.concatenate([
        ka, qb * (SB_DIM ** -0.5 * LOG2E), kb, vb,
        (qc * (SWA_DIM ** -0.5 * LOG2E))[:, _swa_perm()], kc, vc], axis=1)
    transposed = jnp.concatenate([
        qa * (A_QK ** -0.5 * LOG2E), va], axis=1).T
    return row.astype(BF16), transposed.astype(BF16)


def kernel(x, c, ln1_g, ln2_g, w_mod, b_mod, w_in, lam_q1, lam_k1, lam_q2, lam_k2, diff_norm_g,
           sb_norm_g, swa_norm_g, swa_sinks, w_out, w_gate, w_up, w_down, final_g):
    depth = w_in.shape[0]
    perm = _swa_perm()
    mod = _modulation(c, w_mod, b_mod)
    for l in range(depth):
        lam_init = 0.8 - 0.6 * math.exp(-0.3 * l)
        w_row, w_t = _prep_w_in(w_in[l])
        ka, qb, kb, vb, qc, kc, vc, qta, vta, kn = _in_proj(x, ln1_g[l], mod[l], w_row, w_t)
        k_norms = kn[:, :, 0, :A_HEADS].transpose(0, 2, 1).reshape(-1)
        ya = _diff_attn(qta, ka, vta, k_norms, lam_q1[l], lam_k1[l], lam_q2[l], lam_k2[l], lam_init)
        yb = _sb_attn(qb, kb, vb)
        yc = _swa_attn(qc, kc, vc, swa_sinks[l])
        wo = w_out[l].astype(BF16)
        x = _mix_ffn(
            ya, yb, yc, x,
            jnp.tile(diff_norm_g[l], A_HEADS).reshape(1, A_W),
            sb_norm_g[l].reshape(1, SB_W),
            swa_norm_g[l][perm].reshape(1, SWA_Q_W),
            ln2_g[l], mod[l],
            wo[:A_W], wo[A_W:A_W + SB_W], wo[A_W + SB_W:][perm],
            w_gate[l].astype(BF16), w_up[l].astype(BF16), w_down[l].astype(BF16), final_g,
            1.0 - lam_init, l == depth - 1)
    return x
```

```python
import functools
import math

import numpy as np
import jax
import jax.numpy as jnp
from jax import lax
from jax.experimental import pallas as pl
from jax.experimental.pallas import tpu as pltpu

F32 = jnp.float32
BF16 = jnp.bfloat16

N_MOD = 6
EPS = 1e-6
A_HEADS = 4
A_QK = 32
A_V = 64
SB_HEADS = 4
SB_DIM = 64
SWA_Q_HEADS = 8
SWA_KV_HEADS = 2
SWA_REP = 4
SWA_DIM = 64
WINDOW = 128
A_W = 256
SB_W = 256
SWA_Q_W = 512
SWA_KV_W = 128

LANES = 128
NEG = -1e30

VMEM_LIMIT = 56 * 1024 * 1024

ROW_TILE = 512
FFN_TILE = 512
ATT_TILE = 256
A_TILE = ROW_TILE

LOG2E = math.log2(math.e)
A_AUG = 2 * A_QK
A_REF = A_AUG + 16
A_PIECES = 4
A_VT = 80
A_KW = A_HEADS * LANES
A_TW = A_HEADS * (LANES + A_VT)
SKIP_LOG2 = -160.0
FAST_LOG2 = 80.0
A_GROUPS = (4, 2)
ZERO_REF_LOG2 = 60.0
BOUND_MARGIN = 1.0


def _params(*sem):
    return pltpu.CompilerParams(dimension_semantics=sem, vmem_limit_bytes=VMEM_LIMIT)


def _nt_dot(a, b):
    return lax.dot_general(a, b, (((1,), (1,)), ((), ())), preferred_element_type=F32)


def _dot(a, b):
    return jnp.dot(a, b, preferred_element_type=F32)


def _rms(x, g):
    ms = jnp.mean(x * x, axis=-1, keepdims=True)
    return x * lax.rsqrt(ms + EPS) * g


def _alibi_slopes(n):
    return 2.0 ** (-8.0 * np.arange(1, n + 1, dtype=np.float64) / n)


def _split_bf16(x, n):
    pieces = []
    for _ in range(n - 1):
        p = x.astype(BF16)
        pieces.append(p)
        x = x - p.astype(F32)
    pieces.append(x.astype(BF16))
    return pieces


def _mod_kernel(c_ref, w_ref, b_ref, o_ref):
    cv = c_ref[...]
    ca = cv * jax.nn.sigmoid(cv)
    o_ref[0] = jnp.dot(ca, w_ref[0], preferred_element_type=F32,
                       precision=lax.Precision.HIGHEST) + b_ref[0]


def _modulation(c, w_mod, b_mod):
    depth, d, n = w_mod.shape
    b = c.shape[0]
    rows = 8
    cp = jnp.zeros((rows, d), F32).at[:b].set(c)
    tn = 1024
    out = pl.pallas_call(
        _mod_kernel,
        grid=(depth, n // tn),
        in_specs=[
            pl.BlockSpec((rows, d), lambda l, j: (0, 0)),
            pl.BlockSpec((1, d, tn), lambda l, j: (l, 0, j)),
            pl.BlockSpec((1, 1, tn), lambda l, j: (l, 0, j)),
        ],
        out_specs=pl.BlockSpec((1, rows, tn), lambda l, j: (l, 0, j)),
        out_shape=jax.ShapeDtypeStruct((depth, rows, n), F32),
        compiler_params=_params("parallel", "parallel"),
        name="modulation",
    )(cp, w_mod, b_mod.reshape(depth, 1, n))
    return out[:, :b].reshape(depth, b, N_MOD, d)


ROW_SPLITS = (A_KW, SB_W, SB_W, SB_W, SWA_Q_W, SWA_KV_W, SWA_KV_W)


def _in_proj_kernel(x_ref, g_ref, mod_ref, w_ref, wt_ref, kaug_ref, taug_ref, *out_refs):
    row_refs, (qt_ref, vt_ref, kn_ref) = out_refs[:len(ROW_SPLITS)], out_refs[len(ROW_SPLITS):]
    x = x_ref[0]
    h = _rms(x, g_ref[...]) * (1.0 + mod_ref[0, 1:2, :]) + mod_ref[0, 0:1, :]
    hb = h.astype(BF16)
    proj = _dot(hb, w_ref[...])
    start = A_W
    for ref, width in zip(row_refs[1:], ROW_SPLITS[1:]):
        ref[0] = proj[:, start:start + width].astype(BF16)
        start += width
    feat = lax.broadcasted_iota(jnp.int32, (x.shape[0], LANES), 1)
    out_lane = lax.broadcasted_iota(jnp.int32, kn_ref.shape[2:], 1)
    norms = jnp.zeros(kn_ref.shape[2:], F32)
    for hd in range(A_HEADS):
        pair = proj[:, (hd // 2) * LANES:(hd // 2 + 1) * LANES]
        if hd % 2:
            pair = pltpu.roll(pair, A_AUG, 1)
        cols = slice(hd * LANES, (hd + 1) * LANES)
        keys = (jnp.where(feat < A_AUG, pair, 0.0) + kaug_ref[:, cols]).astype(BF16)
        row_refs[0][0, :, cols] = keys
        kf = keys.astype(F32)
        sq = jnp.sum(jnp.where(feat < A_AUG, kf * kf, 0.0), axis=1, keepdims=True)
        norms = jnp.where(out_lane == hd, jnp.sqrt(jnp.max(sq, axis=0, keepdims=True)), norms)
    kn_ref[0, 0] = norms
    proj_t = _nt_dot(wt_ref[...], hb)
    for hd in range(A_HEADS):
        q_rows = proj_t[hd * A_AUG:(hd + 1) * A_AUG]
        q_const = taug_ref[hd * LANES + A_AUG:(hd + 1) * LANES, :]
        qt_ref[0, hd, 0] = jnp.concatenate([q_rows, q_const], axis=0).astype(BF16)
        v_rows = proj_t[A_W + hd * A_V:A_W + (hd + 1) * A_V]
        v_const = taug_ref[A_KW + hd * A_VT + A_V:A_KW + (hd + 1) * A_VT, :]
        vt_ref[0, hd, 0] = jnp.concatenate([v_rows, v_const], axis=0).astype(BF16)


def _bf16_pieces(value, n):
    pieces = []
    rest = float(value)
    for _ in range(n):
        p = float(np.asarray(rest, np.float32).astype(jnp.bfloat16).astype(np.float64))
        pieces.append(p)
        rest -= p
    return pieces


def _slopes_log2(hd):
    pieces = _bf16_pieces(_alibi_slopes(A_HEADS)[hd] * LOG2E, A_PIECES)
    return sum(pieces), pieces


def _alibi_constants(t):
    idx = np.arange(t)
    lo, hi = idx % 256, idx - idx % 256
    kaug = np.zeros((t, A_KW), np.float32)
    taug = np.zeros((A_TW, t), np.float32)
    for hd in range(A_HEADS):
        _, pieces = _slopes_log2(hd)
        k0 = hd * LANES + A_AUG
        for n, piece in enumerate(pieces):
            taug[k0 + n] = -lo
            taug[k0 + A_PIECES + n] = -hi
            kaug[:, k0 + n] = piece
            kaug[:, k0 + A_PIECES + n] = piece
            taug[k0 + 2 * A_PIECES + n] = piece
            taug[k0 + 3 * A_PIECES + n] = piece
            kaug[:, k0 + 2 * A_PIECES + n] = lo
            kaug[:, k0 + 3 * A_PIECES + n] = hi
        r0 = hd * LANES + A_REF
        kaug[:, r0:r0 + 3] = 1.0
        taug[A_KW + hd * A_VT + A_V] = 1.0
    return jnp.asarray(kaug), jnp.asarray(taug)


def _in_proj(x, g, mod, w, wt):
    b, s, d = x.shape
    tm = A_TILE
    nb = s // tm
    n = w.shape[1]
    kaug, taug = _alibi_constants(tm)
    const = lambda shape: pl.BlockSpec(shape, lambda bi, i: (0,) * len(shape))
    return pl.pallas_call(
        _in_proj_kernel,
        grid=(b, nb),
        in_specs=[
            pl.BlockSpec((1, tm, d), lambda bi, i: (bi, i, 0)),
            const((1, d)),
            pl.BlockSpec((1, N_MOD, d), lambda bi, i: (bi, 0, 0)),
            const((d, n)), const(wt.shape), const((tm, A_KW)), const((A_TW, tm)),
        ],
        out_specs=[pl.BlockSpec((1, tm, wd), lambda bi, i: (bi, i, 0)) for wd in ROW_SPLITS] + [
            pl.BlockSpec((1, A_HEADS, 1, LANES, tm), lambda bi, i: (bi, 0, i, 0, 0)),
            pl.BlockSpec((1, A_HEADS, 1, A_VT, tm), lambda bi, i: (bi, 0, i, 0, 0)),
            pl.BlockSpec((1, 1, 8, LANES), lambda bi, i: (bi, i, 0, 0)),
        ],
        out_shape=[jax.ShapeDtypeStruct((b, s, wd), BF16) for wd in ROW_SPLITS] + [
            jax.ShapeDtypeStruct((b, A_HEADS, nb, LANES, tm), BF16),
            jax.ShapeDtypeStruct((b, A_HEADS, nb, A_VT, tm), BF16),
            jax.ShapeDtypeStruct((b, nb, 8, LANES), F32),
        ],
        compiler_params=_params("parallel", "parallel"),
        name="in_proj",
    )(x, g.reshape(1, d), mod, w, wt, kaug, taug)


def _diff_attn_kernel(kn_ref, slopes_ref, lq1_ref, lk1_ref, lq2_ref, lk2_ref, qt_ref, k_ref,
                      vt_ref, o_ref, q_s, m_s, acc_s, *, t, nb, lam_init):
    bi = pl.program_id(0)
    hd = pl.program_id(1)
    i = pl.program_id(2)
    slope = slopes_ref[hd]
    qt = qt_ref[0, 0, 0]
    feat = lax.broadcasted_iota(jnp.int32, (LANES, t), 0)
    zero = jnp.zeros_like(qt)
    q_s[0] = jnp.where(jnp.logical_or(feat < A_QK, feat >= A_AUG), qt, zero)
    q_s[1] = jnp.where(feat >= A_QK, qt, zero)
    qsq = qt.astype(F32) * qt.astype(F32)
    n0 = jnp.sum(jnp.where(feat < A_QK, qsq, 0.0), axis=0, keepdims=True)
    n1 = jnp.sum(jnp.where(jnp.logical_and(feat >= A_QK, feat < A_AUG), qsq, 0.0),
                 axis=0, keepdims=True)
    q_norm = jnp.max(jnp.sqrt(jnp.maximum(n0, n1)))
    key_minus_query = (lax.broadcasted_iota(jnp.int32, (t, t), 0)
                       - lax.broadcasted_iota(jnp.int32, (t, t), 1))
    causal = key_minus_query <= 0

    def reference_rows(j, a):
        off = slope * ((i - j) * t).astype(F32)
        row = lax.broadcasted_iota(jnp.int32, (16, t), 0)
        hi, mid, lo = [p.astype(F32) for p in _split_bf16(-(m_s[a] + off), 3)]
        tile = jnp.where(row == 0, hi, jnp.where(row == 1, mid, jnp.where(row == 2, lo, 0.0)))
        return tile.astype(BF16)

    def scores(j, a, ref=None):
        kj = k_ref[0, pl.ds(pl.multiple_of(j * t, t), t), :]
        qa = q_s[a]
        if ref is not None:
            qa = jnp.concatenate([qa[:A_REF], ref, qa[A_REF + 16:]], axis=0)
        return _dot(kj, qa)

    def exact_block(j, first):
        vtj = vt_ref[0, 0, j]
        for a in range(2):
            s = scores(j, a, None if first else reference_rows(j, a))
            if first:
                s = jnp.where(causal, s, NEG)
            top = jnp.max(s, axis=0, keepdims=True)
            shift = top if first else jnp.maximum(top, 0.0)
            pv = _dot(vtj, jnp.exp2(s - shift).astype(BF16))
            if first:
                acc_s[a] = pv
                m_s[a] = shift
            else:
                acc_s[a] = jnp.exp2(-shift) * acc_s[a] + pv
                m_s[a] = m_s[a] + shift

    def fast_weights(j, a, diagonal):
        s = scores(j, a, reference_rows(j, a))
        if diagonal:
            s = jnp.where(causal, s, NEG)
        return jnp.exp2(s).astype(BF16)

    def fast_group(j, size, from_diagonal=False):
        vts = jnp.concatenate([vt_ref[0, 0, j - u] for u in range(size)], axis=1)
        for a in range(2):
            ps = jnp.concatenate(
                [fast_weights(j - u, a, from_diagonal and u == 0) for u in range(size)], axis=0)
            acc_s[a] = acc_s[a] + _dot(vts, ps)

    def reach_of(j, m_low):
        k_norm = kn_ref[(bi * A_HEADS + hd) * nb + j]
        alibi = jnp.where(j == i, 0.0, slope * (t - (i - j) * t).astype(F32))
        return (q_norm * k_norm + alibi + BOUND_MARGIN) - m_low

    def all_fast(j, size, m_low):
        ok = jnp.bool_(True)
        for u in range(size):
            reach = reach_of(j - u, m_low)
            ok = jnp.logical_and(ok, jnp.logical_and(reach >= SKIP_LOG2, reach <= FAST_LOG2))
        return ok

    plain = q_norm * kn_ref[(bi * A_HEADS + hd) * nb + i] + BOUND_MARGIN <= ZERO_REF_LOG2

    def start_plain():
        m_s[...] = jnp.zeros(m_s.shape, F32)
        acc_s[...] = jnp.zeros(acc_s.shape, F32)
        lead = A_GROUPS[0]
        together = lax.cond(i + 1 >= lead, lambda: all_fast(i, lead, 0.0), lambda: jnp.bool_(False))
        lax.cond(together, lambda: fast_group(i, lead, True), lambda: fast_group(i, 1, True))
        return i + 1 - jnp.where(together, lead, 1)

    def start_exact():
        exact_block(i, True)
        return i

    left = lax.cond(plain, start_plain, start_exact)

    def single(j, m_low):
        reach = reach_of(j, m_low)

        def visit():
            def fast():
                fast_group(j, 1)
                return m_low

            def exact():
                exact_block(j, False)
                return jnp.min(m_s[...])

            return lax.cond(reach <= FAST_LOG2, fast, exact)

        return lax.cond(reach < SKIP_LOG2, lambda: m_low, visit)

    def grouped(size, left, m_low):
        def more(left):
            return lax.cond(left >= size, lambda: all_fast(left - 1, size, m_low),
                            lambda: jnp.bool_(False))

        def body(left):
            fast_group(left - 1, size)
            return left - size

        return lax.while_loop(more, body, left)

    m_low = jnp.min(m_s[...])
    for size in A_GROUPS:
        left = grouped(size, left, m_low)
    lax.fori_loop(0, left, lambda u, m: single(left - 1 - u, m), m_low)

    lam = (jnp.exp(jnp.sum(lq1_ref[...] * lk1_ref[...], keepdims=True))
           - jnp.exp(jnp.sum(lq2_ref[...] * lk2_ref[...], keepdims=True)) + lam_init)
    outs = [acc_s[a, :A_V, :] * (1.0 / acc_s[a, A_V:A_V + 1, :]) for a in range(2)]
    o_ref[0, 0] = outs[0] - lam * outs[1]


def _diff_attn(qt, k, vt, k_norms, lq1, lk1, lq2, lk2, lam_init):
    b, s, _ = k.shape
    t = A_TILE
    nb = s // t
    slopes = jnp.asarray([_slopes_log2(hd)[0] for hd in range(A_HEADS)], F32)
    smem = pl.BlockSpec(memory_space=pltpu.SMEM)
    vec = pl.BlockSpec((1, A_QK), lambda bi, h, i: (0, 0))
    return pl.pallas_call(
        functools.partial(_diff_attn_kernel, t=t, nb=nb, lam_init=lam_init),
        grid=(b, A_HEADS, nb),
        in_specs=[
            smem, smem, vec, vec, vec, vec,
            pl.BlockSpec((1, 1, 1, LANES, t), lambda bi, h, i: (bi, h, i, 0, 0)),
            pl.BlockSpec((1, s, LANES), lambda bi, h, i: (bi, 0, h)),
            pl.BlockSpec((1, 1, nb, A_VT, t), lambda bi, h, i: (bi, h, 0, 0, 0)),
        ],
        out_specs=pl.BlockSpec((1, 1, A_V, t), lambda bi, h, i: (bi, h, 0, i)),
        out_shape=jax.ShapeDtypeStruct((b, A_HEADS, A_V, s), F32),
        scratch_shapes=[
            pltpu.VMEM((2, LANES, t), BF16),
            pltpu.VMEM((2, 1, t), F32),
            pltpu.VMEM((2, A_VT, t), F32),
        ],
        compiler_params=_params("parallel", "parallel", "arbitrary"),
        name="diff_attn",
    )(k_norms, slopes, lq1.reshape(1, A_QK), lk1.reshape(1, A_QK), lq2.reshape(1, A_QK),
      lk2.reshape(1, A_QK), qt, k, vt)


def _sb_attn_kernel(q_ref, k_ref, v_ref, o_ref, r_s, acc_s, *, tq):
    i = pl.program_id(2)
    q = q_ref[0]
    lane = lax.broadcasted_iota(jnp.int32, (tq, LANES), 1)
    qm = [jnp.where(lane < SB_DIM, q, jnp.zeros_like(q)),
          jnp.where(lane >= SB_DIM, q, jnp.zeros_like(q))]
    row = lax.broadcasted_iota(jnp.int32, (tq, tq), 0)
    col = lax.broadcasted_iota(jnp.int32, (tq, tq), 1)
    strict = col < row
    later = (row > col).astype(BF16)

    def keys(j):
        return k_ref[0, pl.ds(pl.multiple_of(j * tq, tq), tq), :]

    def values(j):
        return v_ref[0, pl.ds(pl.multiple_of(j * tq, tq), tq), :]

    def log_weights(kj, h, diagonal):
        z = _nt_dot(qm[h], kj)
        log_beta = jnp.minimum(z, 0.0) - jnp.log2(1.0 + jnp.exp2(-jnp.abs(z)))
        log_1mb = log_beta - z
        if diagonal:
            log_1mb = jnp.where(strict, log_1mb, 0.0)
        after = _dot(jnp.concatenate(_split_bf16(log_1mb, 2), axis=1), later2)
        return log_beta + after, jnp.sum(log_1mb, axis=1, keepdims=True)

    later2 = jnp.concatenate([later, later], axis=0)
    prev = jnp.maximum(i - 1, 0)
    has_prev = (i > 0).astype(F32)
    no_prev = jnp.where(i > 0, 0.0, NEG)
    k_diag, k_prev = keys(i), keys(prev)
    v_both = jnp.concatenate([values(i), values(prev)], axis=0)
    for h in range(2):
        lw_d, tot_d = log_weights(k_diag, h, True)
        lw_p, tot_p = log_weights(k_prev, h, False)
        a_d = jnp.where(strict, jnp.exp2(lw_d), 0.0)
        a_p = jnp.exp2(lw_p + (tot_d + no_prev))
        acc_s[h] = _dot(jnp.concatenate([a_d.astype(BF16), a_p.astype(BF16)], axis=1), v_both)
        r_s[h] = tot_d + tot_p * has_prev

    def cond(carry):
        j, live = carry
        return jnp.logical_and(j >= 0, live)

    def body(carry):
        j, _ = carry
        kj, vj = keys(j), values(j)
        for h in range(2):
            lw, tot = log_weights(kj, h, False)
            run = r_s[h]
            acc_s[h] = acc_s[h] + _dot(jnp.exp2(lw + run).astype(BF16), vj)
            r_s[h] = run + tot
        return j - 1, jnp.max(r_s[...]) > SKIP_LOG2

    lax.while_loop(cond, body, (i - 2, jnp.max(r_s[...]) > SKIP_LOG2))
    o_ref[0] = jnp.where(lane < SB_DIM, acc_s[0], acc_s[1])


def _sb_attn(q, k, v):
    b, s, _ = q.shape
    tq = min(ATT_TILE, s)
    pairs = SB_W // LANES
    return pl.pallas_call(
        functools.partial(_sb_attn_kernel, tq=tq),
        grid=(b, pairs, s // tq),
        in_specs=[
            pl.BlockSpec((1, tq, LANES), lambda bi, p, i: (bi, i, p)),
            pl.BlockSpec((1, s, LANES), lambda bi, p, i: (bi, 0, p)),
            pl.BlockSpec((1, s, LANES), lambda bi, p, i: (bi, 0, p)),
        ],
        out_specs=pl.BlockSpec((1, tq, LANES), lambda bi, p, i: (bi, i, p)),
        out_shape=jax.ShapeDtypeStruct((b, s, SB_W), F32),
        scratch_shapes=[
            pltpu.VMEM((2, tq, 1), F32),
            pltpu.VMEM((2, tq, LANES), F32),
        ],
        compiler_params=_params("parallel", "parallel", "arbitrary"),
        name="sb_attn",
    )(q, k, v)


def _swa_attn_kernel(sinks_ref, bias_ref, q_ref, kc_ref, kp_ref, vc_ref, vp_ref, o_ref, *, tq):
    i = pl.program_id(1)
    half = tq // 2
    col = lax.broadcasted_iota(jnp.int32, (half, tq), 1)
    has_key = col >= jnp.where(i > 0, 0, half)
    lane = lax.broadcasted_iota(jnp.int32, (half, LANES), 1)

    kwin = [jnp.concatenate([kp_ref[0, half:, :], kc_ref[0, :half, :]], axis=0), kc_ref[0]]
    vwin = [jnp.concatenate([vp_ref[0, half:, :], vc_ref[0, :half, :]], axis=0), vc_ref[0]]
    for hf in range(2):
        rows = slice(hf * half, (hf + 1) * half)
        for t in range(SWA_REP):
            q = q_ref[0, rows, t * LANES:(t + 1) * LANES]
            outs = []
            for g in range(SWA_KV_HEADS):
                head = g * SWA_REP + t
                qg = jnp.where((lane // SWA_DIM) == g, q, jnp.zeros_like(q))
                s = _nt_dot(qg, kwin[hf]) + bias_ref[head]
                if hf == 0:
                    s = jnp.where(has_key, s, NEG)
                sink = sinks_ref[head] * LOG2E
                m = jnp.maximum(jnp.max(s, axis=1, keepdims=True), sink)
                p = jnp.exp2(s - m)
                den = jnp.sum(p, axis=1, keepdims=True) + jnp.exp2(sink - m)
                outs.append(_dot(p.astype(BF16), vwin[hf]) / den)
            o_ref[0, rows, t * LANES:(t + 1) * LANES] = jnp.where(lane < SWA_DIM, outs[0], outs[1])


def _swa_bias(tq):
    half = tq // 2
    dist = np.arange(half)[:, None] + half - np.arange(tq)[None, :]
    in_window = (dist >= 0) & (dist < WINDOW)
    slopes = _alibi_slopes(SWA_Q_HEADS) * LOG2E
    bias = np.where(in_window[None], -slopes[:, None, None] * dist[None], NEG)
    return jnp.asarray(bias, F32)


def _swa_attn(q, k, v, sinks):
    b, s, _ = q.shape
    tq = min(ATT_TILE, s)
    smem = pl.BlockSpec(memory_space=pltpu.SMEM)
    cur = pl.BlockSpec((1, tq, SWA_KV_W), lambda bi, i: (bi, i, 0))
    prev = pl.BlockSpec((1, tq, SWA_KV_W), lambda bi, i: (bi, jnp.maximum(i - 1, 0), 0))
    return pl.pallas_call(
        functools.partial(_swa_attn_kernel, tq=tq),
        grid=(b, s // tq),
        in_specs=[
            smem,
            pl.BlockSpec((SWA_Q_HEADS, tq // 2, tq), lambda bi, i: (0, 0, 0)),
            pl.BlockSpec((1, tq, SWA_Q_W), lambda bi, i: (bi, i, 0)),
            cur, prev, cur, prev,
        ],
        out_specs=pl.BlockSpec((1, tq, SWA_Q_W), lambda bi, i: (bi, i, 0)),
        out_shape=jax.ShapeDtypeStruct((b, s, SWA_Q_W), F32),
        compiler_params=_params("parallel", "parallel"),
        name="swa_attn",
    )(sinks.astype(F32), _swa_bias(tq), q, k, k, v, v)


def _mix_ffn_kernel(ya_ref, yb_ref, yc_ref, x_ref, ga_ref, gb_ref, gc_ref, ln2_ref, mod_ref,
                    wa_ref, wb_ref, wc_ref, wg_ref, wu_ref, wd_ref, fg_ref, o_ref,
                    *, a_scale, final):
    mixed = _dot(_rms(yb_ref[0], gb_ref[...]).astype(BF16), wb_ref[...])
    mixed += _dot(_rms(yc_ref[0], gc_ref[...]).astype(BF16), wc_ref[...])
    packed = []
    for h in range(0, A_HEADS, 2):
        pair = []
        for yt in (ya_ref[0, h], ya_ref[0, h + 1]):
            ms = jnp.mean(yt * yt, axis=0, keepdims=True)
            pair.append(yt * lax.rsqrt(ms + EPS))
        packed.append(jnp.concatenate(pair, axis=0).T)
    na = jnp.concatenate(packed, axis=1) * ga_ref[...] * a_scale
    mixed += _dot(na.astype(BF16), wa_ref[...])
    x1 = x_ref[0] + mod_ref[0, 2:3, :] * mixed
    h2 = (_rms(x1, ln2_ref[...]) * (1.0 + mod_ref[0, 4:5, :]) + mod_ref[0, 3:4, :]).astype(BF16)
    gate = _dot(h2, wg_ref[...])
    up = _dot(h2, wu_ref[...])
    act = gate * jax.nn.sigmoid(gate) * up
    out = x1 + mod_ref[0, 5:6, :] * _dot(act.astype(BF16), wd_ref[...])
    if final:
        out = _rms(out, fg_ref[...])
    o_ref[0] = out


def _mix_ffn(ya, yb, yc, x, ga, gb, gc, ln2, mod, wa, wb, wc, wg, wu, wd, fg, a_scale, final):
    b, s, d = x.shape
    f = wg.shape[1]
    tm = min(FFN_TILE, s)
    row = lambda wd_: pl.BlockSpec((1, tm, wd_), lambda bi, i: (bi, i, 0))
    const = lambda r, cdim: pl.BlockSpec((r, cdim), lambda bi, i: (0, 0),
                                         pipeline_mode=pl.Buffered(1))
    return pl.pallas_call(
        functools.partial(_mix_ffn_kernel, a_scale=a_scale, final=final),
        grid=(b, s // tm),
        in_specs=[
            pl.BlockSpec((1, A_HEADS, A_V, tm), lambda bi, i: (bi, 0, 0, i)),
            row(SB_W), row(SWA_Q_W), row(d),
            const(1, A_W), const(1, SB_W), const(1, SWA_Q_W), const(1, d),
            pl.BlockSpec((1, N_MOD, d), lambda bi, i: (bi, 0, 0)),
            const(A_W, d), const(SB_W, d), const(SWA_Q_W, d),
            const(d, f), const(d, f), const(f, d), const(1, d),
        ],
        out_specs=row(d),
        out_shape=jax.ShapeDtypeStruct((b, s, d), F32),
        compiler_params=_params("parallel", "parallel"),
        name="mix_ffn",
    )(ya, yb, yc, x, ga, gb, gc, ln2.reshape(1, d), mod, wa, wb, wc, wg, wu, wd, fg.reshape(1, d))


def _swa_perm():
    cols = []
    for t in range(SWA_REP):
        for g in range(SWA_KV_HEADS):
            head = g * SWA_REP + t
            cols.extend(range(head * SWA_DIM, (head + 1) * SWA_DIM))
    return jnp.asarray(cols, jnp.int32)


def _prep_w_in(w):
    bounds = np.cumsum((0, A_W, A_W, A_W, SB_W, SB_W, SB_W, SWA_Q_W, SWA_KV_W, SWA_KV_W))
    qa, ka, va, qb, kb, vb, qc, kc, vc = [w[:, bounds[n]:bounds[n + 1]] for n in range(9)]
    row = jnp.concatenate([
        ka, qb * (SB_DIM ** -0.5 * LOG2E), kb, vb,
        (qc * (SWA_DIM ** -0.5 * LOG2E))[:, _swa_perm()], kc, vc], axis=1)
    transposed = jnp.concatenate([
        qa * (A_QK ** -0.5 * LOG2E), va], axis=1).T
    return row.astype(BF16), transposed.astype(BF16)


def kernel(x, c, ln1_g, ln2_g, w_mod, b_mod, w_in, lam_q1, lam_k1, lam_q2, lam_k2, diff_norm_g,
           sb_norm_g, swa_norm_g, swa_sinks, w_out, w_gate, w_up, w_down, final_g):
    depth = w_in.shape[0]
    perm = _swa_perm()
    mod = _modulation(c, w_mod, b_mod)
    for l in range(depth):
        lam_init = 0.8 - 0.6 * math.exp(-0.3 * l)
        w_row, w_t = _prep_w_in(w_in[l])
        ka, qb, kb, vb, qc, kc, vc, qta, vta, kn = _in_proj(x, ln1_g[l], mod[l], w_row, w_t)
        k_norms = kn[:, :, 0, :A_HEADS].transpose(0, 2, 1).reshape(-1)
        ya = _diff_attn(qta, ka, vta, k_norms, lam_q1[l], lam_k1[l], lam_q2[l], lam_k2[l], lam_init)
        yb = _sb_attn(qb, kb, vb)
        yc = _swa_attn(qc, kc, vc, swa_sinks[l])
        wo = w_out[l].astype(BF16)
        x = _mix_ffn(
            ya, yb, yc, x,
            jnp.tile(diff_norm_g[l], A_HEADS).reshape(1, A_W),
            sb_norm_g[l].reshape(1, SB_W),
            swa_norm_g[l][perm].reshape(1, SWA_Q_W),
            ln2_g[l], mod[l],
            wo[:A_W], wo[A_W:A_W + SB_W], wo[A_W + SB_W:][perm],
            w_gate[l].astype(BF16), w_up[l].astype(BF16), w_down[l].astype(BF16), final_g,
            1.0 - lam_init, l == depth - 1)
    return x
```

```python
import functools
import math

import numpy as np
import jax
import jax.numpy as jnp
from jax import lax
from jax.experimental import pallas as pl
from jax.experimental.pallas import tpu as pltpu

F32 = jnp.float32
BF16 = jnp.bfloat16

N_MOD = 6
EPS = 1e-6
A_HEADS = 4
A_QK = 32
A_V = 64
SB_HEADS = 4
SB_DIM = 64
SWA_Q_HEADS = 8
SWA_KV_HEADS = 2
SWA_REP = 4
SWA_DIM = 64
WINDOW = 128
A_W = 256
SB_W = 256
SWA_Q_W = 512
SWA_KV_W = 128

LANES = 128
NEG = -1e30

VMEM_LIMIT = 56 * 1024 * 1024

ROW_TILE = 512
FFN_TILE = 512
ATT_TILE = 256
A_TILE = ROW_TILE

LOG2E = math.log2(math.e)
A_AUG = 2 * A_QK
A_REF = A_AUG + 16
A_PIECES = 4
A_VT = 80
A_KW = A_HEADS * LANES
A_TW = A_HEADS * (LANES + A_VT)
SKIP_LOG2 = -160.0
FAST_LOG2 = 80.0
A_GROUPS = (4, 2)
ZERO_REF_LOG2 = 60.0
BOUND_MARGIN = 1.0


def _params(*sem):
    return pltpu.CompilerParams(dimension_semantics=sem, vmem_limit_bytes=VMEM_LIMIT)


def _nt_dot(a, b):
    return lax.dot_general(a, b, (((1,), (1,)), ((), ())), preferred_element_type=F32)


def _dot(a, b):
    return jnp.dot(a, b, preferred_element_type=F32)


def _rms(x, g):
    ms = jnp.mean(x * x, axis=-1, keepdims=True)
    return x * lax.rsqrt(ms + EPS) * g


def _alibi_slopes(n):
    return 2.0 ** (-8.0 * np.arange(1, n + 1, dtype=np.float64) / n)


def _split_bf16(x, n):
    pieces = []
    for _ in range(n - 1):
        p = x.astype(BF16)
        pieces.append(p)
        x = x - p.astype(F32)
    pieces.append(x.astype(BF16))
    return pieces


def _mod_kernel(c_ref, w_ref, b_ref, o_ref):
    cv = c_ref[...]
    ca = cv * jax.nn.sigmoid(cv)
    o_ref[0] = jnp.dot(ca, w_ref[0], preferred_element_type=F32,
                       precision=lax.Precision.HIGHEST) + b_ref[0]


def _modulation(c, w_mod, b_mod):
    depth, d, n = w_mod.shape
    b = c.shape[0]
    rows = 8
    cp = jnp.zeros((rows, d), F32).at[:b].set(c)
    tn = 1024
    out = pl.pallas_call(
        _mod_kernel,
        grid=(depth, n // tn),
        in_specs=[
            pl.BlockSpec((rows, d), lambda l, j: (0, 0)),
            pl.BlockSpec((1, d, tn), lambda l, j: (l, 0, j)),
            pl.BlockSpec((1, 1, tn), lambda l, j: (l, 0, j)),
        ],
        out_specs=pl.BlockSpec((1, rows, tn), lambda l, j: (l, 0, j)),
        out_shape=jax.ShapeDtypeStruct((depth, rows, n), F32),
        compiler_params=_params("parallel", "parallel"),
        name="modulation",
    )(cp, w_mod, b_mod.reshape(depth, 1, n))
    return out[:, :b].reshape(depth, b, N_MOD, d)


ROW_SPLITS = (A_KW, SB_W, SB_W, SB_W, SWA_Q_W, SWA_KV_W, SWA_KV_W)


def _in_proj_kernel(x_ref, g_ref, mod_ref, w_ref, wt_ref, kaug_ref, taug_ref, *out_refs):
    row_refs, (qt_ref, vt_ref, kn_ref) = out_refs[:len(ROW_SPLITS)], out_refs[len(ROW_SPLITS):]
    x = x_ref[0]
    h = _rms(x, g_ref[...]) * (1.0 + mod_ref[0, 1:2, :]) + mod_ref[0, 0:1, :]
    hb = h.astype(BF16)
    proj = _dot(hb, w_ref[...])
    start = A_W
    for ref, width in zip(row_refs[1:], ROW_SPLITS[1:]):
        ref[0] = proj[:, start:start + width].astype(BF16)
        start += width
    feat = lax.broadcasted_iota(jnp.int32, (x.shape[0], LANES), 1)
    out_lane = lax.broadcasted_iota(jnp.int32, kn_ref.shape[2:], 1)
    norms = jnp.zeros(kn_ref.shape[2:], F32)
    for hd in range(A_HEADS):
        pair = proj[:, (hd // 2) * LANES:(hd // 2 + 1) * LANES]
        if hd % 2:
            pair = pltpu.roll(pair, A_AUG, 1)
        cols = slice(hd * LANES, (hd + 1) * LANES)
        keys = (jnp.where(feat < A_AUG, pair, 0.0) + kaug_ref[:, cols]).astype(BF16)
        row_refs[0][0, :, cols] = keys
        kf = keys.astype(F32)
        sq = jnp.sum(jnp.where(feat < A_AUG, kf * kf, 0.0), axis=1, keepdims=True)
        norms = jnp.where(out_lane == hd, jnp.sqrt(jnp.max(sq, axis=0, keepdims=True)), norms)
    proj_t = _nt_dot(wt_ref[...], hb)
    for hd in range(A_HEADS):
        q_rows = proj_t[hd * A_AUG:(hd + 1) * A_AUG].astype(BF16)
        q_const = taug_ref[hd * LANES + A_AUG:(hd + 1) * LANES, :].astype(BF16)
        qt_ref[0, hd, 0] = jnp.concatenate([q_rows, q_const], axis=0)
        qsq = q_rows.astype(F32) * q_rows.astype(F32)
        per_map = jnp.maximum(jnp.sum(qsq[:A_QK], axis=0, keepdims=True),
                              jnp.sum(qsq[A_QK:], axis=0, keepdims=True))
        q_norm = jnp.sqrt(jnp.max(per_map, axis=1, keepdims=True))
        norms = jnp.where(out_lane == A_HEADS + hd, q_norm, norms)
        v_rows = proj_t[A_W + hd * A_V:A_W + (hd + 1) * A_V]
        v_const = taug_ref[A_KW + hd * A_VT + A_V:A_KW + (hd + 1) * A_VT, :]
        vt_ref[0, hd, 0] = jnp.concatenate([v_rows, v_const], axis=0).astype(BF16)
    kn_ref[0, 0] = norms


def _bf16_pieces(value, n):
    pieces = []
    rest = float(value)
    for _ in range(n):
        p = float(np.asarray(rest, np.float32).astype(jnp.bfloat16).astype(np.float64))
        pieces.append(p)
        rest -= p
    return pieces


def _slopes_log2(hd):
    pieces = _bf16_pieces(_alibi_slopes(A_HEADS)[hd] * LOG2E, A_PIECES)
    return sum(pieces), pieces


def _alibi_constants(t):
    idx = np.arange(t)
    lo, hi = idx % 256, idx - idx % 256
    kaug = np.zeros((t, A_KW), np.float32)
    taug = np.zeros((A_TW, t), np.float32)
    for hd in range(A_HEADS):
        _, pieces = _slopes_log2(hd)
        k0 = hd * LANES + A_AUG
        for n, piece in enumerate(pieces):
            taug[k0 + n] = -lo
            taug[k0 + A_PIECES + n] = -hi
            kaug[:, k0 + n] = piece
            kaug[:, k0 + A_PIECES + n] = piece
            taug[k0 + 2 * A_PIECES + n] = piece
            taug[k0 + 3 * A_PIECES + n] = piece
            kaug[:, k0 + 2 * A_PIECES + n] = lo
            kaug[:, k0 + 3 * A_PIECES + n] = hi
        r0 = hd * LANES + A_REF
        kaug[:, r0:r0 + 3] = 1.0
        taug[A_KW + hd * A_VT + A_V] = 1.0
    return jnp.asarray(kaug), jnp.asarray(taug)


def _in_proj(x, g, mod, w, wt):
    b, s, d = x.shape
    tm = A_TILE
    nb = s // tm
    n = w.shape[1]
    kaug, taug = _alibi_constants(tm)
    const = lambda shape: pl.BlockSpec(shape, lambda bi, i: (0,) * len(shape))
    return pl.pallas_call(
        _in_proj_kernel,
        grid=(b, nb),
        in_specs=[
            pl.BlockSpec((1, tm, d), lambda bi, i: (bi, i, 0)),
            const((1, d)),
            pl.BlockSpec((1, N_MOD, d), lambda bi, i: (bi, 0, 0)),
            const((d, n)), const(wt.shape), const((tm, A_KW)), const((A_TW, tm)),
        ],
        out_specs=[pl.BlockSpec((1, tm, wd), lambda bi, i: (bi, i, 0)) for wd in ROW_SPLITS] + [
            pl.BlockSpec((1, A_HEADS, 1, LANES, tm), lambda bi, i: (bi, 0, i, 0, 0)),
            pl.BlockSpec((1, A_HEADS, 1, A_VT, tm), lambda bi, i: (bi, 0, i, 0, 0)),
            pl.BlockSpec((1, 1, 8, LANES), lambda bi, i: (bi, i, 0, 0)),
        ],
        out_shape=[jax.ShapeDtypeStruct((b, s, wd), BF16) for wd in ROW_SPLITS] + [
            jax.ShapeDtypeStruct((b, A_HEADS, nb, LANES, tm), BF16),
            jax.ShapeDtypeStruct((b, A_HEADS, nb, A_VT, tm), BF16),
            jax.ShapeDtypeStruct((b, nb, 8, LANES), F32),
        ],
        compiler_params=_params("parallel", "parallel"),
        name="in_proj",
    )(x, g.reshape(1, d), mod, w, wt, kaug, taug)


def _diff_attn_kernel(kn_ref, qn_ref, slopes_ref, lq1_ref, lk1_ref, lq2_ref, lk2_ref, qt_ref,
                      k_ref, vt_ref, o_ref, q_s, m_s, acc_s, *, t, nb, lam_init):
    bi = pl.program_id(0)
    hd = pl.program_id(1)
    i = pl.program_id(2)
    slope = slopes_ref[hd]
    qt = qt_ref[0, 0, 0]
    feat = lax.broadcasted_iota(jnp.int32, (LANES, t), 0)
    zero = jnp.zeros_like(qt)
    q_s[0] = jnp.where(jnp.logical_or(feat < A_QK, feat >= A_AUG), qt, zero)
    q_s[1] = jnp.where(feat >= A_QK, qt, zero)
    q_norm = qn_ref[(bi * A_HEADS + hd) * nb + i]
    key_minus_query = (lax.broadcasted_iota(jnp.int32, (t, t), 0)
                       - lax.broadcasted_iota(jnp.int32, (t, t), 1))
    causal = key_minus_query <= 0

    def reference_rows(j, a):
        off = slope * ((i - j) * t).astype(F32)
        row = lax.broadcasted_iota(jnp.int32, (16, t), 0)
        hi, mid, lo = [p.astype(F32) for p in _split_bf16(-(m_s[a] + off), 3)]
        tile = jnp.where(row == 0, hi, jnp.where(row == 1, mid, jnp.where(row == 2, lo, 0.0)))
        return tile.astype(BF16)

    def scores(j, a, ref=None):
        kj = k_ref[0, pl.ds(pl.multiple_of(j * t, t), t), :]
        qa = q_s[a]
        if ref is not None:
            qa = jnp.concatenate([qa[:A_REF], ref, qa[A_REF + 16:]], axis=0)
        return _dot(kj, qa)

    def exact_block(j, first):
        vtj = vt_ref[0, 0, j]

        @pl.loop(0, 2)
        def _(a):
            s = scores(j, a, None if first else reference_rows(j, a))
            if first:
                s = jnp.where(causal, s, NEG)
            top = jnp.max(s, axis=0, keepdims=True)
            shift = top if first else jnp.maximum(top, 0.0)
            pv = _dot(vtj, jnp.exp2(s - shift).astype(BF16))
            if first:
                acc_s[a] = pv
                m_s[a] = shift
            else:
                acc_s[a] = jnp.exp2(-shift) * acc_s[a] + pv
                m_s[a] = m_s[a] + shift

    def fast_weights(j, a, diagonal):
        s = scores(j, a, reference_rows(j, a))
        if diagonal:
            s = jnp.where(causal, s, NEG)
        return jnp.exp2(s).astype(BF16)

    def fast_group(j, size, from_diagonal=False):
        vts = jnp.concatenate([vt_ref[0, 0, j - u] for u in range(size)], axis=1)
        for a in range(2):
            ps = jnp.concatenate(
                [fast_weights(j - u, a, from_diagonal and u == 0) for u in range(size)], axis=0)
            acc_s[a] = acc_s[a] + _dot(vts, ps)

    def reach_of(j, m_low):
        k_norm = kn_ref[(bi * A_HEADS + hd) * nb + j]
        alibi = jnp.where(j == i, 0.0, slope * (t - (i - j) * t).astype(F32))
        return (q_norm * k_norm + alibi + BOUND_MARGIN) - m_low

    def all_fast(j, size, m_low):
        ok = jnp.bool_(True)
        for u in range(size):
            reach = reach_of(j - u, m_low)
            ok = jnp.logical_and(ok, jnp.logical_and(reach >= SKIP_LOG2, reach <= FAST_LOG2))
        return ok

    plain = q_norm * kn_ref[(bi * A_HEADS + hd) * nb + i] + BOUND_MARGIN <= ZERO_REF_LOG2

    def start_plain():
        m_s[...] = jnp.zeros(m_s.shape, F32)
        acc_s[...] = jnp.zeros(acc_s.shape, F32)
        lead = A_GROUPS[0]
        together = lax.cond(i + 1 >= lead, lambda: all_fast(i, lead, 0.0), lambda: jnp.bool_(False))
        lax.cond(together, lambda: fast_group(i, lead, True), lambda: fast_group(i, 1, True))
        return i + 1 - jnp.where(together, lead, 1)

    def start_exact():
        exact_block(i, True)
        return i

    left = lax.cond(plain, start_plain, start_exact)

    def single(j, m_low):
        reach = reach_of(j, m_low)

        def visit():
            def fast():
                fast_group(j, 1)
                return m_low

            def exact():
                exact_block(j, False)
                return jnp.min(m_s[...])

            return lax.cond(reach <= FAST_LOG2, fast, exact)

        return lax.cond(reach < SKIP_LOG2, lambda: m_low, visit)

    def grouped(size, left, m_low):
        def more(left):
            return lax.cond(left >= size, lambda: all_fast(left - 1, size, m_low),
                            lambda: jnp.bool_(False))

        def body(left):
            fast_group(left - 1, size)
            return left - size

        return lax.while_loop(more, body, left)

    m_low = jnp.min(m_s[...])
    for size in A_GROUPS:
        left = grouped(size, left, m_low)
    lax.fori_loop(0, left, lambda u, m: single(left - 1 - u, m), m_low)

    lam = (jnp.exp(jnp.sum(lq1_ref[...] * lk1_ref[...], keepdims=True))
           - jnp.exp(jnp.sum(lq2_ref[...] * lk2_ref[...], keepdims=True)) + lam_init)
    outs = [acc_s[a, :A_V, :] * (1.0 / acc_s[a, A_V:A_V + 1, :]) for a in range(2)]
    o_ref[0, 0] = outs[0] - lam * outs[1]


def _diff_attn(qt, k, vt, k_norms, q_norms, lq1, lk1, lq2, lk2, lam_init):
    b, s, _ = k.shape
    t = A_TILE
    nb = s // t
    slopes = jnp.asarray([_slopes_log2(hd)[0] for hd in range(A_HEADS)], F32)
    smem = pl.BlockSpec(memory_space=pltpu.SMEM)
    vec = pl.BlockSpec((1, A_QK), lambda bi, h, i: (0, 0))
    return pl.pallas_call(
        functools.partial(_diff_attn_kernel, t=t, nb=nb, lam_init=lam_init),
        grid=(b, A_HEADS, nb),
        in_specs=[
            smem, smem, smem, vec, vec, vec, vec,
            pl.BlockSpec((1, 1, 1, LANES, t), lambda bi, h, i: (bi, h, i, 0, 0)),
            pl.BlockSpec((1, s, LANES), lambda bi, h, i: (bi, 0, h)),
            pl.BlockSpec((1, 1, nb, A_VT, t), lambda bi, h, i: (bi, h, 0, 0, 0)),
        ],
        out_specs=pl.BlockSpec((1, 1, A_V, t), lambda bi, h, i: (bi, h, 0, i)),
        out_shape=jax.ShapeDtypeStruct((b, A_HEADS, A_V, s), F32),
        scratch_shapes=[
            pltpu.VMEM((2, LANES, t), BF16),
            pltpu.VMEM((2, 1, t), F32),
            pltpu.VMEM((2, A_VT, t), F32),
        ],
        compiler_params=_params("parallel", "parallel", "arbitrary"),
        name="diff_attn",
    )(k_norms, q_norms, slopes, lq1.reshape(1, A_QK), lk1.reshape(1, A_QK), lq2.reshape(1, A_QK),
      lk2.reshape(1, A_QK), qt, k, vt)


def _sb_attn_kernel(q_ref, k_ref, v_ref, o_ref, r_s, acc_s, *, tq):
    i = pl.program_id(2)
    q = q_ref[0]
    lane = lax.broadcasted_iota(jnp.int32, (tq, LANES), 1)
    qm = [jnp.where(lane < SB_DIM, q, jnp.zeros_like(q)),
          jnp.where(lane >= SB_DIM, q, jnp.zeros_like(q))]
    row = lax.broadcasted_iota(jnp.int32, (tq, tq), 0)
    col = lax.broadcasted_iota(jnp.int32, (tq, tq), 1)
    strict = col < row
    later = (row > col).astype(BF16)

    def keys(j):
        return k_ref[0, pl.ds(pl.multiple_of(j * tq, tq), tq), :]

    def values(j):
        return v_ref[0, pl.ds(pl.multiple_of(j * tq, tq), tq), :]

    def log_weights(kj, h, diagonal):
        z = _nt_dot(qm[h], kj)
        log_beta = jnp.minimum(z, 0.0) - jnp.log2(1.0 + jnp.exp2(-jnp.abs(z)))
        log_1mb = log_beta - z
        if diagonal:
            log_1mb = jnp.where(strict, log_1mb, 0.0)
        after = _dot(jnp.concatenate(_split_bf16(log_1mb, 2), axis=1), later2)
        return log_beta + after, jnp.sum(log_1mb, axis=1, keepdims=True)

    later2 = jnp.concatenate([later, later], axis=0)
    prev = jnp.maximum(i - 1, 0)
    has_prev = (i > 0).astype(F32)
    no_prev = jnp.where(i > 0, 0.0, NEG)
    k_diag, k_prev = keys(i), keys(prev)
    v_both = jnp.concatenate([values(i), values(prev)], axis=0)
    for h in range(2):
        lw_d, tot_d = log_weights(k_diag, h, True)
        lw_p, tot_p = log_weights(k_prev, h, False)
        a_d = jnp.where(strict, jnp.exp2(lw_d), 0.0)
        a_p = jnp.exp2(lw_p + (tot_d + no_prev))
        acc_s[h] = _dot(jnp.concatenate([a_d.astype(BF16), a_p.astype(BF16)], axis=1), v_both)
        r_s[h] = tot_d + tot_p * has_prev

    def cond(carry):
        j, live = carry
        return jnp.logical_and(j >= 0, live)

    def body(carry):
        j, _ = carry
        kj, vj = keys(j), values(j)
        for h in range(2):
            lw, tot = log_weights(kj, h, False)
            run = r_s[h]
            acc_s[h] = acc_s[h] + _dot(jnp.exp2(lw + run).astype(BF16), vj)
            r_s[h] = run + tot
        return j - 1, jnp.max(r_s[...]) > SKIP_LOG2

    lax.while_loop(cond, body, (i - 2, jnp.max(r_s[...]) > SKIP_LOG2))
    o_ref[0] = jnp.where(lane < SB_DIM, acc_s[0], acc_s[1])


def _sb_attn(q, k, v):
    b, s, _ = q.shape
    tq = min(ATT_TILE, s)
    pairs = SB_W // LANES
    return pl.pallas_call(
        functools.partial(_sb_attn_kernel, tq=tq),
        grid=(b, pairs, s // tq),
        in_specs=[
            pl.BlockSpec((1, tq, LANES), lambda bi, p, i: (bi, i, p)),
            pl.BlockSpec((1, s, LANES), lambda bi, p, i: (bi, 0, p)),
            pl.BlockSpec((1, s, LANES), lambda bi, p, i: (bi, 0, p)),
        ],
        out_specs=pl.BlockSpec((1, tq, LANES), lambda bi, p, i: (bi, i, p)),
        out_shape=jax.ShapeDtypeStruct((b, s, SB_W), F32),
        scratch_shapes=[
            pltpu.VMEM((2, tq, 1), F32),
            pltpu.VMEM((2, tq, LANES), F32),
        ],
        compiler_params=_params("parallel", "parallel", "arbitrary"),
        name="sb_attn",
    )(q, k, v)


def _swa_attn_kernel(sinks_ref, bias_ref, q_ref, kc_ref, kp_ref, vc_ref, vp_ref, o_ref, *, tq):
    i = pl.program_id(1)
    half = tq // 2
    col = lax.broadcasted_iota(jnp.int32, (half, tq), 1)
    has_key = col >= jnp.where(i > 0, 0, half)
    lane = lax.broadcasted_iota(jnp.int32, (half, LANES), 1)

    kwin = [jnp.concatenate([kp_ref[0, half:, :], kc_ref[0, :half, :]], axis=0), kc_ref[0]]
    vwin = [jnp.concatenate([vp_ref[0, half:, :], vc_ref[0, :half, :]], axis=0), vc_ref[0]]
    for hf in range(2):
        rows = slice(hf * half, (hf + 1) * half)
        for t in range(SWA_REP):
            q = q_ref[0, rows, t * LANES:(t + 1) * LANES]
            outs = []
            for g in range(SWA_KV_HEADS):
                head = g * SWA_REP + t
                qg = jnp.where((lane // SWA_DIM) == g, q, jnp.zeros_like(q))
                s = _nt_dot(qg, kwin[hf]) + bias_ref[head]
                if hf == 0:
                    s = jnp.where(has_key, s, NEG)
                sink = sinks_ref[head] * LOG2E
                m = jnp.maximum(jnp.max(s, axis=1, keepdims=True), sink)
                p = jnp.exp2(s - m)
                den = jnp.sum(p, axis=1, keepdims=True) + jnp.exp2(sink - m)
                outs.append(_dot(p.astype(BF16), vwin[hf]) / den)
            o_ref[0, rows, t * LANES:(t + 1) * LANES] = jnp.where(lane < SWA_DIM, outs[0], outs[1])


def _swa_bias(tq):
    half = tq // 2
    dist = np.arange(half)[:, None] + half - np.arange(tq)[None, :]
    in_window = (dist >= 0) & (dist < WINDOW)
    slopes = _alibi_slopes(SWA_Q_HEADS) * LOG2E
    bias = np.where(in_window[None], -slopes[:, None, None] * dist[None], NEG)
    return jnp.asarray(bias, F32)


def _swa_attn(q, k, v, sinks):
    b, s, _ = q.shape
    tq = min(ATT_TILE, s)
    smem = pl.BlockSpec(memory_space=pltpu.SMEM)
    cur = pl.BlockSpec((1, tq, SWA_KV_W), lambda bi, i: (bi, i, 0))
    prev = pl.BlockSpec((1, tq, SWA_KV_W), lambda bi, i: (bi, jnp.maximum(i - 1, 0), 0))
    return pl.pallas_call(
        functools.partial(_swa_attn_kernel, tq=tq),
        grid=(b, s // tq),
        in_specs=[
            smem,
            pl.BlockSpec((SWA_Q_HEADS, tq // 2, tq), lambda bi, i: (0, 0, 0)),
            pl.BlockSpec((1, tq, SWA_Q_W), lambda bi, i: (bi, i, 0)),
            cur, prev, cur, prev,
        ],
        out_specs=pl.BlockSpec((1, tq, SWA_Q_W), lambda bi, i: (bi, i, 0)),
        out_shape=jax.ShapeDtypeStruct((b, s, SWA_Q_W), F32),
        compiler_params=_params("parallel", "parallel"),
        name="swa_attn",
    )(sinks.astype(F32), _swa_bias(tq), q, k, k, v, v)


def _mix_ffn_kernel(ya_ref, yb_ref, yc_ref, x_ref, ga_ref, gb_ref, gc_ref, ln2_ref, mod_ref,
                    wa_ref, wb_ref, wc_ref, wg_ref, wu_ref, wd_ref, fg_ref, o_ref,
                    *, a_scale, final):
    mixed = _dot(_rms(yb_ref[0], gb_ref[...]).astype(BF16), wb_ref[...])
    mixed += _dot(_rms(yc_ref[0], gc_ref[...]).astype(BF16), wc_ref[...])
    packed = []
    for h in range(0, A_HEADS, 2):
        pair = []
        for yt in (ya_ref[0, h], ya_ref[0, h + 1]):
            ms = jnp.mean(yt * yt, axis=0, keepdims=True)
            pair.append(yt * lax.rsqrt(ms + EPS))
        packed.append(jnp.concatenate(pair, axis=0).T)
    na = jnp.concatenate(packed, axis=1) * ga_ref[...] * a_scale
    mixed += _dot(na.astype(BF16), wa_ref[...])
    x1 = x_ref[0] + mod_ref[0, 2:3, :] * mixed
    h2 = (_rms(x1, ln2_ref[...]) * (1.0 + mod_ref[0, 4:5, :]) + mod_ref[0, 3:4, :]).astype(BF16)
    gate = _dot(h2, wg_ref[...])
    up = _dot(h2, wu_ref[...])
    act = gate * jax.nn.sigmoid(gate) * up
    out = x1 + mod_ref[0, 5:6, :] * _dot(act.astype(BF16), wd_ref[...])
    if final:
        out = _rms(out, fg_ref[...])
    o_ref[0] = out


def _mix_ffn(ya, yb, yc, x, ga, gb, gc, ln2, mod, wa, wb, wc, wg, wu, wd, fg, a_scale, final):
    b, s, d = x.shape
    f = wg.shape[1]
    tm = min(FFN_TILE, s)
    row = lambda wd_: pl.BlockSpec((1, tm, wd_), lambda bi, i: (bi, i, 0))
    const = lambda r, cdim: pl.BlockSpec((r, cdim), lambda bi, i: (0, 0),
                                         pipeline_mode=pl.Buffered(1))
    return pl.pallas_call(
        functools.partial(_mix_ffn_kernel, a_scale=a_scale, final=final),
        grid=(b, s // tm),
        in_specs=[
            pl.BlockSpec((1, A_HEADS, A_V, tm), lambda bi, i: (bi, 0, 0, i)),
            row(SB_W), row(SWA_Q_W), row(d),
            const(1, A_W), const(1, SB_W), const(1, SWA_Q_W), const(1, d),
            pl.BlockSpec((1, N_MOD, d), lambda bi, i: (bi, 0, 0)),
            const(A_W, d), const(SB_W, d), const(SWA_Q_W, d),
            const(d, f), const(d, f), const(f, d), const(1, d),
        ],
        out_specs=row(d),
        out_shape=jax.ShapeDtypeStruct((b, s, d), F32),
        compiler_params=_params("parallel", "parallel"),
        name="mix_ffn",
    )(ya, yb, yc, x, ga, gb, gc, ln2.reshape(1, d), mod, wa, wb, wc, wg, wu, wd, fg.reshape(1, d))


def _swa_perm():
    cols = []
    for t in range(SWA_REP):
        for g in range(SWA_KV_HEADS):
            head = g * SWA_REP + t
            cols.extend(range(head * SWA_DIM, (head + 1) * SWA_DIM))
    return jnp.asarray(cols, jnp.int32)


def _prep_w_in(w):
    bounds = np.cumsum((0, A_W, A_W, A_W, SB_W, SB_W, SB_W, SWA_Q_W, SWA_KV_W, SWA_KV_W))
    qa, ka, va, qb, kb, vb, qc, kc, vc = [w[:, bounds[n]:bounds[n + 1]] for n in range(9)]
    row = jnp.concatenate([
        ka, qb * (SB_DIM ** -0.5 * LOG2E), kb, vb,
        (qc * (SWA_DIM ** -0.5 * LOG2E))[:, _swa_perm()], kc, vc], axis=1)
    transposed = jnp.concatenate([
        qa * (A_QK ** -0.5 * LOG2E), va], axis=1).T
    return row.astype(BF16), transposed.astype(BF16)


def kernel(x, c, ln1_g, ln2_g, w_mod, b_mod, w_in, lam_q1, lam_k1, lam_q2, lam_k2, diff_norm_g,
           sb_norm_g, swa_norm_g, swa_sinks, w_out, w_gate, w_up, w_down, final_g):
    depth = w_in.shape[0]
    perm = _swa_perm()
    mod = _modulation(c, w_mod, b_mod)
    for l in range(depth):
        lam_init = 0.8 - 0.6 * math.exp(-0.3 * l)
        w_row, w_t = _prep_w_in(w_in[l])
        ka, qb, kb, vb, qc, kc, vc, qta, vta, kn = _in_proj(x, ln1_g[l], mod[l], w_row, w_t)
        norms = kn[:, :, 0, :2 * A_HEADS].transpose(0, 2, 1)
        k_norms, q_norms = norms[:, :A_HEADS].reshape(-1), norms[:, A_HEADS:].reshape(-1)
        ya = _diff_attn(qta, ka, vta, k_norms, q_norms, lam_q1[l], lam_k1[l], lam_q2[l], lam_k2[l], lam_init)
        yb = _sb_attn(qb, kb, vb)
        yc = _swa_attn(qc, kc, vc, swa_sinks[l])
        wo = w_out[l].astype(BF16)
        x = _mix_ffn(
            ya, yb, yc, x,
            jnp.tile(diff_norm_g[l], A_HEADS).reshape(1, A_W),
            sb_norm_g[l].reshape(1, SB_W),
            swa_norm_g[l][perm].reshape(1, SWA_Q_W),
            ln2_g[l], mod[l],
            wo[:A_W], wo[A_W:A_W + SB_W], wo[A_W + SB_W:][perm],
            w_gate[l].astype(BF16), w_up[l].astype(BF16), w_down[l].astype(BF16), final_g,
            1.0 - lam_init, l == depth - 1)
    return x
```

```python
import functools
import math

import numpy as np
import jax
import jax.numpy as jnp
from jax import lax
from jax.experimental import pallas as pl
from jax.experimental.pallas import tpu as pltpu

F32 = jnp.float32
BF16 = jnp.bfloat16

N_MOD = 6
EPS = 1e-6
A_HEADS = 4
A_QK = 32
A_V = 64
SB_HEADS = 4
SB_DIM = 64
SWA_Q_HEADS = 8
SWA_KV_HEADS = 2
SWA_REP = 4
SWA_DIM = 64
WINDOW = 128
A_W = 256
SB_W = 256
SWA_Q_W = 512
SWA_KV_W = 128

LANES = 128
NEG = -1e30

VMEM_LIMIT = 56 * 1024 * 1024

ROW_TILE = 512
FFN_TILE = 512
ATT_TILE = 256
A_TILE = ROW_TILE

LOG2E = math.log2(math.e)
A_AUG = 2 * A_QK
A_REF = A_AUG + 16
A_PIECES = 4
A_VT = 80
A_KW = A_HEADS * LANES
A_TW = A_HEADS * (LANES + A_VT)
SKIP_LOG2 = -160.0
FAST_LOG2 = 80.0
A_GROUPS = (4,)
ZERO_REF_LOG2 = 60.0
BOUND_MARGIN = 1.0


def _params(*sem):
    return pltpu.CompilerParams(dimension_semantics=sem, vmem_limit_bytes=VMEM_LIMIT)


def _nt_dot(a, b):
    return lax.dot_general(a, b, (((1,), (1,)), ((), ())), preferred_element_type=F32)


def _dot(a, b):
    return jnp.dot(a, b, preferred_element_type=F32)


def _rms(x, g):
    ms = jnp.mean(x * x, axis=-1, keepdims=True)
    return x * lax.rsqrt(ms + EPS) * g


def _alibi_slopes(n):
    return 2.0 ** (-8.0 * np.arange(1, n + 1, dtype=np.float64) / n)


def _split_bf16(x, n):
    pieces = []
    for _ in range(n - 1):
        p = x.astype(BF16)
        pieces.append(p)
        x = x - p.astype(F32)
    pieces.append(x.astype(BF16))
    return pieces


def _mod_kernel(c_ref, w_ref, b_ref, o_ref):
    cv = c_ref[...]
    ca = cv * jax.nn.sigmoid(cv)
    o_ref[0] = jnp.dot(ca, w_ref[0], preferred_element_type=F32,
                       precision=lax.Precision.HIGHEST) + b_ref[0]


def _modulation(c, w_mod, b_mod):
    depth, d, n = w_mod.shape
    b = c.shape[0]
    rows = 8
    cp = jnp.zeros((rows, d), F32).at[:b].set(c)
    tn = 1024
    out = pl.pallas_call(
        _mod_kernel,
        grid=(depth, n // tn),
        in_specs=[
            pl.BlockSpec((rows, d), lambda l, j: (0, 0)),
            pl.BlockSpec((1, d, tn), lambda l, j: (l, 0, j)),
            pl.BlockSpec((1, 1, tn), lambda l, j: (l, 0, j)),
        ],
        out_specs=pl.BlockSpec((1, rows, tn), lambda l, j: (l, 0, j)),
        out_shape=jax.ShapeDtypeStruct((depth, rows, n), F32),
        compiler_params=_params("parallel", "parallel"),
        name="modulation",
    )(cp, w_mod, b_mod.reshape(depth, 1, n))
    return out[:, :b].reshape(depth, b, N_MOD, d)


ROW_SPLITS = (A_KW, SB_W, SB_W, SB_W, SWA_Q_W, SWA_KV_W, SWA_KV_W)


def _in_proj_kernel(x_ref, g_ref, mod_ref, w_ref, wt_ref, kaug_ref, taug_ref, *out_refs):
    row_refs, (qt_ref, vt_ref, kn_ref) = out_refs[:len(ROW_SPLITS)], out_refs[len(ROW_SPLITS):]
    x = x_ref[0]
    h = _rms(x, g_ref[...]) * (1.0 + mod_ref[0, 1:2, :]) + mod_ref[0, 0:1, :]
    hb = h.astype(BF16)
    proj = _dot(hb, w_ref[...])
    start = A_W
    for ref, width in zip(row_refs[1:], ROW_SPLITS[1:]):
        ref[0] = proj[:, start:start + width].astype(BF16)
        start += width
    feat = lax.broadcasted_iota(jnp.int32, (x.shape[0], LANES), 1)
    out_lane = lax.broadcasted_iota(jnp.int32, kn_ref.shape[2:], 1)
    norms = jnp.zeros(kn_ref.shape[2:], F32)
    for hd in range(A_HEADS):
        pair = proj[:, (hd // 2) * LANES:(hd // 2 + 1) * LANES]
        if hd % 2:
            pair = pltpu.roll(pair, A_AUG, 1)
        cols = slice(hd * LANES, (hd + 1) * LANES)
        keys = (jnp.where(feat < A_AUG, pair, 0.0) + kaug_ref[:, cols]).astype(BF16)
        row_refs[0][0, :, cols] = keys
        kf = keys.astype(F32)
        sq = jnp.sum(jnp.where(feat < A_AUG, kf * kf, 0.0), axis=1, keepdims=True)
        norms = jnp.where(out_lane == hd, jnp.sqrt(jnp.max(sq, axis=0, keepdims=True)), norms)
    proj_t = _nt_dot(wt_ref[...], hb)
    for hd in range(A_HEADS):
        q_rows = proj_t[hd * A_AUG:(hd + 1) * A_AUG].astype(BF16)
        q_const = taug_ref[hd * LANES + A_AUG:(hd + 1) * LANES, :].astype(BF16)
        qt_ref[0, hd, 0] = jnp.concatenate([q_rows, q_const], axis=0)
        qsq = q_rows.astype(F32) * q_rows.astype(F32)
        per_map = jnp.maximum(jnp.sum(qsq[:A_QK], axis=0, keepdims=True),
                              jnp.sum(qsq[A_QK:], axis=0, keepdims=True))
        q_norm = jnp.sqrt(jnp.max(per_map, axis=1, keepdims=True))
        norms = jnp.where(out_lane == A_HEADS + hd, q_norm, norms)
        v_rows = proj_t[A_W + hd * A_V:A_W + (hd + 1) * A_V]
        v_const = taug_ref[A_KW + hd * A_VT + A_V:A_KW + (hd + 1) * A_VT, :]
        vt_ref[0, hd, 0] = jnp.concatenate([v_rows, v_const], axis=0).astype(BF16)
    kn_ref[0, 0] = norms


def _bf16_pieces(value, n):
    pieces = []
    rest = float(value)
    for _ in range(n):
        p = float(np.asarray(rest, np.float32).astype(jnp.bfloat16).astype(np.float64))
        pieces.append(p)
        rest -= p
    return pieces


def _slopes_log2(hd):
    pieces = _bf16_pieces(_alibi_slopes(A_HEADS)[hd] * LOG2E, A_PIECES)
    return sum(pieces), pieces


def _alibi_constants(t):
    idx = np.arange(t)
    lo, hi = idx % 256, idx - idx % 256
    kaug = np.zeros((t, A_KW), np.float32)
    taug = np.zeros((A_TW, t), np.float32)
    for hd in range(A_HEADS):
        _, pieces = _slopes_log2(hd)
        k0 = hd * LANES + A_AUG
        for n, piece in enumerate(pieces):
            taug[k0 + n] = -lo
            taug[k0 + A_PIECES + n] = -hi
            kaug[:, k0 + n] = piece
            kaug[:, k0 + A_PIECES + n] = piece
            taug[k0 + 2 * A_PIECES + n] = piece
            taug[k0 + 3 * A_PIECES + n] = piece
            kaug[:, k0 + 2 * A_PIECES + n] = lo
            kaug[:, k0 + 3 * A_PIECES + n] = hi
        r0 = hd * LANES + A_REF
        kaug[:, r0:r0 + 3] = 1.0
        taug[A_KW + hd * A_VT + A_V] = 1.0
    return jnp.asarray(kaug), jnp.asarray(taug)


def _in_proj(x, g, mod, w, wt):
    b, s, d = x.shape
    tm = A_TILE
    nb = s // tm
    n = w.shape[1]
    kaug, taug = _alibi_constants(tm)
    const = lambda shape: pl.BlockSpec(shape, lambda bi, i: (0,) * len(shape))
    return pl.pallas_call(
        _in_proj_kernel,
        grid=(b, nb),
        in_specs=[
            pl.BlockSpec((1, tm, d), lambda bi, i: (bi, i, 0)),
            const((1, d)),
            pl.BlockSpec((1, N_MOD, d), lambda bi, i: (bi, 0, 0)),
            const((d, n)), const(wt.shape), const((tm, A_KW)), const((A_TW, tm)),
        ],
        out_specs=[pl.BlockSpec((1, tm, wd), lambda bi, i: (bi, i, 0)) for wd in ROW_SPLITS] + [
            pl.BlockSpec((1, A_HEADS, 1, LANES, tm), lambda bi, i: (bi, 0, i, 0, 0)),
            pl.BlockSpec((1, A_HEADS, 1, A_VT, tm), lambda bi, i: (bi, 0, i, 0, 0)),
            pl.BlockSpec((1, 1, 8, LANES), lambda bi, i: (bi, i, 0, 0)),
        ],
        out_shape=[jax.ShapeDtypeStruct((b, s, wd), BF16) for wd in ROW_SPLITS] + [
            jax.ShapeDtypeStruct((b, A_HEADS, nb, LANES, tm), BF16),
            jax.ShapeDtypeStruct((b, A_HEADS, nb, A_VT, tm), BF16),
            jax.ShapeDtypeStruct((b, nb, 8, LANES), F32),
        ],
        compiler_params=_params("parallel", "parallel"),
        name="in_proj",
    )(x, g.reshape(1, d), mod, w, wt, kaug, taug)


def _diff_attn_kernel(kn_ref, qn_ref, slopes_ref, lq1_ref, lk1_ref, lq2_ref, lk2_ref, qt_ref,
                      k_ref, vt_ref, o_ref, q_s, m_s, acc_s, *, t, nb, lam_init):
    bi = pl.program_id(0)
    hd = pl.program_id(1)
    i = pl.program_id(2)
    slope = slopes_ref[hd]
    qt = qt_ref[0, 0, 0]
    feat = lax.broadcasted_iota(jnp.int32, (LANES, t), 0)
    zero = jnp.zeros_like(qt)
    q_s[0] = jnp.where(jnp.logical_or(feat < A_QK, feat >= A_AUG), qt, zero)
    q_s[1] = jnp.where(feat >= A_QK, qt, zero)
    q_norm = qn_ref[(bi * A_HEADS + hd) * nb + i]
    key_minus_query = (lax.broadcasted_iota(jnp.int32, (t, t), 0)
                       - lax.broadcasted_iota(jnp.int32, (t, t), 1))
    causal = key_minus_query <= 0

    def reference_rows(j, a):
        off = slope * ((i - j) * t).astype(F32)
        row = lax.broadcasted_iota(jnp.int32, (16, t), 0)
        hi, mid, lo = [p.astype(F32) for p in _split_bf16(-(m_s[a] + off), 3)]
        tile = jnp.where(row == 0, hi, jnp.where(row == 1, mid, jnp.where(row == 2, lo, 0.0)))
        return tile.astype(BF16)

    def scores(j, a, ref=None):
        kj = k_ref[0, pl.ds(pl.multiple_of(j * t, t), t), :]
        qa = q_s[a]
        if ref is not None:
            qa = jnp.concatenate([qa[:A_REF], ref, qa[A_REF + 16:]], axis=0)
        return _dot(kj, qa)

    def exact_block(j, first):
        vtj = vt_ref[0, 0, j]

        @pl.loop(0, 2)
        def _(a):
            s = scores(j, a, None if first else reference_rows(j, a))
            if first:
                s = jnp.where(causal, s, NEG)
            top = jnp.max(s, axis=0, keepdims=True)
            shift = top if first else jnp.maximum(top, 0.0)
            pv = _dot(vtj, jnp.exp2(s - shift).astype(BF16))
            if first:
                acc_s[a] = pv
                m_s[a] = shift
            else:
                acc_s[a] = jnp.exp2(-shift) * acc_s[a] + pv
                m_s[a] = m_s[a] + shift

    def fast_weights(j, a, diagonal):
        s = scores(j, a, reference_rows(j, a))
        if diagonal:
            s = jnp.where(causal, s, NEG)
        return jnp.exp2(s).astype(BF16)

    def fast_group(j, size, from_diagonal=False):
        vts = jnp.concatenate([vt_ref[0, 0, j - u] for u in range(size)], axis=1)
        for a in range(2):
            ps = jnp.concatenate(
                [fast_weights(j - u, a, from_diagonal and u == 0) for u in range(size)], axis=0)
            acc_s[a] = acc_s[a] + _dot(vts, ps)

    def reach_of(j, m_low):
        k_norm = kn_ref[(bi * A_HEADS + hd) * nb + j]
        alibi = jnp.where(j == i, 0.0, slope * (t - (i - j) * t).astype(F32))
        return (q_norm * k_norm + alibi + BOUND_MARGIN) - m_low

    def all_fast(j, size, m_low):
        ok = jnp.bool_(True)
        for u in range(size):
            reach = reach_of(j - u, m_low)
            ok = jnp.logical_and(ok, jnp.logical_and(reach >= SKIP_LOG2, reach <= FAST_LOG2))
        return ok

    plain = q_norm * kn_ref[(bi * A_HEADS + hd) * nb + i] + BOUND_MARGIN <= ZERO_REF_LOG2

    def start_plain():
        m_s[...] = jnp.zeros(m_s.shape, F32)
        acc_s[...] = jnp.zeros(acc_s.shape, F32)
        lead = A_GROUPS[0]
        together = lax.cond(i + 1 >= lead, lambda: all_fast(i, lead, 0.0), lambda: jnp.bool_(False))
        lax.cond(together, lambda: fast_group(i, lead, True), lambda: fast_group(i, 1, True))
        return i + 1 - jnp.where(together, lead, 1)

    def start_exact():
        exact_block(i, True)
        return i

    left = lax.cond(plain, start_plain, start_exact)

    def single(j, m_low):
        reach = reach_of(j, m_low)

        def visit():
            def fast():
                fast_group(j, 1)
                return m_low

            def exact():
                exact_block(j, False)
                return jnp.min(m_s[...])

            return lax.cond(reach <= FAST_LOG2, fast, exact)

        return lax.cond(reach < SKIP_LOG2, lambda: m_low, visit)

    def grouped(size, left, m_low):
        def more(left):
            return lax.cond(left >= size, lambda: all_fast(left - 1, size, m_low),
                            lambda: jnp.bool_(False))

        def body(left):
            fast_group(left - 1, size)
            return left - size

        return lax.while_loop(more, body, left)

    m_low = jnp.min(m_s[...])
    for size in A_GROUPS:
        left = grouped(size, left, m_low)
    lax.fori_loop(0, left, lambda u, m: single(left - 1 - u, m), m_low)

    lam = (jnp.exp(jnp.sum(lq1_ref[...] * lk1_ref[...], keepdims=True))
           - jnp.exp(jnp.sum(lq2_ref[...] * lk2_ref[...], keepdims=True)) + lam_init)
    outs = [acc_s[a, :A_V, :] * (1.0 / acc_s[a, A_V:A_V + 1, :]) for a in range(2)]
    o_ref[0, 0] = outs[0] - lam * outs[1]


def _diff_attn(qt, k, vt, k_norms, q_norms, lq1, lk1, lq2, lk2, lam_init):
    b, s, _ = k.shape
    t = A_TILE
    nb = s // t
    slopes = jnp.asarray([_slopes_log2(hd)[0] for hd in range(A_HEADS)], F32)
    smem = pl.BlockSpec(memory_space=pltpu.SMEM)
    vec = pl.BlockSpec((1, A_QK), lambda bi, h, i: (0, 0))
    return pl.pallas_call(
        functools.partial(_diff_attn_kernel, t=t, nb=nb, lam_init=lam_init),
        grid=(b, A_HEADS, nb),
        in_specs=[
            smem, smem, smem, vec, vec, vec, vec,
            pl.BlockSpec((1, 1, 1, LANES, t), lambda bi, h, i: (bi, h, i, 0, 0)),
            pl.BlockSpec((1, s, LANES), lambda bi, h, i: (bi, 0, h)),
            pl.BlockSpec((1, 1, nb, A_VT, t), lambda bi, h, i: (bi, h, 0, 0, 0)),
        ],
        out_specs=pl.BlockSpec((1, 1, A_V, t), lambda bi, h, i: (bi, h, 0, i)),
        out_shape=jax.ShapeDtypeStruct((b, A_HEADS, A_V, s), F32),
        scratch_shapes=[
            pltpu.VMEM((2, LANES, t), BF16),
            pltpu.VMEM((2, 1, t), F32),
            pltpu.VMEM((2, A_VT, t), F32),
        ],
        compiler_params=_params("parallel", "parallel", "arbitrary"),
        name="diff_attn",
    )(k_norms, q_norms, slopes, lq1.reshape(1, A_QK), lk1.reshape(1, A_QK), lq2.reshape(1, A_QK),
      lk2.reshape(1, A_QK), qt, k, vt)


def _sb_attn_kernel(q_ref, k_ref, v_ref, o_ref, r_s, acc_s, *, tq):
    i = pl.program_id(2)
    q = q_ref[0]
    lane = lax.broadcasted_iota(jnp.int32, (tq, LANES), 1)
    qm = [jnp.where(lane < SB_DIM, q, jnp.zeros_like(q)),
          jnp.where(lane >= SB_DIM, q, jnp.zeros_like(q))]
    row = lax.broadcasted_iota(jnp.int32, (tq, tq), 0)
    col = lax.broadcasted_iota(jnp.int32, (tq, tq), 1)
    strict = col < row
    later = (row > col).astype(BF16)

    def keys(j):
        return k_ref[0, pl.ds(pl.multiple_of(j * tq, tq), tq), :]

    def values(j):
        return v_ref[0, pl.ds(pl.multiple_of(j * tq, tq), tq), :]

    def log_weights(kj, h, diagonal):
        z = _nt_dot(qm[h], kj)
        log_beta = jnp.minimum(z, 0.0) - jnp.log2(1.0 + jnp.exp2(-jnp.abs(z)))
        log_1mb = log_beta - z
        if diagonal:
            log_1mb = jnp.where(strict, log_1mb, 0.0)
        after = _dot(jnp.concatenate(_split_bf16(log_1mb, 2), axis=1), later2)
        return log_beta + after, jnp.sum(log_1mb, axis=1, keepdims=True)

    later2 = jnp.concatenate([later, later], axis=0)
    prev = jnp.maximum(i - 1, 0)
    has_prev = (i > 0).astype(F32)
    no_prev = jnp.where(i > 0, 0.0, NEG)
    k_diag, k_prev = keys(i), keys(prev)
    v_both = jnp.concatenate([values(i), values(prev)], axis=0)
    for h in range(2):
        lw_d, tot_d = log_weights(k_diag, h, True)
        lw_p, tot_p = log_weights(k_prev, h, False)
        a_d = jnp.where(strict, jnp.exp2(lw_d), 0.0)
        a_p = jnp.exp2(lw_p + (tot_d + no_prev))
        acc_s[h] = _dot(jnp.concatenate([a_d.astype(BF16), a_p.astype(BF16)], axis=1), v_both)
        r_s[h] = tot_d + tot_p * has_prev

    def cond(carry):
        j, live = carry
        return jnp.logical_and(j >= 0, live)

    def body(carry):
        j, _ = carry
        kj, vj = keys(j), values(j)
        for h in range(2):
            lw, tot = log_weights(kj, h, False)
            run = r_s[h]
            acc_s[h] = acc_s[h] + _dot(jnp.exp2(lw + run).astype(BF16), vj)
            r_s[h] = run + tot
        return j - 1, jnp.max(r_s[...]) > SKIP_LOG2

    lax.while_loop(cond, body, (i - 2, jnp.max(r_s[...]) > SKIP_LOG2))
    o_ref[0] = jnp.where(lane < SB_DIM, acc_s[0], acc_s[1])


def _sb_attn(q, k, v):
    b, s, _ = q.shape
    tq = min(ATT_TILE, s)
    pairs = SB_W // LANES
    return pl.pallas_call(
        functools.partial(_sb_attn_kernel, tq=tq),
        grid=(b, pairs, s // tq),
        in_specs=[
            pl.BlockSpec((1, tq, LANES), lambda bi, p, i: (bi, i, p)),
            pl.BlockSpec((1, s, LANES), lambda bi, p, i: (bi, 0, p)),
            pl.BlockSpec((1, s, LANES), lambda bi, p, i: (bi, 0, p)),
        ],
        out_specs=pl.BlockSpec((1, tq, LANES), lambda bi, p, i: (bi, i, p)),
        out_shape=jax.ShapeDtypeStruct((b, s, SB_W), F32),
        scratch_shapes=[
            pltpu.VMEM((2, tq, 1), F32),
            pltpu.VMEM((2, tq, LANES), F32),
        ],
        compiler_params=_params("parallel", "parallel", "arbitrary"),
        name="sb_attn",
    )(q, k, v)


def _swa_attn_kernel(sinks_ref, bias_ref, q_ref, kc_ref, kp_ref, vc_ref, vp_ref, o_ref, *, tq):
    i = pl.program_id(1)
    half = tq // 2
    col = lax.broadcasted_iota(jnp.int32, (half, tq), 1)
    has_key = col >= jnp.where(i > 0, 0, half)
    lane = lax.broadcasted_iota(jnp.int32, (half, LANES), 1)

    kwin = [jnp.concatenate([kp_ref[0, half:, :], kc_ref[0, :half, :]], axis=0), kc_ref[0]]
    vwin = [jnp.concatenate([vp_ref[0, half:, :], vc_ref[0, :half, :]], axis=0), vc_ref[0]]
    for hf in range(2):
        rows = slice(hf * half, (hf + 1) * half)
        for t in range(SWA_REP):
            q = q_ref[0, rows, t * LANES:(t + 1) * LANES]
            outs = []
            for g in range(SWA_KV_HEADS):
                head = g * SWA_REP + t
                qg = jnp.where((lane // SWA_DIM) == g, q, jnp.zeros_like(q))
                s = _nt_dot(qg, kwin[hf]) + bias_ref[head]
                if hf == 0:
                    s = jnp.where(has_key, s, NEG)
                sink = sinks_ref[head] * LOG2E
                m = jnp.maximum(jnp.max(s, axis=1, keepdims=True), sink)
                p = jnp.exp2(s - m)
                den = jnp.sum(p, axis=1, keepdims=True) + jnp.exp2(sink - m)
                outs.append(_dot(p.astype(BF16), vwin[hf]) / den)
            o_ref[0, rows, t * LANES:(t + 1) * LANES] = jnp.where(lane < SWA_DIM, outs[0], outs[1])


def _swa_bias(tq):
    half = tq // 2
    dist = np.arange(half)[:, None] + half - np.arange(tq)[None, :]
    in_window = (dist >= 0) & (dist < WINDOW)
    slopes = _alibi_slopes(SWA_Q_HEADS) * LOG2E
    bias = np.where(in_window[None], -slopes[:, None, None] * dist[None], NEG)
    return jnp.asarray(bias, F32)


def _swa_attn(q, k, v, sinks):
    b, s, _ = q.shape
    tq = min(ATT_TILE, s)
    smem = pl.BlockSpec(memory_space=pltpu.SMEM)
    cur = pl.BlockSpec((1, tq, SWA_KV_W), lambda bi, i: (bi, i, 0))
    prev = pl.BlockSpec((1, tq, SWA_KV_W), lambda bi, i: (bi, jnp.maximum(i - 1, 0), 0))
    return pl.pallas_call(
        functools.partial(_swa_attn_kernel, tq=tq),
        grid=(b, s // tq),
        in_specs=[
            smem,
            pl.BlockSpec((SWA_Q_HEADS, tq // 2, tq), lambda bi, i: (0, 0, 0)),
            pl.BlockSpec((1, tq, SWA_Q_W), lambda bi, i: (bi, i, 0)),
            cur, prev, cur, prev,
        ],
        out_specs=pl.BlockSpec((1, tq, SWA_Q_W), lambda bi, i: (bi, i, 0)),
        out_shape=jax.ShapeDtypeStruct((b, s, SWA_Q_W), F32),
        compiler_params=_params("parallel", "parallel"),
        name="swa_attn",
    )(sinks.astype(F32), _swa_bias(tq), q, k, k, v, v)


def _mix_ffn_kernel(ya_ref, yb_ref, yc_ref, x_ref, ga_ref, gb_ref, gc_ref, ln2_ref, mod_ref,
                    wa_ref, wb_ref, wc_ref, wg_ref, wu_ref, wd_ref, fg_ref, o_ref,
                    *, a_scale, final):
    mixed = _dot(_rms(yb_ref[0], gb_ref[...]).astype(BF16), wb_ref[...])
    mixed += _dot(_rms(yc_ref[0], gc_ref[...]).astype(BF16), wc_ref[...])
    packed = []
    for h in range(0, A_HEADS, 2):
        pair = []
        for yt in (ya_ref[0, h], ya_ref[0, h + 1]):
            ms = jnp.mean(yt * yt, axis=0, keepdims=True)
            pair.append(yt * lax.rsqrt(ms + EPS))
        packed.append(jnp.concatenate(pair, axis=0).T)
    na = jnp.concatenate(packed, axis=1) * ga_ref[...] * a_scale
    mixed += _dot(na.astype(BF16), wa_ref[...])
    x1 = x_ref[0] + mod_ref[0, 2:3, :] * mixed
    h2 = (_rms(x1, ln2_ref[...]) * (1.0 + mod_ref[0, 4:5, :]) + mod_ref[0, 3:4, :]).astype(BF16)
    gate = _dot(h2, wg_ref[...])
    up = _dot(h2, wu_ref[...])
    act = gate * jax.nn.sigmoid(gate) * up
    out = x1 + mod_ref[0, 5:6, :] * _dot(act.astype(BF16), wd_ref[...])
    if final:
        out = _rms(out, fg_ref[...])
    o_ref[0] = out


def _mix_ffn(ya, yb, yc, x, ga, gb, gc, ln2, mod, wa, wb, wc, wg, wu, wd, fg, a_scale, final):
    b, s, d = x.shape
    f = wg.shape[1]
    tm = min(FFN_TILE, s)
    row = lambda wd_: pl.BlockSpec((1, tm, wd_), lambda bi, i: (bi, i, 0))
    const = lambda r, cdim: pl.BlockSpec((r, cdim), lambda bi, i: (0, 0),
                                         pipeline_mode=pl.Buffered(1))
    return pl.pallas_call(
        functools.partial(_mix_ffn_kernel, a_scale=a_scale, final=final),
        grid=(b, s // tm),
        in_specs=[
            pl.BlockSpec((1, A_HEADS, A_V, tm), lambda bi, i: (bi, 0, 0, i)),
            row(SB_W), row(SWA_Q_W), row(d),
            const(1, A_W), const(1, SB_W), const(1, SWA_Q_W), const(1, d),
            pl.BlockSpec((1, N_MOD, d), lambda bi, i: (bi, 0, 0)),
            const(A_W, d), const(SB_W, d), const(SWA_Q_W, d),
            const(d, f), const(d, f), const(f, d), const(1, d),
        ],
        out_specs=row(d),
        out_shape=jax.ShapeDtypeStruct((b, s, d), F32),
        compiler_params=_params("parallel", "parallel"),
        name="mix_ffn",
    )(ya, yb, yc, x, ga, gb, gc, ln2.reshape(1, d), mod, wa, wb, wc, wg, wu, wd, fg.reshape(1, d))


def _swa_perm():
    cols = []
    for t in range(SWA_REP):
        for g in range(SWA_KV_HEADS):
            head = g * SWA_REP + t
            cols.extend(range(head * SWA_DIM, (head + 1) * SWA_DIM))
    return jnp.asarray(cols, jnp.int32)


def _prep_w_in(w):
    bounds = np.cumsum((0, A_W, A_W, A_W, SB_W, SB_W, SB_W, SWA_Q_W, SWA_KV_W, SWA_KV_W))
    qa, ka, va, qb, kb, vb, qc, kc, vc = [w[:, bounds[n]:bounds[n + 1]] for n in range(9)]
    row = jnp.concatenate([
        ka, qb * (SB_DIM ** -0.5 * LOG2E), kb, vb,
        (qc * (SWA_DIM ** -0.5 * LOG2E))[:, _swa_perm()], kc, vc], axis=1)
    transposed = jnp.concatenate([
        qa * (A_QK ** -0.5 * LOG2E), va], axis=1).T
    return row.astype(BF16), transposed.astype(BF16)


def kernel(x, c, ln1_g, ln2_g, w_mod, b_mod, w_in, lam_q1, lam_k1, lam_q2, lam_k2, diff_norm_g,
           sb_norm_g, swa_norm_g, swa_sinks, w_out, w_gate, w_up, w_down, final_g):
    depth = w_in.shape[0]
    perm = _swa_perm()
    mod = _modulation(c, w_mod, b_mod)
    for l in range(depth):
        lam_init = 0.8 - 0.6 * math.exp(-0.3 * l)
        w_row, w_t = _prep_w_in(w_in[l])
        ka, qb, kb, vb, qc, kc, vc, qta, vta, kn = _in_proj(x, ln1_g[l], mod[l], w_row, w_t)
        norms = kn[:, :, 0, :2 * A_HEADS].transpose(0, 2, 1)
        k_norms, q_norms = norms[:, :A_HEADS].reshape(-1), norms[:, A_HEADS:].reshape(-1)
        ya = _diff_attn(qta, ka, vta, k_norms, q_norms, lam_q1[l], lam_k1[l], lam_q2[l], lam_k2[l], lam_init)
        yb = _sb_attn(qb, kb, vb)
        yc = _swa_attn(qc, kc, vc, swa_sinks[l])
        wo = w_out[l].astype(BF16)
        x = _mix_ffn(
            ya, yb, yc, x,
            jnp.tile(diff_norm_g[l], A_HEADS).reshape(1, A_W),
            sb_norm_g[l].reshape(1, SB_W),
            swa_norm_g[l][perm].reshape(1, SWA_Q_W),
            ln2_g[l], mod[l],
            wo[:A_W], wo[A_W:A_W + SB_W], wo[A_W + SB_W:][perm],
            w_gate[l].astype(BF16), w_up[l].astype(BF16), w_down[l].astype(BF16), final_g,
            1.0 - lam_init, l == depth - 1)
    return x
```

```python
import functools
import math

import numpy as np
import jax
import jax.numpy as jnp
from jax import lax
from jax.experimental import pallas as pl
from jax.experimental.pallas import tpu as pltpu

F32 = jnp.float32
BF16 = jnp.bfloat16

N_MOD = 6
EPS = 1e-6
A_HEADS = 4
A_QK = 32
A_V = 64
SB_HEADS = 4
SB_DIM = 64
SWA_Q_HEADS = 8
SWA_KV_HEADS = 2
SWA_REP = 4
SWA_DIM = 64
WINDOW = 128
A_W = 256
SB_W = 256
SWA_Q_W = 512
SWA_KV_W = 128

LANES = 128
NEG = -1e30

VMEM_LIMIT = 56 * 1024 * 1024

ROW_TILE = 512
FFN_TILE = 512
ATT_TILE = 256
A_TILE = ROW_TILE

LOG2E = math.log2(math.e)
A_AUG = 2 * A_QK
A_REF = A_AUG + 16
A_PIECES = 4
A_VT = 80
A_KW = A_HEADS * LANES
A_TW = A_HEADS * (LANES + A_VT)
SKIP_LOG2 = -160.0
FAST_LOG2 = 80.0
A_GROUPS = (4, 2)
ZERO_REF_LOG2 = 60.0
BOUND_MARGIN = 1.0


def _params(*sem):
    return pltpu.CompilerParams(dimension_semantics=sem, vmem_limit_bytes=VMEM_LIMIT)


def _nt_dot(a, b):
    return lax.dot_general(a, b, (((1,), (1,)), ((), ())), preferred_element_type=F32)


def _dot(a, b):
    return jnp.dot(a, b, preferred_element_type=F32)


def _rms(x, g):
    ms = jnp.mean(x * x, axis=-1, keepdims=True)
    return x * lax.rsqrt(ms + EPS) * g


def _alibi_slopes(n):
    return 2.0 ** (-8.0 * np.arange(1, n + 1, dtype=np.float64) / n)


def _split_bf16(x, n):
    pieces = []
    for _ in range(n - 1):
        p = x.astype(BF16)
        pieces.append(p)
        x = x - p.astype(F32)
    pieces.append(x.astype(BF16))
    return pieces


def _mod_kernel(c_ref, w_ref, b_ref, o_ref):
    cv = c_ref[...]
    ca = cv * jax.nn.sigmoid(cv)
    o_ref[0] = jnp.dot(ca, w_ref[0], preferred_element_type=F32,
                       precision=lax.Precision.HIGHEST) + b_ref[0]


def _modulation(c, w_mod, b_mod):
    depth, d, n = w_mod.shape
    b = c.shape[0]
    rows = 8
    cp = jnp.zeros((rows, d), F32).at[:b].set(c)
    tn = 1024
    out = pl.pallas_call(
        _mod_kernel,
        grid=(depth, n // tn),
        in_specs=[
            pl.BlockSpec((rows, d), lambda l, j: (0, 0)),
            pl.BlockSpec((1, d, tn), lambda l, j: (l, 0, j)),
            pl.BlockSpec((1, 1, tn), lambda l, j: (l, 0, j)),
        ],
        out_specs=pl.BlockSpec((1, rows, tn), lambda l, j: (l, 0, j)),
        out_shape=jax.ShapeDtypeStruct((depth, rows, n), F32),
        compiler_params=_params("parallel", "parallel"),
        name="modulation",
    )(cp, w_mod, b_mod.reshape(depth, 1, n))
    return out[:, :b].reshape(depth, b, N_MOD, d)


ROW_SPLITS = (A_KW, SB_W, SB_W, SB_W, SWA_Q_W, SWA_KV_W, SWA_KV_W)


def _in_proj_kernel(x_ref, g_ref, mod_ref, w_ref, wt_ref, kaug_ref, taug_ref, *out_refs):
    row_refs, (qt_ref, vt_ref, kn_ref) = out_refs[:len(ROW_SPLITS)], out_refs[len(ROW_SPLITS):]
    x = x_ref[0]
    h = _rms(x, g_ref[...]) * (1.0 + mod_ref[0, 1:2, :]) + mod_ref[0, 0:1, :]
    hb = h.astype(BF16)
    proj = _dot(hb, w_ref[...])
    start = A_W
    for ref, width in zip(row_refs[1:], ROW_SPLITS[1:]):
        ref[0] = proj[:, start:start + width].astype(BF16)
        start += width
    feat = lax.broadcasted_iota(jnp.int32, (x.shape[0], LANES), 1)
    out_lane = lax.broadcasted_iota(jnp.int32, kn_ref.shape[2:], 1)
    norms = jnp.zeros(kn_ref.shape[2:], F32)
    for hd in range(A_HEADS):
        pair = proj[:, (hd // 2) * LANES:(hd // 2 + 1) * LANES]
        if hd % 2:
            pair = pltpu.roll(pair, A_AUG, 1)
        cols = slice(hd * LANES, (hd + 1) * LANES)
        keys = (jnp.where(feat < A_AUG, pair, 0.0) + kaug_ref[:, cols]).astype(BF16)
        row_refs[0][0, :, cols] = keys
        kf = keys.astype(F32)
        sq = jnp.sum(jnp.where(feat < A_AUG, kf * kf, 0.0), axis=1, keepdims=True)
        norms = jnp.where(out_lane == hd, jnp.sqrt(jnp.max(sq, axis=0, keepdims=True)), norms)
    proj_t = _nt_dot(wt_ref[...], hb)
    for hd in range(A_HEADS):
        q_rows = proj_t[hd * A_AUG:(hd + 1) * A_AUG].astype(BF16)
        q_const = taug_ref[hd * LANES + A_AUG:(hd + 1) * LANES, :].astype(BF16)
        qt_ref[0, hd, 0] = jnp.concatenate([q_rows, q_const], axis=0)
        qsq = q_rows.astype(F32) * q_rows.astype(F32)
        per_map = jnp.maximum(jnp.sum(qsq[:A_QK], axis=0, keepdims=True),
                              jnp.sum(qsq[A_QK:], axis=0, keepdims=True))
        q_norm = jnp.sqrt(jnp.max(per_map, axis=1, keepdims=True))
        norms = jnp.where(out_lane == A_HEADS + hd, q_norm, norms)
        v_rows = proj_t[A_W + hd * A_V:A_W + (hd + 1) * A_V]
        v_const = taug_ref[A_KW + hd * A_VT + A_V:A_KW + (hd + 1) * A_VT, :]
        vt_ref[0, hd, 0] = jnp.concatenate([v_rows, v_const], axis=0).astype(BF16)
    kn_ref[0, 0] = norms


def _bf16_pieces(value, n):
    pieces = []
    rest = float(value)
    for _ in range(n):
        p = float(np.asarray(rest, np.float32).astype(jnp.bfloat16).astype(np.float64))
        pieces.append(p)
        rest -= p
    return pieces


def _slopes_log2(hd):
    pieces = _bf16_pieces(_alibi_slopes(A_HEADS)[hd] * LOG2E, A_PIECES)
    return sum(pieces), pieces


def _alibi_constants(t):
    idx = np.arange(t)
    lo, hi = idx % 256, idx - idx % 256
    kaug = np.zeros((t, A_KW), np.float32)
    taug = np.zeros((A_TW, t), np.float32)
    for hd in range(A_HEADS):
        _, pieces = _slopes_log2(hd)
        k0 = hd * LANES + A_AUG
        for n, piece in enumerate(pieces):
            taug[k0 + n] = -lo
            taug[k0 + A_PIECES + n] = -hi
            kaug[:, k0 + n] = piece
            kaug[:, k0 + A_PIECES + n] = piece
            taug[k0 + 2 * A_PIECES + n] = piece
            taug[k0 + 3 * A_PIECES + n] = piece
            kaug[:, k0 + 2 * A_PIECES + n] = lo
            kaug[:, k0 + 3 * A_PIECES + n] = hi
        r0 = hd * LANES + A_REF
        kaug[:, r0:r0 + 3] = 1.0
        taug[A_KW + hd * A_VT + A_V] = 1.0
    return jnp.asarray(kaug), jnp.asarray(taug)


def _in_proj(x, g, mod, w, wt):
    b, s, d = x.shape
    tm = A_TILE
    nb = s // tm
    n = w.shape[1]
    kaug, taug = _alibi_constants(tm)
    const = lambda shape: pl.BlockSpec(shape, lambda bi, i: (0,) * len(shape))
    return pl.pallas_call(
        _in_proj_kernel,
        grid=(b, nb),
        in_specs=[
            pl.BlockSpec((1, tm, d), lambda bi, i: (bi, i, 0)),
            const((1, d)),
            pl.BlockSpec((1, N_MOD, d), lambda bi, i: (bi, 0, 0)),
            const((d, n)), const(wt.shape), const((tm, A_KW)), const((A_TW, tm)),
        ],
        out_specs=[pl.BlockSpec((1, tm, wd), lambda bi, i: (bi, i, 0)) for wd in ROW_SPLITS] + [
            pl.BlockSpec((1, A_HEADS, 1, LANES, tm), lambda bi, i: (bi, 0, i, 0, 0)),
            pl.BlockSpec((1, A_HEADS, 1, A_VT, tm), lambda bi, i: (bi, 0, i, 0, 0)),
            pl.BlockSpec((1, 1, 8, LANES), lambda bi, i: (bi, i, 0, 0)),
        ],
        out_shape=[jax.ShapeDtypeStruct((b, s, wd), BF16) for wd in ROW_SPLITS] + [
            jax.ShapeDtypeStruct((b, A_HEADS, nb, LANES, tm), BF16),
            jax.ShapeDtypeStruct((b, A_HEADS, nb, A_VT, tm), BF16),
            jax.ShapeDtypeStruct((b, nb, 8, LANES), F32),
        ],
        compiler_params=_params("parallel", "parallel"),
        name="in_proj",
    )(x, g.reshape(1, d), mod, w, wt, kaug, taug)


def _diff_attn_kernel(kn_ref, qn_ref, slopes_ref, lq1_ref, lk1_ref, lq2_ref, lk2_ref, qt_ref,
                      k_ref, vt_ref, o_ref, q_s, m_s, acc_s, *, t, nb, lam_init):
    bi = pl.program_id(0)
    hd = pl.program_id(1)
    i = pl.program_id(2)
    slope = slopes_ref[hd]
    qt = qt_ref[0, 0, 0]
    feat = lax.broadcasted_iota(jnp.int32, (LANES, t), 0)
    zero = jnp.zeros_like(qt)
    q_s[0] = jnp.where(jnp.logical_or(feat < A_QK, feat >= A_AUG), qt, zero)
    q_s[1] = jnp.where(feat >= A_QK, qt, zero)
    q_norm = qn_ref[(bi * A_HEADS + hd) * nb + i]
    lam = (jnp.exp(jnp.sum(lq1_ref[...] * lk1_ref[...], keepdims=True))
           - jnp.exp(jnp.sum(lq2_ref[...] * lk2_ref[...], keepdims=True)) + lam_init)
    key_minus_query = (lax.broadcasted_iota(jnp.int32, (t, t), 0)
                       - lax.broadcasted_iota(jnp.int32, (t, t), 1))
    causal = key_minus_query <= 0

    def reference_rows(j, a):
        off = slope * ((i - j) * t).astype(F32)
        row = lax.broadcasted_iota(jnp.int32, (16, t), 0)
        hi, mid, lo = [p.astype(F32) for p in _split_bf16(-(m_s[a] + off), 3)]
        tile = jnp.where(row == 0, hi, jnp.where(row == 1, mid, jnp.where(row == 2, lo, 0.0)))
        return tile.astype(BF16)

    def scores(j, a, ref=None):
        kj = k_ref[0, pl.ds(pl.multiple_of(j * t, t), t), :]
        qa = q_s[a]
        if ref is not None:
            qa = jnp.concatenate([qa[:A_REF], ref, qa[A_REF + 16:]], axis=0)
        return _dot(kj, qa)

    def exact_block(j, first):
        vtj = vt_ref[0, 0, j]

        @pl.loop(0, 2)
        def _(a):
            s = scores(j, a, None if first else reference_rows(j, a))
            if first:
                s = jnp.where(causal, s, NEG)
            top = jnp.max(s, axis=0, keepdims=True)
            shift = top if first else jnp.maximum(top, 0.0)
            pv = _dot(vtj, jnp.exp2(s - shift).astype(BF16))
            if first:
                acc_s[a] = pv
                m_s[a] = shift
            else:
                acc_s[a] = jnp.exp2(-shift) * acc_s[a] + pv
                m_s[a] = m_s[a] + shift

    def fast_weights(j, a, diagonal):
        s = scores(j, a, reference_rows(j, a))
        if diagonal:
            s = jnp.where(causal, s, NEG)
        return jnp.exp2(s).astype(BF16)

    def fast_group(j, size, from_diagonal=False):
        vts = jnp.concatenate([vt_ref[0, 0, j - u] for u in range(size)], axis=1)
        for a in range(2):
            ps = jnp.concatenate(
                [fast_weights(j - u, a, from_diagonal and u == 0) for u in range(size)], axis=0)
            acc_s[a] = acc_s[a] + _dot(vts, ps)

    def reach_of(j, m_low):
        k_norm = kn_ref[(bi * A_HEADS + hd) * nb + j]
        alibi = jnp.where(j == i, 0.0, slope * (t - (i - j) * t).astype(F32))
        return (q_norm * k_norm + alibi + BOUND_MARGIN) - m_low

    def all_fast(j, size, m_low):
        ok = jnp.bool_(True)
        for u in range(size):
            reach = reach_of(j - u, m_low)
            ok = jnp.logical_and(ok, jnp.logical_and(reach >= SKIP_LOG2, reach <= FAST_LOG2))
        return ok

    plain = q_norm * kn_ref[(bi * A_HEADS + hd) * nb + i] + BOUND_MARGIN <= ZERO_REF_LOG2

    def start_plain():
        m_s[...] = jnp.zeros(m_s.shape, F32)
        acc_s[...] = jnp.zeros(acc_s.shape, F32)
        lead = A_GROUPS[0]
        together = lax.cond(i + 1 >= lead, lambda: all_fast(i, lead, 0.0), lambda: jnp.bool_(False))
        lax.cond(together, lambda: fast_group(i, lead, True), lambda: fast_group(i, 1, True))
        return i + 1 - jnp.where(together, lead, 1), jnp.float32(0.0)

    def start_exact():
        exact_block(i, True)
        return i, jnp.min(m_s[...])

    left, m_low = lax.cond(plain, start_plain, start_exact)

    def single(j, m_low):
        reach = reach_of(j, m_low)

        def visit():
            def fast():
                fast_group(j, 1)
                return m_low

            def exact():
                exact_block(j, False)
                return jnp.min(m_s[...])

            return lax.cond(reach <= FAST_LOG2, fast, exact)

        return lax.cond(reach < SKIP_LOG2, lambda: m_low, visit)

    def grouped(size, left, m_low):
        def more(left):
            return lax.cond(left >= size, lambda: all_fast(left - 1, size, m_low),
                            lambda: jnp.bool_(False))

        def body(left):
            fast_group(left - 1, size)
            return left - size

        return lax.while_loop(more, body, left)

    for size in A_GROUPS:
        left = grouped(size, left, m_low)
    lax.fori_loop(0, left, lambda u, m: single(left - 1 - u, m), m_low)

    outs = [acc_s[a, :A_V, :] * (1.0 / acc_s[a, A_V:A_V + 1, :]) for a in range(2)]
    o_ref[0, 0] = outs[0] - lam * outs[1]


def _diff_attn(qt, k, vt, k_norms, q_norms, lq1, lk1, lq2, lk2, lam_init):
    b, s, _ = k.shape
    t = A_TILE
    nb = s // t
    slopes = jnp.asarray([_slopes_log2(hd)[0] for hd in range(A_HEADS)], F32)
    smem = pl.BlockSpec(memory_space=pltpu.SMEM)
    vec = pl.BlockSpec((1, A_QK), lambda bi, h, i: (0, 0))
    return pl.pallas_call(
        functools.partial(_diff_attn_kernel, t=t, nb=nb, lam_init=lam_init),
        grid=(b, A_HEADS, nb),
        in_specs=[
            smem, smem, smem, vec, vec, vec, vec,
            pl.BlockSpec((1, 1, 1, LANES, t), lambda bi, h, i: (bi, h, i, 0, 0)),
            pl.BlockSpec((1, s, LANES), lambda bi, h, i: (bi, 0, h)),
            pl.BlockSpec((1, 1, nb, A_VT, t), lambda bi, h, i: (bi, h, 0, 0, 0)),
        ],
        out_specs=pl.BlockSpec((1, 1, A_V, t), lambda bi, h, i: (bi, h, 0, i)),
        out_shape=jax.ShapeDtypeStruct((b, A_HEADS, A_V, s), F32),
        scratch_shapes=[
            pltpu.VMEM((2, LANES, t), BF16),
            pltpu.VMEM((2, 1, t), F32),
            pltpu.VMEM((2, A_VT, t), F32),
        ],
        compiler_params=_params("parallel", "parallel", "arbitrary"),
        name="diff_attn",
    )(k_norms, q_norms, slopes, lq1.reshape(1, A_QK), lk1.reshape(1, A_QK), lq2.reshape(1, A_QK),
      lk2.reshape(1, A_QK), qt, k, vt)


def _sb_attn_kernel(q_ref, k_ref, v_ref, o_ref, r_s, acc_s, *, tq):
    i = pl.program_id(2)
    q = q_ref[0]
    lane = lax.broadcasted_iota(jnp.int32, (tq, LANES), 1)
    qm = [jnp.where(lane < SB_DIM, q, jnp.zeros_like(q)),
          jnp.where(lane >= SB_DIM, q, jnp.zeros_like(q))]
    row = lax.broadcasted_iota(jnp.int32, (tq, tq), 0)
    col = lax.broadcasted_iota(jnp.int32, (tq, tq), 1)
    strict = col < row
    later = (row > col).astype(BF16)

    def keys(j):
        return k_ref[0, pl.ds(pl.multiple_of(j * tq, tq), tq), :]

    def values(j):
        return v_ref[0, pl.ds(pl.multiple_of(j * tq, tq), tq), :]

    def log_weights(kj, h, diagonal):
        z = _nt_dot(qm[h], kj)
        log_beta = jnp.minimum(z, 0.0) - jnp.log2(1.0 + jnp.exp2(-jnp.abs(z)))
        log_1mb = log_beta - z
        if diagonal:
            log_1mb = jnp.where(strict, log_1mb, 0.0)
        after = _dot(jnp.concatenate(_split_bf16(log_1mb, 2), axis=1), later2)
        return log_beta + after, jnp.sum(log_1mb, axis=1, keepdims=True)

    later2 = jnp.concatenate([later, later], axis=0)
    prev = jnp.maximum(i - 1, 0)
    has_prev = (i > 0).astype(F32)
    no_prev = jnp.where(i > 0, 0.0, NEG)
    k_diag, k_prev = keys(i), keys(prev)
    v_both = jnp.concatenate([values(i), values(prev)], axis=0)
    for h in range(2):
        lw_d, tot_d = log_weights(k_diag, h, True)
        lw_p, tot_p = log_weights(k_prev, h, False)
        a_d = jnp.where(strict, jnp.exp2(lw_d), 0.0)
        a_p = jnp.exp2(lw_p + (tot_d + no_prev))
        acc_s[h] = _dot(jnp.concatenate([a_d.astype(BF16), a_p.astype(BF16)], axis=1), v_both)
        r_s[h] = tot_d + tot_p * has_prev

    def cond(carry):
        j, live = carry
        return jnp.logical_and(j >= 0, live)

    def body(carry):
        j, _ = carry
        kj, vj = keys(j), values(j)
        for h in range(2):
            lw, tot = log_weights(kj, h, False)
            run = r_s[h]
            acc_s[h] = acc_s[h] + _dot(jnp.exp2(lw + run).astype(BF16), vj)
            r_s[h] = run + tot
        return j - 1, jnp.max(r_s[...]) > SKIP_LOG2

    lax.while_loop(cond, body, (i - 2, jnp.max(r_s[...]) > SKIP_LOG2))
    o_ref[0] = jnp.where(lane < SB_DIM, acc_s[0], acc_s[1])


def _sb_attn(q, k, v):
    b, s, _ = q.shape
    tq = min(ATT_TILE, s)
    pairs = SB_W // LANES
    return pl.pallas_call(
        functools.partial(_sb_attn_kernel, tq=tq),
        grid=(b, pairs, s // tq),
        in_specs=[
            pl.BlockSpec((1, tq, LANES), lambda bi, p, i: (bi, i, p)),
            pl.BlockSpec((1, s, LANES), lambda bi, p, i: (bi, 0, p)),
            pl.BlockSpec((1, s, LANES), lambda bi, p, i: (bi, 0, p)),
        ],
        out_specs=pl.BlockSpec((1, tq, LANES), lambda bi, p, i: (bi, i, p)),
        out_shape=jax.ShapeDtypeStruct((b, s, SB_W), F32),
        scratch_shapes=[
            pltpu.VMEM((2, tq, 1), F32),
            pltpu.VMEM((2, tq, LANES), F32),
        ],
        compiler_params=_params("parallel", "parallel", "arbitrary"),
        name="sb_attn",
    )(q, k, v)


def _swa_attn_kernel(sinks_ref, bias_ref, q_ref, kc_ref, kp_ref, vc_ref, vp_ref, o_ref, *, tq):
    i = pl.program_id(1)
    half = tq // 2
    col = lax.broadcasted_iota(jnp.int32, (half, tq), 1)
    has_key = col >= jnp.where(i > 0, 0, half)
    lane = lax.broadcasted_iota(jnp.int32, (half, LANES), 1)

    kwin = [jnp.concatenate([kp_ref[0, half:, :], kc_ref[0, :half, :]], axis=0), kc_ref[0]]
    vwin = [jnp.concatenate([vp_ref[0, half:, :], vc_ref[0, :half, :]], axis=0), vc_ref[0]]
    for hf in range(2):
        rows = slice(hf * half, (hf + 1) * half)
        for t in range(SWA_REP):
            q = q_ref[0, rows, t * LANES:(t + 1) * LANES]
            outs = []
            for g in range(SWA_KV_HEADS):
                head = g * SWA_REP + t
                qg = jnp.where((lane // SWA_DIM) == g, q, jnp.zeros_like(q))
                s = _nt_dot(qg, kwin[hf]) + bias_ref[head]
                if hf == 0:
                    s = jnp.where(has_key, s, NEG)
                sink = sinks_ref[head] * LOG2E
                m = jnp.maximum(jnp.max(s, axis=1, keepdims=True), sink)
                p = jnp.exp2(s - m)
                den = jnp.sum(p, axis=1, keepdims=True) + jnp.exp2(sink - m)
                outs.append(_dot(p.astype(BF16), vwin[hf]) / den)
            o_ref[0, rows, t * LANES:(t + 1) * LANES] = jnp.where(lane < SWA_DIM, outs[0], outs[1])


def _swa_bias(tq):
    half = tq // 2
    dist = np.arange(half)[:, None] + half - np.arange(tq)[None, :]
    in_window = (dist >= 0) & (dist < WINDOW)
    slopes = _alibi_slopes(SWA_Q_HEADS) * LOG2E
    bias = np.where(in_window[None], -slopes[:, None, None] * dist[None], NEG)
    return jnp.asarray(bias, F32)


def _swa_attn(q, k, v, sinks):
    b, s, _ = q.shape
    tq = min(ATT_TILE, s)
    smem = pl.BlockSpec(memory_space=pltpu.SMEM)
    cur = pl.BlockSpec((1, tq, SWA_KV_W), lambda bi, i: (bi, i, 0))
    prev = pl.BlockSpec((1, tq, SWA_KV_W), lambda bi, i: (bi, jnp.maximum(i - 1, 0), 0))
    return pl.pallas_call(
        functools.partial(_swa_attn_kernel, tq=tq),
        grid=(b, s // tq),
        in_specs=[
            smem,
            pl.BlockSpec((SWA_Q_HEADS, tq // 2, tq), lambda bi, i: (0, 0, 0)),
            pl.BlockSpec((1, tq, SWA_Q_W), lambda bi, i: (bi, i, 0)),
            cur, prev, cur, prev,
        ],
        out_specs=pl.BlockSpec((1, tq, SWA_Q_W), lambda bi, i: (bi, i, 0)),
        out_shape=jax.ShapeDtypeStruct((b, s, SWA_Q_W), F32),
        compiler_params=_params("parallel", "parallel"),
        name="swa_attn",
    )(sinks.astype(F32), _swa_bias(tq), q, k, k, v, v)


def _mix_ffn_kernel(ya_ref, yb_ref, yc_ref, x_ref, ga_ref, gb_ref, gc_ref, ln2_ref, mod_ref,
                    wa_ref, wb_ref, wc_ref, wg_ref, wu_ref, wd_ref, fg_ref, o_ref,
                    *, a_scale, final):
    mixed = _dot(_rms(yb_ref[0], gb_ref[...]).astype(BF16), wb_ref[...])
    mixed += _dot(_rms(yc_ref[0], gc_ref[...]).astype(BF16), wc_ref[...])
    packed = []
    for h in range(0, A_HEADS, 2):
        pair = []
        for yt in (ya_ref[0, h], ya_ref[0, h + 1]):
            ms = jnp.mean(yt * yt, axis=0, keepdims=True)
            pair.append(yt * lax.rsqrt(ms + EPS))
        packed.append(jnp.concatenate(pair, axis=0).T)
    na = jnp.concatenate(packed, axis=1) * ga_ref[...] * a_scale
    mixed += _dot(na.astype(BF16), wa_ref[...])
    x1 = x_ref[0] + mod_ref[0, 2:3, :] * mixed
    h2 = (_rms(x1, ln2_ref[...]) * (1.0 + mod_ref[0, 4:5, :]) + mod_ref[0, 3:4, :]).astype(BF16)
    gate = _dot(h2, wg_ref[...])
    up = _dot(h2, wu_ref[...])
    act = gate * jax.nn.sigmoid(gate) * up
    out = x1 + mod_ref[0, 5:6, :] * _dot(act.astype(BF16), wd_ref[...])
    if final:
        out = _rms(out, fg_ref[...])
    o_ref[0] = out


def _mix_ffn(ya, yb, yc, x, ga, gb, gc, ln2, mod, wa, wb, wc, wg, wu, wd, fg, a_scale, final):
    b, s, d = x.shape
    f = wg.shape[1]
    tm = min(FFN_TILE, s)
    row = lambda wd_: pl.BlockSpec((1, tm, wd_), lambda bi, i: (bi, i, 0))
    const = lambda r, cdim: pl.BlockSpec((r, cdim), lambda bi, i: (0, 0),
                                         pipeline_mode=pl.Buffered(1))
    return pl.pallas_call(
        functools.partial(_mix_ffn_kernel, a_scale=a_scale, final=final),
        grid=(b, s // tm),
        in_specs=[
            pl.BlockSpec((1, A_HEADS, A_V, tm), lambda bi, i: (bi, 0, 0, i)),
            row(SB_W), row(SWA_Q_W), row(d),
            const(1, A_W), const(1, SB_W), const(1, SWA_Q_W), const(1, d),
            pl.BlockSpec((1, N_MOD, d), lambda bi, i: (bi, 0, 0)),
            const(A_W, d), const(SB_W, d), const(SWA_Q_W, d),
            const(d, f), const(d, f), const(f, d), const(1, d),
        ],
        out_specs=row(d),
        out_shape=jax.ShapeDtypeStruct((b, s, d), F32),
        compiler_params=_params("parallel", "parallel"),
        name="mix_ffn",
    )(ya, yb, yc, x, ga, gb, gc, ln2.reshape(1, d), mod, wa, wb, wc, wg, wu, wd, fg.reshape(1, d))


def _swa_perm():
    cols = []
    for t in range(SWA_REP):
        for g in range(SWA_KV_HEADS):
            head = g * SWA_REP + t
            cols.extend(range(head * SWA_DIM, (head + 1) * SWA_DIM))
    return jnp.asarray(cols, jnp.int32)


def _prep_w_in(w):
    bounds = np.cumsum((0, A_W, A_W, A_W, SB_W, SB_W, SB_W, SWA_Q_W, SWA_KV_W, SWA_KV_W))
    qa, ka, va, qb, kb, vb, qc, kc, vc = [w[:, bounds[n]:bounds[n + 1]] for n in range(9)]
    row = jnp.concatenate([
        ka, qb * (SB_DIM ** -0.5 * LOG2E), kb, vb,
        (qc * (SWA_DIM ** -0.5 * LOG2E))[:, _swa_perm()], kc, vc], axis=1)
    transposed = jnp.concatenate([
        qa * (A_QK ** -0.5 * LOG2E), va], axis=1).T
    return row.astype(BF16), transposed.astype(BF16)


def kernel(x, c, ln1_g, ln2_g, w_mod, b_mod, w_in, lam_q1, lam_k1, lam_q2, lam_k2, diff_norm_g,
           sb_norm_g, swa_norm_g, swa_sinks, w_out, w_gate, w_up, w_down, final_g):
    depth = w_in.shape[0]
    perm = _swa_perm()
    mod = _modulation(c, w_mod, b_mod)
    for l in range(depth):
        lam_init = 0.8 - 0.6 * math.exp(-0.3 * l)
        w_row, w_t = _prep_w_in(w_in[l])
        ka, qb, kb, vb, qc, kc, vc, qta, vta, kn = _in_proj(x, ln1_g[l], mod[l], w_row, w_t)
        norms = kn[:, :, 0, :2 * A_HEADS].transpose(0, 2, 1)
        k_norms, q_norms = norms[:, :A_HEADS].reshape(-1), norms[:, A_HEADS:].reshape(-1)
        ya = _diff_attn(qta, ka, vta, k_norms, q_norms, lam_q1[l], lam_k1[l], lam_q2[l], lam_k2[l], lam_init)
        yb = _sb_attn(qb, kb, vb)
        yc = _swa_attn(qc, kc, vc, swa_sinks[l])
        wo = w_out[l].astype(BF16)
        x = _mix_ffn(
            ya, yb, yc, x,
            jnp.tile(diff_norm_g[l], A_HEADS).reshape(1, A_W),
            sb_norm_g[l].reshape(1, SB_W),
            swa_norm_g[l][perm].reshape(1, SWA_Q_W),
            ln2_g[l], mod[l],
            wo[:A_W], wo[A_W:A_W + SB_W], wo[A_W + SB_W:][perm],
            w_gate[l].astype(BF16), w_up[l].astype(BF16), w_down[l].astype(BF16), final_g,
            1.0 - lam_init, l == depth - 1)
    return x
```

```python
import functools
import math

import numpy as np
import jax
import jax.numpy as jnp
from jax import lax
from jax.experimental import pallas as pl
from jax.experimental.pallas import tpu as pltpu

F32 = jnp.float32
BF16 = jnp.bfloat16

N_MOD = 6
EPS = 1e-6
A_HEADS = 4
A_QK = 32
A_V = 64
SB_HEADS = 4
SB_DIM = 64
SWA_Q_HEADS = 8
SWA_KV_HEADS = 2
SWA_REP = 4
SWA_DIM = 64
WINDOW = 128
A_W = 256
SB_W = 256
SWA_Q_W = 512
SWA_KV_W = 128

LANES = 128
NEG = -1e30

VMEM_LIMIT = 56 * 1024 * 1024

ROW_TILE = 512
FFN_TILE = 512
ATT_TILE = 256
A_TILE = ROW_TILE

LOG2E = math.log2(math.e)
A_AUG = 2 * A_QK
A_REF = A_AUG + 16
A_PIECES = 4
A_VT = 80
A_KW = A_HEADS * LANES
A_TW = A_HEADS * (LANES + A_VT)
SKIP_LOG2 = -160.0
FAST_LOG2 = 80.0
A_GROUPS = (4, 2)
ZERO_REF_LOG2 = 60.0
BOUND_MARGIN = 1.0


def _params(*sem):
    return pltpu.CompilerParams(dimension_semantics=sem, vmem_limit_bytes=VMEM_LIMIT)


def _nt_dot(a, b):
    return lax.dot_general(a, b, (((1,), (1,)), ((), ())), preferred_element_type=F32)


def _dot(a, b):
    return jnp.dot(a, b, preferred_element_type=F32)


def _rms(x, g):
    ms = jnp.mean(x * x, axis=-1, keepdims=True)
    return x * lax.rsqrt(ms + EPS) * g


def _alibi_slopes(n):
    return 2.0 ** (-8.0 * np.arange(1, n + 1, dtype=np.float64) / n)


def _split_bf16(x, n):
    pieces = []
    for _ in range(n - 1):
        p = x.astype(BF16)
        pieces.append(p)
        x = x - p.astype(F32)
    pieces.append(x.astype(BF16))
    return pieces


def _mod_kernel(c_ref, w_ref, b_ref, o_ref):
    cv = c_ref[...]
    ca = cv * jax.nn.sigmoid(cv)
    o_ref[0] = jnp.dot(ca, w_ref[0], preferred_element_type=F32,
                       precision=lax.Precision.HIGHEST) + b_ref[0]


def _modulation(c, w_mod, b_mod):
    depth, d, n = w_mod.shape
    b = c.shape[0]
    rows = 8
    cp = jnp.zeros((rows, d), F32).at[:b].set(c)
    tn = 1024
    out = pl.pallas_call(
        _mod_kernel,
        grid=(depth, n // tn),
        in_specs=[
            pl.BlockSpec((rows, d), lambda l, j: (0, 0)),
            pl.BlockSpec((1, d, tn), lambda l, j: (l, 0, j)),
            pl.BlockSpec((1, 1, tn), lambda l, j: (l, 0, j)),
        ],
        out_specs=pl.BlockSpec((1, rows, tn), lambda l, j: (l, 0, j)),
        out_shape=jax.ShapeDtypeStruct((depth, rows, n), F32),
        compiler_params=_params("parallel", "parallel"),
        name="modulation",
    )(cp, w_mod, b_mod.reshape(depth, 1, n))
    return out[:, :b].reshape(depth, b, N_MOD, d)


ROW_SPLITS = (A_KW, SB_W, SB_W, SB_W, SWA_Q_W, SWA_KV_W, SWA_KV_W)


def _in_proj_kernel(x_ref, g_ref, mod_ref, w_ref, wt_ref, kaug_ref, taug_ref, *out_refs):
    row_refs, (qt_ref, vt_ref, kn_ref) = out_refs[:len(ROW_SPLITS)], out_refs[len(ROW_SPLITS):]
    x = x_ref[0]
    h = _rms(x, g_ref[...]) * (1.0 + mod_ref[0, 1:2, :]) + mod_ref[0, 0:1, :]
    hb = h.astype(BF16)
    proj = _dot(hb, w_ref[...])
    start = A_W
    for ref, width in zip(row_refs[1:], ROW_SPLITS[1:]):
        ref[0] = proj[:, start:start + width].astype(BF16)
        start += width
    feat = lax.broadcasted_iota(jnp.int32, (x.shape[0], LANES), 1)
    out_lane = lax.broadcasted_iota(jnp.int32, kn_ref.shape[2:], 1)
    norms = jnp.zeros(kn_ref.shape[2:], F32)
    for hd in range(A_HEADS):
        pair = proj[:, (hd // 2) * LANES:(hd // 2 + 1) * LANES]
        if hd % 2:
            pair = pltpu.roll(pair, A_AUG, 1)
        cols = slice(hd * LANES, (hd + 1) * LANES)
        keys = (jnp.where(feat < A_AUG, pair, 0.0) + kaug_ref[:, cols]).astype(BF16)
        row_refs[0][0, :, cols] = keys
        kf = keys.astype(F32)
        sq = jnp.sum(jnp.where(feat < A_AUG, kf * kf, 0.0), axis=1, keepdims=True)
        norms = jnp.where(out_lane == hd, jnp.sqrt(jnp.max(sq, axis=0, keepdims=True)), norms)
    proj_t = _nt_dot(wt_ref[...], hb)
    for hd in range(A_HEADS):
        q_rows = proj_t[hd * A_AUG:(hd + 1) * A_AUG].astype(BF16)
        q_const = taug_ref[hd * LANES + A_AUG:(hd + 1) * LANES, :].astype(BF16)
        qt_ref[0, hd, 0] = jnp.concatenate([q_rows, q_const], axis=0)
        qsq = q_rows.astype(F32) * q_rows.astype(F32)
        per_map = jnp.maximum(jnp.sum(qsq[:A_QK], axis=0, keepdims=True),
                              jnp.sum(qsq[A_QK:], axis=0, keepdims=True))
        q_norm = jnp.sqrt(jnp.max(per_map, axis=1, keepdims=True))
        norms = jnp.where(out_lane == A_HEADS + hd, q_norm, norms)
        v_rows = proj_t[A_W + hd * A_V:A_W + (hd + 1) * A_V]
        v_const = taug_ref[A_KW + hd * A_VT + A_V:A_KW + (hd + 1) * A_VT, :]
        vt_ref[0, hd, 0] = jnp.concatenate([v_rows, v_const], axis=0).astype(BF16)
    kn_ref[0, 0] = norms


def _bf16_pieces(value, n):
    pieces = []
    rest = float(value)
    for _ in range(n):
        p = float(np.asarray(rest, np.float32).astype(jnp.bfloat16).astype(np.float64))
        pieces.append(p)
        rest -= p
    return pieces


def _slopes_log2(hd):
    pieces = _bf16_pieces(_alibi_slopes(A_HEADS)[hd] * LOG2E, A_PIECES)
    return sum(pieces), pieces


def _alibi_constants(t):
    idx = np.arange(t)
    lo, hi = idx % 256, idx - idx % 256
    kaug = np.zeros((t, A_KW), np.float32)
    taug = np.zeros((A_TW, t), np.float32)
    for hd in range(A_HEADS):
        _, pieces = _slopes_log2(hd)
        k0 = hd * LANES + A_AUG
        for n, piece in enumerate(pieces):
            taug[k0 + n] = -lo
            taug[k0 + A_PIECES + n] = -hi
            kaug[:, k0 + n] = piece
            kaug[:, k0 + A_PIECES + n] = piece
            taug[k0 + 2 * A_PIECES + n] = piece
            taug[k0 + 3 * A_PIECES + n] = piece
            kaug[:, k0 + 2 * A_PIECES + n] = lo
            kaug[:, k0 + 3 * A_PIECES + n] = hi
        r0 = hd * LANES + A_REF
        kaug[:, r0:r0 + 3] = 1.0
        taug[A_KW + hd * A_VT + A_V] = 1.0
    return jnp.asarray(kaug), jnp.asarray(taug)


def _in_proj(x, g, mod, w, wt):
    b, s, d = x.shape
    tm = A_TILE
    nb = s // tm
    n = w.shape[1]
    kaug, taug = _alibi_constants(tm)
    const = lambda shape: pl.BlockSpec(shape, lambda bi, i: (0,) * len(shape))
    return pl.pallas_call(
        _in_proj_kernel,
        grid=(b, nb),
        in_specs=[
            pl.BlockSpec((1, tm, d), lambda bi, i: (bi, i, 0)),
            const((1, d)),
            pl.BlockSpec((1, N_MOD, d), lambda bi, i: (bi, 0, 0)),
            const((d, n)), const(wt.shape), const((tm, A_KW)), const((A_TW, tm)),
        ],
        out_specs=[pl.BlockSpec((1, tm, wd), lambda bi, i: (bi, i, 0)) for wd in ROW_SPLITS] + [
            pl.BlockSpec((1, A_HEADS, 1, LANES, tm), lambda bi, i: (bi, 0, i, 0, 0)),
            pl.BlockSpec((1, A_HEADS, 1, A_VT, tm), lambda bi, i: (bi, 0, i, 0, 0)),
            pl.BlockSpec((1, 1, 8, LANES), lambda bi, i: (bi, i, 0, 0)),
        ],
        out_shape=[jax.ShapeDtypeStruct((b, s, wd), BF16) for wd in ROW_SPLITS] + [
            jax.ShapeDtypeStruct((b, A_HEADS, nb, LANES, tm), BF16),
            jax.ShapeDtypeStruct((b, A_HEADS, nb, A_VT, tm), BF16),
            jax.ShapeDtypeStruct((b, nb, 8, LANES), F32),
        ],
        compiler_params=_params("parallel", "parallel"),
        name="in_proj",
    )(x, g.reshape(1, d), mod, w, wt, kaug, taug)


def _diff_attn_kernel(kn_ref, qn_ref, slopes_ref, lq1_ref, lk1_ref, lq2_ref, lk2_ref, qt_ref,
                      k_ref, vt_ref, o_ref, q_s, m_s, acc_s, *, t, nb, lam_init):
    bi = pl.program_id(0)
    hd = pl.program_id(1)
    i = pl.program_id(2)
    slope = slopes_ref[hd]
    qt = qt_ref[0, 0, 0]
    feat = lax.broadcasted_iota(jnp.int32, (LANES, t), 0)
    zero = jnp.zeros_like(qt)
    q_s[0] = jnp.where(jnp.logical_or(feat < A_QK, feat >= A_AUG), qt, zero)
    q_s[1] = jnp.where(feat >= A_QK, qt, zero)
    q_norm = qn_ref[(bi * A_HEADS + hd) * nb + i]
    lam = (jnp.exp(jnp.sum(lq1_ref[...] * lk1_ref[...], keepdims=True))
           - jnp.exp(jnp.sum(lq2_ref[...] * lk2_ref[...], keepdims=True)) + lam_init)
    key_minus_query = (lax.broadcasted_iota(jnp.int32, (t, t), 0)
                       - lax.broadcasted_iota(jnp.int32, (t, t), 1))
    causal = key_minus_query <= 0

    def reference_rows(j, a):
        off = slope * ((i - j) * t).astype(F32)
        row = lax.broadcasted_iota(jnp.int32, (16, t), 0)
        hi, mid, lo = [p.astype(F32) for p in _split_bf16(-(m_s[a] + off), 3)]
        tile = jnp.where(row == 0, hi, jnp.where(row == 1, mid, jnp.where(row == 2, lo, 0.0)))
        return tile.astype(BF16)

    def scores(j, a, ref=None):
        kj = k_ref[0, pl.ds(pl.multiple_of(j * t, t), t), :]
        qa = q_s[a]
        if ref is not None:
            qa = jnp.concatenate([qa[:A_REF], ref, qa[A_REF + 16:]], axis=0)
        return _dot(kj, qa)

    def exact_block(j, first):
        vtj = vt_ref[0, 0, j]

        @pl.loop(0, 2)
        def _(a):
            s = scores(j, a, None if first else reference_rows(j, a))
            if first:
                s = jnp.where(causal, s, NEG)
            top = jnp.max(s, axis=0, keepdims=True)
            shift = top if first else jnp.maximum(top, 0.0)
            pv = _dot(vtj, jnp.exp2(s - shift).astype(BF16))
            if first:
                acc_s[a] = pv
                m_s[a] = shift
            else:
                acc_s[a] = jnp.exp2(-shift) * acc_s[a] + pv
                m_s[a] = m_s[a] + shift

    def fast_weights(j, a, diagonal):
        s = scores(j, a, reference_rows(j, a))
        if diagonal:
            s = jnp.where(causal, s, NEG)
        return jnp.exp2(s).astype(BF16)

    def fast_group(j, size, from_diagonal=False):
        vts = jnp.concatenate([vt_ref[0, 0, j - u] for u in range(size)], axis=1)
        for a in range(2):
            ps = jnp.concatenate(
                [fast_weights(j - u, a, from_diagonal and u == 0) for u in range(size)], axis=0)
            acc_s[a] = acc_s[a] + _dot(vts, ps)

    def reach_of(j, m_low):
        k_norm = kn_ref[(bi * A_HEADS + hd) * nb + j]
        alibi = jnp.where(j == i, 0.0, slope * (t - (i - j) * t).astype(F32))
        return (q_norm * k_norm + alibi + BOUND_MARGIN) - m_low

    def all_fast(j, size, m_low):
        ok = jnp.bool_(True)
        for u in range(size):
            reach = reach_of(j - u, m_low)
            ok = jnp.logical_and(ok, jnp.logical_and(reach >= SKIP_LOG2, reach <= FAST_LOG2))
        return ok

    plain = q_norm * kn_ref[(bi * A_HEADS + hd) * nb + i] + BOUND_MARGIN <= ZERO_REF_LOG2

    def start_plain():
        m_s[...] = jnp.zeros(m_s.shape, F32)
        acc_s[...] = jnp.zeros(acc_s.shape, F32)
        taken = jnp.int32(1)
        for size in reversed(A_GROUPS):
            fits = lax.cond(i + 1 >= size, functools.partial(all_fast, i, size, 0.0),
                            lambda: jnp.bool_(False))
            taken = jnp.where(fits, size, taken)
        lax.switch(sum((taken >= size).astype(jnp.int32) for size in A_GROUPS),
                   [functools.partial(fast_group, i, size, True)
                    for size in (1,) + tuple(reversed(A_GROUPS))])
        return i + 1 - taken, jnp.float32(0.0)

    def start_exact():
        exact_block(i, True)
        return i, jnp.min(m_s[...])

    left, m_low = lax.cond(plain, start_plain, start_exact)

    def single(j, m_low):
        reach = reach_of(j, m_low)

        def visit():
            def fast():
                fast_group(j, 1)
                return m_low

            def exact():
                exact_block(j, False)
                return jnp.min(m_s[...])

            return lax.cond(reach <= FAST_LOG2, fast, exact)

        return lax.cond(reach < SKIP_LOG2, lambda: m_low, visit)

    def grouped(size, left, m_low):
        def more(left):
            return lax.cond(left >= size, lambda: all_fast(left - 1, size, m_low),
                            lambda: jnp.bool_(False))

        def body(left):
            fast_group(left - 1, size)
            return left - size

        return lax.while_loop(more, body, left)

    for size in A_GROUPS:
        left = grouped(size, left, m_low)
    lax.fori_loop(0, left, lambda u, m: single(left - 1 - u, m), m_low)

    outs = [acc_s[a, :A_V, :] * (1.0 / acc_s[a, A_V:A_V + 1, :]) for a in range(2)]
    o_ref[0, 0] = outs[0] - lam * outs[1]


def _diff_attn(qt, k, vt, k_norms, q_norms, lq1, lk1, lq2, lk2, lam_init):
    b, s, _ = k.shape
    t = A_TILE
    nb = s // t
    slopes = jnp.asarray([_slopes_log2(hd)[0] for hd in range(A_HEADS)], F32)
    smem = pl.BlockSpec(memory_space=pltpu.SMEM)
    vec = pl.BlockSpec((1, A_QK), lambda bi, h, i: (0, 0))
    return pl.pallas_call(
        functools.partial(_diff_attn_kernel, t=t, nb=nb, lam_init=lam_init),
        grid=(b, A_HEADS, nb),
        in_specs=[
            smem, smem, smem, vec, vec, vec, vec,
            pl.BlockSpec((1, 1, 1, LANES, t), lambda bi, h, i: (bi, h, i, 0, 0)),
            pl.BlockSpec((1, s, LANES), lambda bi, h, i: (bi, 0, h)),
            pl.BlockSpec((1, 1, nb, A_VT, t), lambda bi, h, i: (bi, h, 0, 0, 0)),
        ],
        out_specs=pl.BlockSpec((1, 1, A_V, t), lambda bi, h, i: (bi, h, 0, i)),
        out_shape=jax.ShapeDtypeStruct((b, A_HEADS, A_V, s), F32),
        scratch_shapes=[
            pltpu.VMEM((2, LANES, t), BF16),
            pltpu.VMEM((2, 1, t), F32),
            pltpu.VMEM((2, A_VT, t), F32),
        ],
        compiler_params=_params("parallel", "parallel", "arbitrary"),
        name="diff_attn",
    )(k_norms, q_norms, slopes, lq1.reshape(1, A_QK), lk1.reshape(1, A_QK), lq2.reshape(1, A_QK),
      lk2.reshape(1, A_QK), qt, k, vt)


def _sb_attn_kernel(q_ref, k_ref, v_ref, o_ref, r_s, acc_s, *, tq):
    i = pl.program_id(2)
    q = q_ref[0]
    lane = lax.broadcasted_iota(jnp.int32, (tq, LANES), 1)
    qm = [jnp.where(lane < SB_DIM, q, jnp.zeros_like(q)),
          jnp.where(lane >= SB_DIM, q, jnp.zeros_like(q))]
    row = lax.broadcasted_iota(jnp.int32, (tq, tq), 0)
    col = lax.broadcasted_iota(jnp.int32, (tq, tq), 1)
    strict = col < row
    later = (row > col).astype(BF16)

    def keys(j):
        return k_ref[0, pl.ds(pl.multiple_of(j * tq, tq), tq), :]

    def values(j):
        return v_ref[0, pl.ds(pl.multiple_of(j * tq, tq), tq), :]

    def log_weights(kj, h, diagonal):
        z = _nt_dot(qm[h], kj)
        log_beta = jnp.minimum(z, 0.0) - jnp.log2(1.0 + jnp.exp2(-jnp.abs(z)))
        log_1mb = log_beta - z
        if diagonal:
            log_1mb = jnp.where(strict, log_1mb, 0.0)
        after = _dot(jnp.concatenate(_split_bf16(log_1mb, 2), axis=1), later2)
        return log_beta + after, jnp.sum(log_1mb, axis=1, keepdims=True)

    later2 = jnp.concatenate([later, later], axis=0)
    prev = jnp.maximum(i - 1, 0)
    has_prev = (i > 0).astype(F32)
    no_prev = jnp.where(i > 0, 0.0, NEG)
    k_diag, k_prev = keys(i), keys(prev)
    v_both = jnp.concatenate([values(i), values(prev)], axis=0)
    for h in range(2):
        lw_d, tot_d = log_weights(k_diag, h, True)
        lw_p, tot_p = log_weights(k_prev, h, False)
        a_d = jnp.where(strict, jnp.exp2(lw_d), 0.0)
        a_p = jnp.exp2(lw_p + (tot_d + no_prev))
        acc_s[h] = _dot(jnp.concatenate([a_d.astype(BF16), a_p.astype(BF16)], axis=1), v_both)
        r_s[h] = tot_d + tot_p * has_prev

    def cond(carry):
        j, live = carry
        return jnp.logical_and(j >= 0, live)

    def body(carry):
        j, _ = carry
        kj, vj = keys(j), values(j)
        for h in range(2):
            lw, tot = log_weights(kj, h, False)
            run = r_s[h]
            acc_s[h] = acc_s[h] + _dot(jnp.exp2(lw + run).astype(BF16), vj)
            r_s[h] = run + tot
        return j - 1, jnp.max(r_s[...]) > SKIP_LOG2

    lax.while_loop(cond, body, (i - 2, jnp.max(r_s[...]) > SKIP_LOG2))
    o_ref[0] = jnp.where(lane < SB_DIM, acc_s[0], acc_s[1])


def _sb_attn(q, k, v):
    b, s, _ = q.shape
    tq = min(ATT_TILE, s)
    pairs = SB_W // LANES
    return pl.pallas_call(
        functools.partial(_sb_attn_kernel, tq=tq),
        grid=(b, pairs, s // tq),
        in_specs=[
            pl.BlockSpec((1, tq, LANES), lambda bi, p, i: (bi, i, p)),
            pl.BlockSpec((1, s, LANES), lambda bi, p, i: (bi, 0, p)),
            pl.BlockSpec((1, s, LANES), lambda bi, p, i: (bi, 0, p)),
        ],
        out_specs=pl.BlockSpec((1, tq, LANES), lambda bi, p, i: (bi, i, p)),
        out_shape=jax.ShapeDtypeStruct((b, s, SB_W), F32),
        scratch_shapes=[
            pltpu.VMEM((2, tq, 1), F32),
            pltpu.VMEM((2, tq, LANES), F32),
        ],
        compiler_params=_params("parallel", "parallel", "arbitrary"),
        name="sb_attn",
    )(q, k, v)


def _swa_attn_kernel(sinks_ref, bias_ref, q_ref, kc_ref, kp_ref, vc_ref, vp_ref, o_ref, *, tq):
    i = pl.program_id(1)
    half = tq // 2
    col = lax.broadcasted_iota(jnp.int32, (half, tq), 1)
    has_key = col >= jnp.where(i > 0, 0, half)
    lane = lax.broadcasted_iota(jnp.int32, (half, LANES), 1)

    kwin = [jnp.concatenate([kp_ref[0, half:, :], kc_ref[0, :half, :]], axis=0), kc_ref[0]]
    vwin = [jnp.concatenate([vp_ref[0, half:, :], vc_ref[0, :half, :]], axis=0), vc_ref[0]]
    for hf in range(2):
        rows = slice(hf * half, (hf + 1) * half)
        for t in range(SWA_REP):
            q = q_ref[0, rows, t * LANES:(t + 1) * LANES]
            outs = []
            for g in range(SWA_KV_HEADS):
                head = g * SWA_REP + t
                qg = jnp.where((lane // SWA_DIM) == g, q, jnp.zeros_like(q))
                s = _nt_dot(qg, kwin[hf]) + bias_ref[head]
                if hf == 0:
                    s = jnp.where(has_key, s, NEG)
                sink = sinks_ref[head] * LOG2E
                m = jnp.maximum(jnp.max(s, axis=1, keepdims=True), sink)
                p = jnp.exp2(s - m)
                den = jnp.sum(p, axis=1, keepdims=True) + jnp.exp2(sink - m)
                outs.append(_dot(p.astype(BF16), vwin[hf]) / den)
            o_ref[0, rows, t * LANES:(t + 1) * LANES] = jnp.where(lane < SWA_DIM, outs[0], outs[1])


def _swa_bias(tq):
    half = tq // 2
    dist = np.arange(half)[:, None] + half - np.arange(tq)[None, :]
    in_window = (dist >= 0) & (dist < WINDOW)
    slopes = _alibi_slopes(SWA_Q_HEADS) * LOG2E
    bias = np.where(in_window[None], -slopes[:, None, None] * dist[None], NEG)
    return jnp.asarray(bias, F32)


def _swa_attn(q, k, v, sinks):
    b, s, _ = q.shape
    tq = min(ATT_TILE, s)
    smem = pl.BlockSpec(memory_space=pltpu.SMEM)
    cur = pl.BlockSpec((1, tq, SWA_KV_W), lambda bi, i: (bi, i, 0))
    prev = pl.BlockSpec((1, tq, SWA_KV_W), lambda bi, i: (bi, jnp.maximum(i - 1, 0), 0))
    return pl.pallas_call(
        functools.partial(_swa_attn_kernel, tq=tq),
        grid=(b, s // tq),
        in_specs=[
            smem,
            pl.BlockSpec((SWA_Q_HEADS, tq // 2, tq), lambda bi, i: (0, 0, 0)),
            pl.BlockSpec((1, tq, SWA_Q_W), lambda bi, i: (bi, i, 0)),
            cur, prev, cur, prev,
        ],
        out_specs=pl.BlockSpec((1, tq, SWA_Q_W), lambda bi, i: (bi, i, 0)),
        out_shape=jax.ShapeDtypeStruct((b, s, SWA_Q_W), F32),
        compiler_params=_params("parallel", "parallel"),
        name="swa_attn",
    )(sinks.astype(F32), _swa_bias(tq), q, k, k, v, v)


def _mix_ffn_kernel(ya_ref, yb_ref, yc_ref, x_ref, ga_ref, gb_ref, gc_ref, ln2_ref, mod_ref,
                    wa_ref, wb_ref, wc_ref, wg_ref, wu_ref, wd_ref, fg_ref, o_ref,
                    *, a_scale, final):
    mixed = _dot(_rms(yb_ref[0], gb_ref[...]).astype(BF16), wb_ref[...])
    mixed += _dot(_rms(yc_ref[0], gc_ref[...]).astype(BF16), wc_ref[...])
    packed = []
    for h in range(0, A_HEADS, 2):
        pair = []
        for yt in (ya_ref[0, h], ya_ref[0, h + 1]):
            ms = jnp.mean(yt * yt, axis=0, keepdims=True)
            pair.append(yt * lax.rsqrt(ms + EPS))
        packed.append(jnp.concatenate(pair, axis=0).T)
    na = jnp.concatenate(packed, axis=1) * ga_ref[...] * a_scale
    mixed += _dot(na.astype(BF16), wa_ref[...])
    x1 = x_ref[0] + mod_ref[0, 2:3, :] * mixed
    h2 = (_rms(x1, ln2_ref[...]) * (1.0 + mod_ref[0, 4:5, :]) + mod_ref[0, 3:4, :]).astype(BF16)
    gate = _dot(h2, wg_ref[...])
    up = _dot(h2, wu_ref[...])
    act = gate * jax.nn.sigmoid(gate) * up
    out = x1 + mod_ref[0, 5:6, :] * _dot(act.astype(BF16), wd_ref[...])
    if final:
        out = _rms(out, fg_ref[...])
    o_ref[0] = out


def _mix_ffn(ya, yb, yc, x, ga, gb, gc, ln2, mod, wa, wb, wc, wg, wu, wd, fg, a_scale, final):
    b, s, d = x.shape
    f = wg.shape[1]
    tm = min(FFN_TILE, s)
    row = lambda wd_: pl.BlockSpec((1, tm, wd_), lambda bi, i: (bi, i, 0))
    const = lambda r, cdim: pl.BlockSpec((r, cdim), lambda bi, i: (0, 0),
                                         pipeline_mode=pl.Buffered(1))
    return pl.pallas_call(
        functools.partial(_mix_ffn_kernel, a_scale=a_scale, final=final),
        grid=(b, s // tm),
        in_specs=[
            pl.BlockSpec((1, A_HEADS, A_V, tm), lambda bi, i: (bi, 0, 0, i)),
            row(SB_W), row(SWA_Q_W), row(d),
            const(1, A_W), const(1, SB_W), const(1, SWA_Q_W), const(1, d),
            pl.BlockSpec((1, N_MOD, d), lambda bi, i: (bi, 0, 0)),
            const(A_W, d), const(SB_W, d), const(SWA_Q_W, d),
            const(d, f), const(d, f), const(f, d), const(1, d),
        ],
        out_specs=row(d),
        out_shape=jax.ShapeDtypeStruct((b, s, d), F32),
        compiler_params=_params("parallel", "parallel"),
        name="mix_ffn",
    )(ya, yb, yc, x, ga, gb, gc, ln2.reshape(1, d), mod, wa, wb, wc, wg, wu, wd, fg.reshape(1, d))


def _swa_perm():
    cols = []
    for t in range(SWA_REP):
        for g in range(SWA_KV_HEADS):
            head = g * SWA_REP + t
            cols.extend(range(head * SWA_DIM, (head + 1) * SWA_DIM))
    return jnp.asarray(cols, jnp.int32)


def _prep_w_in(w):
    bounds = np.cumsum((0, A_W, A_W, A_W, SB_W, SB_W, SB_W, SWA_Q_W, SWA_KV_W, SWA_KV_W))
    qa, ka, va, qb, kb, vb, qc, kc, vc = [w[:, bounds[n]:bounds[n + 1]] for n in range(9)]
    row = jnp.concatenate([
        ka, qb * (SB_DIM ** -0.5 * LOG2E), kb, vb,
        (qc * (SWA_DIM ** -0.5 * LOG2E))[:, _swa_perm()], kc, vc], axis=1)
    transposed = jnp.concatenate([
        qa * (A_QK ** -0.5 * LOG2E), va], axis=1).T
    return row.astype(BF16), transposed.astype(BF16)


def kernel(x, c, ln1_g, ln2_g, w_mod, b_mod, w_in, lam_q1, lam_k1, lam_q2, lam_k2, diff_norm_g,
           sb_norm_g, swa_norm_g, swa_sinks, w_out, w_gate, w_up, w_down, final_g):
    depth = w_in.shape[0]
    perm = _swa_perm()
    mod = _modulation(c, w_mod, b_mod)
    for l in range(depth):
        lam_init = 0.8 - 0.6 * math.exp(-0.3 * l)
        w_row, w_t = _prep_w_in(w_in[l])
        ka, qb, kb, vb, qc, kc, vc, qta, vta, kn = _in_proj(x, ln1_g[l], mod[l], w_row, w_t)
        norms = kn[:, :, 0, :2 * A_HEADS].transpose(0, 2, 1)
        k_norms, q_norms = norms[:, :A_HEADS].reshape(-1), norms[:, A_HEADS:].reshape(-1)
        ya = _diff_attn(qta, ka, vta, k_norms, q_norms, lam_q1[l], lam_k1[l], lam_q2[l], lam_k2[l], lam_init)
        yb = _sb_attn(qb, kb, vb)
        yc = _swa_attn(qc, kc, vc, swa_sinks[l])
        wo = w_out[l].astype(BF16)
        x = _mix_ffn(
            ya, yb, yc, x,
            jnp.tile(diff_norm_g[l], A_HEADS).reshape(1, A_W),
            sb_norm_g[l].reshape(1, SB_W),
            swa_norm_g[l][perm].reshape(1, SWA_Q_W),
            ln2_g[l], mod[l],
            wo[:A_W], wo[A_W:A_W + SB_W], wo[A_W + SB_W:][perm],
            w_gate[l].astype(BF16), w_up[l].astype(BF16), w_down[l].astype(BF16), final_g,
            1.0 - lam_init, l == depth - 1)
    return x
```

```python
import functools
import math

import numpy as np
import jax
import jax.numpy as jnp
from jax import lax
from jax.experimental import pallas as pl
from jax.experimental.pallas import tpu as pltpu

F32 = jnp.float32
BF16 = jnp.bfloat16

N_MOD = 6
EPS = 1e-6
A_HEADS = 4
A_QK = 32
A_V = 64
SB_HEADS = 4
SB_DIM = 64
SWA_Q_HEADS = 8
SWA_KV_HEADS = 2
SWA_REP = 4
SWA_DIM = 64
WINDOW = 128
A_W = 256
SB_W = 256
SWA_Q_W = 512
SWA_KV_W = 128

LANES = 128
NEG = -1e30

VMEM_LIMIT = 56 * 1024 * 1024

ROW_TILE = 512
FFN_TILE = 512
ATT_TILE = 256
A_TILE = ROW_TILE

LOG2E = math.log2(math.e)
A_AUG = 2 * A_QK
A_REF = A_AUG + 16
A_PIECES = 4
A_VT = 80
A_KW = A_HEADS * LANES
A_TW = A_HEADS * (LANES + A_VT)
SKIP_LOG2 = -160.0
FAST_LOG2 = 80.0
A_GROUPS = (4, 2)
ZERO_REF_LOG2 = 60.0
BOUND_MARGIN = 1.0


def _params(*sem):
    return pltpu.CompilerParams(dimension_semantics=sem, vmem_limit_bytes=VMEM_LIMIT)


def _nt_dot(a, b):
    return lax.dot_general(a, b, (((1,), (1,)), ((), ())), preferred_element_type=F32)


def _dot(a, b):
    return jnp.dot(a, b, preferred_element_type=F32)


def _rms(x, g):
    ms = jnp.mean(x * x, axis=-1, keepdims=True)
    return x * lax.rsqrt(ms + EPS) * g


def _alibi_slopes(n):
    return 2.0 ** (-8.0 * np.arange(1, n + 1, dtype=np.float64) / n)


def _split_bf16(x, n):
    pieces = []
    for _ in range(n - 1):
        p = x.astype(BF16)
        pieces.append(p)
        x = x - p.astype(F32)
    pieces.append(x.astype(BF16))
    return pieces


def _mod_kernel(c_ref, w_ref, b_ref, o_ref):
    cv = c_ref[...]
    ca = cv * jax.nn.sigmoid(cv)
    o_ref[0] = jnp.dot(ca, w_ref[0], preferred_element_type=F32,
                       precision=lax.Precision.HIGHEST) + b_ref[0]


def _modulation(c, w_mod, b_mod):
    depth, d, n = w_mod.shape
    b = c.shape[0]
    rows = 8
    cp = jnp.zeros((rows, d), F32).at[:b].set(c)
    tn = 1024
    out = pl.pallas_call(
        _mod_kernel,
        grid=(depth, n // tn),
        in_specs=[
            pl.BlockSpec((rows, d), lambda l, j: (0, 0)),
            pl.BlockSpec((1, d, tn), lambda l, j: (l, 0, j)),
            pl.BlockSpec((1, 1, tn), lambda l, j: (l, 0, j)),
        ],
        out_specs=pl.BlockSpec((1, rows, tn), lambda l, j: (l, 0, j)),
        out_shape=jax.ShapeDtypeStruct((depth, rows, n), F32),
        compiler_params=_params("parallel", "parallel"),
        name="modulation",
    )(cp, w_mod, b_mod.reshape(depth, 1, n))
    return out[:, :b].reshape(depth, b, N_MOD, d)


ROW_SPLITS = (A_KW, SB_W, SB_W, SB_W, SWA_Q_W, SWA_KV_W, SWA_KV_W)


def _in_proj_kernel(x_ref, g_ref, mod_ref, w_ref, wt_ref, kaug_ref, taug_ref, *out_refs):
    row_refs, (qt_ref, vt_ref, kn_ref) = out_refs[:len(ROW_SPLITS)], out_refs[len(ROW_SPLITS):]
    x = x_ref[0]
    h = _rms(x, g_ref[...]) * (1.0 + mod_ref[0, 1:2, :]) + mod_ref[0, 0:1, :]
    hb = h.astype(BF16)
    proj = _dot(hb, w_ref[...])
    start = A_W
    for ref, width in zip(row_refs[1:], ROW_SPLITS[1:]):
        ref[0] = proj[:, start:start + width].astype(BF16)
        start += width
    feat = lax.broadcasted_iota(jnp.int32, (x.shape[0], LANES), 1)
    out_lane = lax.broadcasted_iota(jnp.int32, kn_ref.shape[2:], 1)
    norms = jnp.zeros(kn_ref.shape[2:], F32)
    for hd in range(A_HEADS):
        pair = proj[:, (hd // 2) * LANES:(hd // 2 + 1) * LANES]
        if hd % 2:
            pair = pltpu.roll(pair, A_AUG, 1)
        cols = slice(hd * LANES, (hd + 1) * LANES)
        keys = (jnp.where(feat < A_AUG, pair, 0.0) + kaug_ref[:, cols]).astype(BF16)
        row_refs[0][0, :, cols] = keys
        kf = keys.astype(F32)
        sq = jnp.sum(jnp.where(feat < A_AUG, kf * kf, 0.0), axis=1, keepdims=True)
        norms = jnp.where(out_lane == hd, jnp.sqrt(jnp.max(sq, axis=0, keepdims=True)), norms)
    proj_t = _nt_dot(wt_ref[...], hb)
    for hd in range(A_HEADS):
        q_rows = proj_t[hd * A_AUG:(hd + 1) * A_AUG].astype(BF16)
        q_const = taug_ref[hd * LANES + A_AUG:(hd + 1) * LANES, :].astype(BF16)
        qt_ref[0, hd, 0] = jnp.concatenate([q_rows, q_const], axis=0)
        qsq = q_rows.astype(F32) * q_rows.astype(F32)
        per_map = jnp.maximum(jnp.sum(qsq[:A_QK], axis=0, keepdims=True),
                              jnp.sum(qsq[A_QK:], axis=0, keepdims=True))
        q_norm = jnp.sqrt(jnp.max(per_map, axis=1, keepdims=True))
        norms = jnp.where(out_lane == A_HEADS + hd, q_norm, norms)
        v_rows = proj_t[A_W + hd * A_V:A_W + (hd + 1) * A_V]
        v_const = taug_ref[A_KW + hd * A_VT + A_V:A_KW + (hd + 1) * A_VT, :]
        vt_ref[0, hd, 0] = jnp.concatenate([v_rows, v_const], axis=0).astype(BF16)
    kn_ref[0, 0] = norms


def _bf16_pieces(value, n):
    pieces = []
    rest = float(value)
    for _ in range(n):
        p = float(np.asarray(rest, np.float32).astype(jnp.bfloat16).astype(np.float64))
        pieces.append(p)
        rest -= p
    return pieces


def _slopes_log2(hd):
    pieces = _bf16_pieces(_alibi_slopes(A_HEADS)[hd] * LOG2E, A_PIECES)
    return sum(pieces), pieces


def _alibi_constants(t):
    idx = np.arange(t)
    lo, hi = idx % 256, idx - idx % 256
    kaug = np.zeros((t, A_KW), np.float32)
    taug = np.zeros((A_TW, t), np.float32)
    for hd in range(A_HEADS):
        _, pieces = _slopes_log2(hd)
        k0 = hd * LANES + A_AUG
        for n, piece in enumerate(pieces):
            taug[k0 + n] = -lo
            taug[k0 + A_PIECES + n] = -hi
            kaug[:, k0 + n] = piece
            kaug[:, k0 + A_PIECES + n] = piece
            taug[k0 + 2 * A_PIECES + n] = piece
            taug[k0 + 3 * A_PIECES + n] = piece
            kaug[:, k0 + 2 * A_PIECES + n] = lo
            kaug[:, k0 + 3 * A_PIECES + n] = hi
        r0 = hd * LANES + A_REF
        kaug[:, r0:r0 + 3] = 1.0
        taug[A_KW + hd * A_VT + A_V] = 1.0
    return jnp.asarray(kaug), jnp.asarray(taug)


def _in_proj(x, g, mod, w, wt):
    b, s, d = x.shape
    tm = A_TILE
    nb = s // tm
    n = w.shape[1]
    kaug, taug = _alibi_constants(tm)
    const = lambda shape: pl.BlockSpec(shape, lambda bi, i: (0,) * len(shape))
    return pl.pallas_call(
        _in_proj_kernel,
        grid=(b, nb),
        in_specs=[
            pl.BlockSpec((1, tm, d), lambda bi, i: (bi, i, 0)),
            const((1, d)),
            pl.BlockSpec((1, N_MOD, d), lambda bi, i: (bi, 0, 0)),
            const((d, n)), const(wt.shape), const((tm, A_KW)), const((A_TW, tm)),
        ],
        out_specs=[pl.BlockSpec((1, tm, wd), lambda bi, i: (bi, i, 0)) for wd in ROW_SPLITS] + [
            pl.BlockSpec((1, A_HEADS, 1, LANES, tm), lambda bi, i: (bi, 0, i, 0, 0)),
            pl.BlockSpec((1, A_HEADS, 1, A_VT, tm), lambda bi, i: (bi, 0, i, 0, 0)),
            pl.BlockSpec((1, 1, 8, LANES), lambda bi, i: (bi, i, 0, 0)),
        ],
        out_shape=[jax.ShapeDtypeStruct((b, s, wd), BF16) for wd in ROW_SPLITS] + [
            jax.ShapeDtypeStruct((b, A_HEADS, nb, LANES, tm), BF16),
            jax.ShapeDtypeStruct((b, A_HEADS, nb, A_VT, tm), BF16),
            jax.ShapeDtypeStruct((b, nb, 8, LANES), F32),
        ],
        compiler_params=_params("parallel", "parallel"),
        name="in_proj",
    )(x, g.reshape(1, d), mod, w, wt, kaug, taug)


def _diff_attn_kernel(kn_ref, qn_ref, slopes_ref, lq1_ref, lk1_ref, lq2_ref, lk2_ref, qt_ref,
                      k_ref, vt_ref, o_ref, q_s, m_s, acc_s, *, t, nb, lam_init):
    bi = pl.program_id(0)
    hd = pl.program_id(1)
    i = pl.program_id(2)
    slope = slopes_ref[hd]
    qt = qt_ref[0, 0, 0]
    feat = lax.broadcasted_iota(jnp.int32, (LANES, t), 0)
    zero = jnp.zeros_like(qt)
    q_s[0] = jnp.where(jnp.logical_or(feat < A_QK, feat >= A_AUG), qt, zero)
    q_s[1] = jnp.where(feat >= A_QK, qt, zero)
    q_norm = qn_ref[(bi * A_HEADS + hd) * nb + i]
    lam = (jnp.exp(jnp.sum(lq1_ref[...] * lk1_ref[...], keepdims=True))
           - jnp.exp(jnp.sum(lq2_ref[...] * lk2_ref[...], keepdims=True)) + lam_init)
    key_minus_query = (lax.broadcasted_iota(jnp.int32, (t, t), 0)
                       - lax.broadcasted_iota(jnp.int32, (t, t), 1))
    causal = key_minus_query <= 0

    def reference_rows(j, a):
        off = slope * ((i - j) * t).astype(F32)
        row = lax.broadcasted_iota(jnp.int32, (16, t), 0)
        hi, mid, lo = [p.astype(F32) for p in _split_bf16(-(m_s[a] + off), 3)]
        tile = jnp.where(row == 0, hi, jnp.where(row == 1, mid, jnp.where(row == 2, lo, 0.0)))
        return tile.astype(BF16)

    def scores(j, a, ref=None):
        kj = k_ref[0, pl.ds(pl.multiple_of(j * t, t), t), :]
        qa = q_s[a]
        if ref is not None:
            qa = jnp.concatenate([qa[:A_REF], ref, qa[A_REF + 16:]], axis=0)
        return _dot(kj, qa)

    def exact_block(j, first):
        vtj = vt_ref[0, 0, j]

        @pl.loop(0, 2)
        def _(a):
            s = scores(j, a, None if first else reference_rows(j, a))
            if first:
                s = jnp.where(causal, s, NEG)
            top = jnp.max(s, axis=0, keepdims=True)
            shift = top if first else jnp.maximum(top, 0.0)
            pv = _dot(vtj, jnp.exp2(s - shift).astype(BF16))
            if first:
                acc_s[a] = pv
                m_s[a] = shift
            else:
                acc_s[a] = jnp.exp2(-shift) * acc_s[a] + pv
                m_s[a] = m_s[a] + shift

    def fast_weights(j, a, diagonal):
        s = scores(j, a, reference_rows(j, a))
        if diagonal:
            s = jnp.where(causal, s, NEG)
        return jnp.exp2(s).astype(BF16)

    def fast_group(j, size, from_diagonal=False):
        vts = jnp.concatenate([vt_ref[0, 0, j - u] for u in range(size)], axis=1)
        for a in range(2):
            ps = jnp.concatenate(
                [fast_weights(j - u, a, from_diagonal and u == 0) for u in range(size)], axis=0)
            acc_s[a] = acc_s[a] + _dot(vts, ps)

    def reach_of(j, m_low):
        k_norm = kn_ref[(bi * A_HEADS + hd) * nb + j]
        alibi = jnp.where(j == i, 0.0, slope * (t - (i - j) * t).astype(F32))
        return (q_norm * k_norm + alibi + BOUND_MARGIN) - m_low

    def all_fast(j, size, m_low):
        ok = jnp.bool_(True)
        for u in range(size):
            reach = reach_of(j - u, m_low)
            ok = jnp.logical_and(ok, jnp.logical_and(reach >= SKIP_LOG2, reach <= FAST_LOG2))
        return ok

    plain = q_norm * kn_ref[(bi * A_HEADS + hd) * nb + i] + BOUND_MARGIN <= ZERO_REF_LOG2

    def start_plain():
        m_s[...] = jnp.zeros(m_s.shape, F32)
        acc_s[...] = jnp.zeros(acc_s.shape, F32)
        taken = jnp.int32(1)
        for size in reversed(A_GROUPS):
            fits = lax.cond(i + 1 >= size, functools.partial(all_fast, i, size, 0.0),
                            lambda: jnp.bool_(False))
            taken = jnp.where(fits, size, taken)
        lax.switch(sum((taken >= size).astype(jnp.int32) for size in A_GROUPS),
                   [functools.partial(fast_group, i, size, True)
                    for size in (1,) + tuple(reversed(A_GROUPS))])
        return i + 1 - taken, jnp.float32(0.0)

    def start_exact():
        exact_block(i, True)
        return i, jnp.min(m_s[...])

    left, m_low = lax.cond(plain, start_plain, start_exact)

    def single(j, m_low):
        reach = reach_of(j, m_low)

        def visit():
            def fast():
                fast_group(j, 1)
                return m_low

            def exact():
                exact_block(j, False)
                return jnp.min(m_s[...])

            return lax.cond(reach <= FAST_LOG2, fast, exact)

        return lax.cond(reach < SKIP_LOG2, lambda: m_low, visit)

    def grouped(size, left, m_low):
        def more(left):
            return lax.cond(left >= size, lambda: all_fast(left - 1, size, m_low),
                            lambda: jnp.bool_(False))

        def body(left):
            fast_group(left - 1, size)
            return left - size

        return lax.while_loop(more, body, left)

    for size in A_GROUPS:
        left = grouped(size, left, m_low)
    lax.fori_loop(0, left, lambda u, m: single(left - 1 - u, m), m_low)

    outs = [acc_s[a, :A_V, :] * (1.0 / acc_s[a, A_V:A_V + 1, :]) for a in range(2)]
    o_ref[0, 0] = outs[0] - lam * outs[1]


def _diff_attn(qt, k, vt, k_norms, q_norms, lq1, lk1, lq2, lk2, lam_init):
    b, s, _ = k.shape
    t = A_TILE
    nb = s // t
    slopes = jnp.asarray([_slopes_log2(hd)[0] for hd in range(A_HEADS)], F32)
    smem = pl.BlockSpec(memory_space=pltpu.SMEM)
    vec = pl.BlockSpec((1, A_QK), lambda bi, h, i: (0, 0))
    return pl.pallas_call(
        functools.partial(_diff_attn_kernel, t=t, nb=nb, lam_init=lam_init),
        grid=(b, A_HEADS, nb),
        in_specs=[
            smem, smem, smem, vec, vec, vec, vec,
            pl.BlockSpec((1, 1, 1, LANES, t), lambda bi, h, i: (bi, h, i, 0, 0)),
            pl.BlockSpec((1, s, LANES), lambda bi, h, i: (bi, 0, h)),
            pl.BlockSpec((1, 1, nb, A_VT, t), lambda bi, h, i: (bi, h, 0, 0, 0)),
        ],
        out_specs=pl.BlockSpec((1, 1, A_V, t), lambda bi, h, i: (bi, h, 0, i)),
        out_shape=jax.ShapeDtypeStruct((b, A_HEADS, A_V, s), F32),
        scratch_shapes=[
            pltpu.VMEM((2, LANES, t), BF16),
            pltpu.VMEM((2, 1, t), F32),
            pltpu.VMEM((2, A_VT, t), F32),
        ],
        compiler_params=_params("parallel", "parallel", "arbitrary"),
        name="diff_attn",
    )(k_norms, q_norms, slopes, lq1.reshape(1, A_QK), lk1.reshape(1, A_QK), lq2.reshape(1, A_QK),
      lk2.reshape(1, A_QK), qt, k, vt)


def _sb_attn_kernel(q_ref, k_ref, v_ref, o_ref, r_s, acc_s, *, tq):
    i = pl.program_id(2)
    q = q_ref[0]
    lane = lax.broadcasted_iota(jnp.int32, (tq, LANES), 1)
    qm = [jnp.where(lane < SB_DIM, q, jnp.zeros_like(q)),
          jnp.where(lane >= SB_DIM, q, jnp.zeros_like(q))]
    row = lax.broadcasted_iota(jnp.int32, (tq, tq), 0)
    col = lax.broadcasted_iota(jnp.int32, (tq, tq), 1)
    strict = col < row
    later = (row > col).astype(BF16)

    def keys(j):
        return k_ref[0, pl.ds(pl.multiple_of(j * tq, tq), tq), :]

    def values(j):
        return v_ref[0, pl.ds(pl.multiple_of(j * tq, tq), tq), :]

    def log_weights(kj, h, diagonal):
        z = _nt_dot(qm[h], kj)
        log_beta = jnp.minimum(z, 0.0) - jnp.log2(1.0 + jnp.exp2(-jnp.abs(z)))
        log_1mb = log_beta - z
        if diagonal:
            log_1mb = jnp.where(strict, log_1mb, 0.0)
        after = _dot(jnp.concatenate(_split_bf16(log_1mb, 2), axis=1), later2)
        return log_beta + after, jnp.sum(log_1mb, axis=1, keepdims=True)

    later2 = jnp.concatenate([later, later], axis=0)
    prev = jnp.maximum(i - 1, 0)
    has_prev = (i > 0).astype(F32)
    no_prev = jnp.where(i > 0, 0.0, NEG)
    k_diag, k_prev = keys(i), keys(prev)
    v_both = jnp.concatenate([values(i), values(prev)], axis=0)
    for h in range(2):
        lw_d, tot_d = log_weights(k_diag, h, True)
        lw_p, tot_p = log_weights(k_prev, h, False)
        a_d = jnp.where(strict, jnp.exp2(lw_d), 0.0)
        a_p = jnp.exp2(lw_p + (tot_d + no_prev))
        acc_s[h] = _dot(jnp.concatenate([a_d.astype(BF16), a_p.astype(BF16)], axis=1), v_both)
        r_s[h] = tot_d + tot_p * has_prev

    def cond(carry):
        j, live = carry
        return jnp.logical_and(j >= 0, live)

    def body(carry):
        j, _ = carry
        kj, vj = keys(j), values(j)
        for h in range(2):
            lw, tot = log_weights(kj, h, False)
            run = r_s[h]
            acc_s[h] = acc_s[h] + _dot(jnp.exp2(lw + run).astype(BF16), vj)
            r_s[h] = run + tot
        return j - 1, jnp.max(r_s[...]) > SKIP_LOG2

    lax.while_loop(cond, body, (i - 2, jnp.max(r_s[...]) > SKIP_LOG2))
    o_ref[0] = jnp.where(lane < SB_DIM, acc_s[0], acc_s[1])


def _sb_attn(q, k, v):
    b, s, _ = q.shape
    tq = min(ATT_TILE, s)
    pairs = SB_W // LANES
    return pl.pallas_call(
        functools.partial(_sb_attn_kernel, tq=tq),
        grid=(b, pairs, s // tq),
        in_specs=[
            pl.BlockSpec((1, tq, LANES), lambda bi, p, i: (bi, i, p)),
            pl.BlockSpec((1, s, LANES), lambda bi, p, i: (bi, 0, p)),
            pl.BlockSpec((1, s, LANES), lambda bi, p, i: (bi, 0, p)),
        ],
        out_specs=pl.BlockSpec((1, tq, LANES), lambda bi, p, i: (bi, i, p)),
        out_shape=jax.ShapeDtypeStruct((b, s, SB_W), F32),
        scratch_shapes=[
            pltpu.VMEM((2, tq, 1), F32),
            pltpu.VMEM((2, tq, LANES), F32),
        ],
        compiler_params=_params("parallel", "parallel", "arbitrary"),
        name="sb_attn",
    )(q, k, v)


def _swa_attn_kernel(sinks_ref, bias_ref, q_ref, kc_ref, kp_ref, vc_ref, vp_ref, o_ref, *, tq):
    i = pl.program_id(1)
    half = tq // 2
    col = lax.broadcasted_iota(jnp.int32, (half, tq), 1)
    has_key = col >= jnp.where(i > 0, 0, half)
    lane = lax.broadcasted_iota(jnp.int32, (half, LANES), 1)

    kwin = [jnp.concatenate([kp_ref[0, half:, :], kc_ref[0, :half, :]], axis=0), kc_ref[0]]
    vwin = [jnp.concatenate([vp_ref[0, half:, :], vc_ref[0, :half, :]], axis=0), vc_ref[0]]
    for hf in range(2):
        rows = slice(hf * half, (hf + 1) * half)
        for t in range(SWA_REP):
            q = q_ref[0, rows, t * LANES:(t + 1) * LANES]
            outs = []
            for g in range(SWA_KV_HEADS):
                head = g * SWA_REP + t
                qg = jnp.where((lane // SWA_DIM) == g, q, jnp.zeros_like(q))
                s = _nt_dot(qg, kwin[hf]) + bias_ref[head]
                if hf == 0:
                    s = jnp.where(has_key, s, NEG)
                sink = sinks_ref[head] * LOG2E
                m = jnp.maximum(jnp.max(s, axis=1, keepdims=True), sink)
                p = jnp.exp2(s - m)
                den = jnp.sum(p, axis=1, keepdims=True) + jnp.exp2(sink - m)
                outs.append(_dot(p.astype(BF16), vwin[hf]) / den)
            o_ref[0, rows, t * LANES:(t + 1) * LANES] = jnp.where(lane < SWA_DIM, outs[0], outs[1])


def _swa_bias(tq):
    half = tq // 2
    dist = np.arange(half)[:, None] + half - np.arange(tq)[None, :]
    in_window = (dist >= 0) & (dist < WINDOW)
    slopes = _alibi_slopes(SWA_Q_HEADS) * LOG2E
    bias = np.where(in_window[None], -slopes[:, None, None] * dist[None], NEG)
    return jnp.asarray(bias, F32)


def _swa_attn(q, k, v, sinks):
    b, s, _ = q.shape
    tq = min(ATT_TILE, s)
    smem = pl.BlockSpec(memory_space=pltpu.SMEM)
    cur = pl.BlockSpec((1, tq, SWA_KV_W), lambda bi, i: (bi, i, 0))
    prev = pl.BlockSpec((1, tq, SWA_KV_W), lambda bi, i: (bi, jnp.maximum(i - 1, 0), 0))
    return pl.pallas_call(
        functools.partial(_swa_attn_kernel, tq=tq),
        grid=(b, s // tq),
        in_specs=[
            smem,
            pl.BlockSpec((SWA_Q_HEADS, tq // 2, tq), lambda bi, i: (0, 0, 0)),
            pl.BlockSpec((1, tq, SWA_Q_W), lambda bi, i: (bi, i, 0)),
            cur, prev, cur, prev,
        ],
        out_specs=pl.BlockSpec((1, tq, SWA_Q_W), lambda bi, i: (bi, i, 0)),
        out_shape=jax.ShapeDtypeStruct((b, s, SWA_Q_W), F32),
        compiler_params=_params("parallel", "parallel"),
        name="swa_attn",
    )(sinks.astype(F32), _swa_bias(tq), q, k, k, v, v)


def _mix_ffn_kernel(ya_ref, yb_ref, yc_ref, x_ref, ga_ref, gb_ref, gc_ref, ln2_ref, mod_ref,
                    wa_ref, wb_ref, wc_ref, wg_ref, wu_ref, wd_ref, fg_ref, o_ref,
                    *, a_scale, final):
    mixed = _dot(_rms(yb_ref[0], gb_ref[...]).astype(BF16), wb_ref[...])
    mixed += _dot(_rms(yc_ref[0], gc_ref[...]).astype(BF16), wc_ref[...])
    packed = []
    for h in range(0, A_HEADS, 2):
        pair = []
        for yt in (ya_ref[0, h], ya_ref[0, h + 1]):
            ms = jnp.mean(yt * yt, axis=0, keepdims=True)
            pair.append(yt * lax.rsqrt(ms + EPS))
        packed.append(jnp.concatenate(pair, axis=0).T)
    na = jnp.concatenate(packed, axis=1) * ga_ref[...] * a_scale
    mixed += _dot(na.astype(BF16), wa_ref[...])
    x1 = x_ref[0] + mod_ref[0, 2:3, :] * mixed
    h2 = (_rms(x1, ln2_ref[...]) * (1.0 + mod_ref[0, 4:5, :]) + mod_ref[0, 3:4, :]).astype(BF16)
    gate = _dot(h2, wg_ref[...])
    up = _dot(h2, wu_ref[...])
    act = gate * jax.nn.sigmoid(gate) * up
    out = x1 + mod_ref[0, 5:6, :] * _dot(act.astype(BF16), wd_ref[...])
    if final:
        out = _rms(out, fg_ref[...])
    o_ref[0] = out


def _mix_ffn(ya, yb, yc, x, ga, gb, gc, ln2, mod, wa, wb, wc, wg, wu, wd, fg, a_scale, final,
             layer):
    b, s, d = x.shape
    f = wg.shape[2]
    tm = min(FFN_TILE, s)
    row = lambda wd_: pl.BlockSpec((1, tm, wd_), lambda bi, i: (bi, i, 0))
    const = lambda r, cdim: pl.BlockSpec((r, cdim), lambda bi, i: (0, 0),
                                         pipeline_mode=pl.Buffered(1))
    stacked = lambda r, cdim: pl.BlockSpec((None, r, cdim), lambda bi, i: (layer, 0, 0),
                                           pipeline_mode=pl.Buffered(1))
    return pl.pallas_call(
        functools.partial(_mix_ffn_kernel, a_scale=a_scale, final=final),
        grid=(b, s // tm),
        in_specs=[
            pl.BlockSpec((1, A_HEADS, A_V, tm), lambda bi, i: (bi, 0, 0, i)),
            row(SB_W), row(SWA_Q_W), row(d),
            const(1, A_W), const(1, SB_W), const(1, SWA_Q_W), const(1, d),
            pl.BlockSpec((1, N_MOD, d), lambda bi, i: (bi, 0, 0)),
            const(A_W, d), const(SB_W, d), const(SWA_Q_W, d),
            stacked(d, f), stacked(d, f), stacked(f, d), const(1, d),
        ],
        out_specs=row(d),
        out_shape=jax.ShapeDtypeStruct((b, s, d), F32),
        compiler_params=_params("parallel", "parallel"),
        name="mix_ffn",
    )(ya, yb, yc, x, ga, gb, gc, ln2.reshape(1, d), mod, wa, wb, wc, wg, wu, wd, fg.reshape(1, d))


def _swa_perm():
    cols = []
    for t in range(SWA_REP):
        for g in range(SWA_KV_HEADS):
            head = g * SWA_REP + t
            cols.extend(range(head * SWA_DIM, (head + 1) * SWA_DIM))
    return jnp.asarray(cols, jnp.int32)


def _prep_w_in(w):
    bounds = np.cumsum((0, A_W, A_W, A_W, SB_W, SB_W, SB_W, SWA_Q_W, SWA_KV_W, SWA_KV_W))
    qa, ka, va, qb, kb, vb, qc, kc, vc = [w[:, bounds[n]:bounds[n + 1]] for n in range(9)]
    row = jnp.concatenate([
        ka, qb * (SB_DIM ** -0.5 * LOG2E), kb, vb,
        (qc * (SWA_DIM ** -0.5 * LOG2E))[:, _swa_perm()], kc, vc], axis=1)
    transposed = jnp.concatenate([
        qa * (A_QK ** -0.5 * LOG2E), va], axis=1).T
    return row.astype(BF16), transposed.astype(BF16)


def kernel(x, c, ln1_g, ln2_g, w_mod, b_mod, w_in, lam_q1, lam_k1, lam_q2, lam_k2, diff_norm_g,
           sb_norm_g, swa_norm_g, swa_sinks, w_out, w_gate, w_up, w_down, final_g):
    depth = w_in.shape[0]
    perm = _swa_perm()
    mod = _modulation(c, w_mod, b_mod)
    w_gate_b, w_up_b, w_down_b = w_gate.astype(BF16), w_up.astype(BF16), w_down.astype(BF16)
    for l in range(depth):
        lam_init = 0.8 - 0.6 * math.exp(-0.3 * l)
        w_row, w_t = _prep_w_in(w_in[l])
        ka, qb, kb, vb, qc, kc, vc, qta, vta, kn = _in_proj(x, ln1_g[l], mod[l], w_row, w_t)
        norms = kn[:, :, 0, :2 * A_HEADS].transpose(0, 2, 1)
        k_norms, q_norms = norms[:, :A_HEADS].reshape(-1), norms[:, A_HEADS:].reshape(-1)
        ya = _diff_attn(qta, ka, vta, k_norms, q_norms, lam_q1[l], lam_k1[l], lam_q2[l], lam_k2[l], lam_init)
        yb = _sb_attn(qb, kb, vb)
        yc = _swa_attn(qc, kc, vc, swa_sinks[l])
        wo = w_out[l].astype(BF16)
        x = _mix_ffn(
            ya, yb, yc, x,
            jnp.tile(diff_norm_g[l], A_HEADS).reshape(1, A_W),
            sb_norm_g[l].reshape(1, SB_W),
            swa_norm_g[l][perm].reshape(1, SWA_Q_W),
            ln2_g[l], mod[l],
            wo[:A_W], wo[A_W:A_W + SB_W], wo[A_W + SB_W:][perm],
            w_gate_b, w_up_b, w_down_b, final_g,
            1.0 - lam_init, l == depth - 1, l)
    return x
```
